```python
import math
import jax, jax.numpy as jnp
from jax import lax
import numpy as np

D_MODEL = 4096
BATCH = 2
SEQ = 4096
DEPTH = 4

MIX_WIDTH = D_MODEL
GROUP_WIDTH = MIX_WIDTH // 4
D_FF = (3 * D_MODEL) // 2
NORM_EPS = 1e-6
Q_BLOCK = 128

MLA_HEADS = GROUP_WIDTH // 128
MLA_Q_LORA = D_MODEL // 4
MLA_KV_LORA = D_MODEL // 8
MLA_NOPE = 128
MLA_ROPE = 64
MLA_V = GROUP_WIDTH // MLA_HEADS
ROPE_THETA = 10000.0

MLSTM_HEADS = 4
MLSTM_V = GROUP_WIDTH // MLSTM_HEADS
MLSTM_QK = MLSTM_V // 2
MLSTM_CONV = 5
MLSTM_CHUNK = 64

DIFF_HEADS = 8
DIFF_HEAD_DIM = GROUP_WIDTH // (2 * DIFF_HEADS)

SWA_HEADS = 16
SWA_KV_HEADS = 2
SWA_HEAD_DIM = GROUP_WIDTH // SWA_HEADS
SWA_WINDOW = 128

MLA_COLS = MLA_Q_LORA + MLA_KV_LORA + MLA_ROPE
MLSTM_COLS = 2 * MLSTM_HEADS * MLSTM_QK + 2 * MLSTM_HEADS * MLSTM_V + 4 * MLSTM_HEADS
DIFF_COLS = 3 * DIFF_HEADS * 2 * DIFF_HEAD_DIM
SWA_COLS = (SWA_HEADS + 2 * SWA_KV_HEADS) * SWA_HEAD_DIM
IN_COLS = MLA_COLS + MLSTM_COLS + DIFF_COLS + SWA_COLS

kernel_name = "hymba_style_hybrid_encoder"


def rms_norm(x, g):
    xf = x.astype(jnp.float32)
    y = xf * lax.rsqrt(jnp.mean(xf * xf, axis=-1, keepdims=True) + NORM_EPS)
    return (y * g.astype(jnp.float32)).astype(x.dtype)


def swiglu(h, w_gu, w_down):
    gate, up = jnp.split(h @ w_gu, 2, axis=-1)
    return (jax.nn.silu(gate) * up) @ w_down


def alibi_slopes(n):
    return 2.0 ** (-8.0 * jnp.arange(1, n + 1, dtype=jnp.float32) / n)


def rope(x, pos):
    half = x.shape[-1] // 2
    inv = ROPE_THETA ** (-jnp.arange(half, dtype=jnp.float32) / half)
    ang = pos.astype(jnp.float32)[:, None] * inv[None, :]
    cos = jnp.cos(ang)[None, :, None, :]
    sin = jnp.sin(ang)[None, :, None, :]
    xf = x.astype(jnp.float32)
    x1, x2 = xf[..., :half], xf[..., half:]
    return jnp.concatenate([x1 * cos - x2 * sin, x2 * cos + x1 * sin], axis=-1).astype(x.dtype)


def query_blocks(t):
    b, s = t.shape[:2]
    return jnp.moveaxis(t.reshape(b, s // Q_BLOCK, Q_BLOCK, *t.shape[2:]), 1, 0)


def merge_blocks(t):
    t = jnp.moveaxis(t, 0, 1)
    return t.reshape(t.shape[0], -1, *t.shape[3:])


def mla_mixer(z, q_norm, kv_norm, w_uq, w_ukv):
    b, s, _ = z.shape
    c_q, c_kv, k_pe = jnp.split(z, [MLA_Q_LORA, MLA_Q_LORA + MLA_KV_LORA], axis=-1)
    pos = jnp.arange(s)
    q = (rms_norm(c_q, q_norm) @ w_uq).reshape(b, s, MLA_HEADS, MLA_NOPE + MLA_ROPE)
    kv = (rms_norm(c_kv, kv_norm) @ w_ukv).reshape(b, s, MLA_HEADS, MLA_NOPE + MLA_V)
    k_nope, v = kv[..., :MLA_NOPE], kv[..., MLA_NOPE:]
    q = jnp.concatenate([q[..., :MLA_NOPE], rope(q[..., MLA_NOPE:], pos)], axis=-1)
    k_pe = rope(k_pe[:, :, None, :], pos)
    k = jnp.concatenate([k_nope, jnp.broadcast_to(k_pe, (b, s, MLA_HEADS, MLA_ROPE))], axis=-1)
    scale = (MLA_NOPE + MLA_ROPE) ** -0.5

    def block(qb):
        sc = jnp.einsum('bqhd,bkhd->bhqk', qb, k).astype(jnp.float32) * scale
        p = jax.nn.softmax(sc, axis=-1).astype(v.dtype)
        return jnp.einsum('bhqk,bkhd->bqhd', p, v)

    o = merge_blocks(lax.map(block, query_blocks(q)))
    return o.reshape(b, s, MLA_HEADS * MLA_V)


def centred_depthwise_conv(x, w, bias):
    c = x.shape[-1]
    pad = w.shape[0] // 2
    y = lax.conv_general_dilated(x, w[:, None, :].astype(x.dtype), window_strides=(1,),
                                 padding=((pad, pad),), dimension_numbers=('NWC', 'WIO', 'NWC'),
                                 feature_group_count=c)
    return y + bias.astype(x.dtype)


def mlstm_chunkwise(q, k, v, li, lf):
    b, h, s, dk = q.shape
    dv = v.shape[-1]
    L = MLSTM_CHUNK
    nc = s // L

    def chunks(t):
        return jnp.moveaxis(t.reshape(b, h, nc, L, *t.shape[3:]), 2, 0)

    qc, kc, vc, lic, lfc = chunks(q), chunks(k), chunks(v), chunks(li), chunks(lf)
    bc = jnp.cumsum(lfc, axis=-1)
    tri = jnp.tril(jnp.ones((L, L), dtype=bool))

    def step(carry, inp):
        C, n, m = carry
        qq, kk, vv, ii, bb = inp
        d_mat = jnp.where(tri, bb[..., :, None] - bb[..., None, :] + ii[..., None, :], -jnp.inf)
        m_inter = bb + m[..., None]
        m_t = jnp.maximum(m_inter, jnp.max(d_mat, axis=-1))
        w = jnp.exp(d_mat - m_t[..., None]) * jnp.einsum('bhtd,bhsd->bhts', qq, kk)
        inter = jnp.exp(m_inter - m_t)
        num = inter[..., None] * jnp.einsum('bhvd,bhtd->bhtv', C, qq) + jnp.einsum('bhts,bhsv->bhtv', w, vv)
        den = inter * jnp.einsum('bhd,bhtd->bht', n, qq) + jnp.sum(w, axis=-1)
        h_out = num / jnp.maximum(jnp.abs(den), jnp.exp(-m_t))[..., None]
        b_last = bb[..., -1]
        g = b_last[..., None] - bb + ii
        m_new = jnp.maximum(b_last + m, jnp.max(g, axis=-1))
        decay = jnp.exp(b_last + m - m_new)
        wgt = jnp.exp(g - m_new[..., None])
        C_new = decay[..., None, None] * C + jnp.einsum('bhsv,bhsd->bhvd', wgt[..., None] * vv, kk)
        n_new = decay[..., None] * n + jnp.einsum('bhs,bhsd->bhd', wgt, kk)
        return (C_new, n_new, m_new), h_out

    init = (jnp.zeros((b, h, dv, dk), jnp.float32), jnp.zeros((b, h, dk), jnp.float32),
            jnp.full((b, h), -jnp.inf, jnp.float32))
    _, hs = lax.scan(step, init, (qc, kc, vc, lic, bc))
    return jnp.moveaxis(hs, 0, 2).reshape(b, h, s, dv)


def mlstm_mixer(z, conv_w, conv_b, gate_b):
    b, s, _ = z.shape
    qk_w = 2 * MLSTM_HEADS * MLSTM_QK
    v_w = MLSTM_HEADS * MLSTM_V
    qk, v, o, gates = jnp.split(z, [qk_w, qk_w + v_w, qk_w + 2 * v_w], axis=-1)
    qk = jax.nn.silu(centred_depthwise_conv(qk, conv_w, conv_b))
    q, k = jnp.split(qk, 2, axis=-1)
    heads = lambda t, d: t.astype(jnp.float32).reshape(b, s, MLSTM_HEADS, d).transpose(0, 2, 1, 3)
    q = heads(q, MLSTM_QK)
    k = heads(k, MLSTM_QK) * (MLSTM_QK ** -0.5)
    v = heads(v, MLSTM_V)
    g = (gates.astype(jnp.float32).reshape(b, s, 4, MLSTM_HEADS) + gate_b.astype(jnp.float32)).transpose(0, 2, 3, 1)
    li_f, lf_f = g[:, 0], jax.nn.log_sigmoid(g[:, 1])
    li_b, lf_b = g[:, 2], jax.nn.log_sigmoid(g[:, 3])
    flip = lambda t: jnp.flip(t, axis=2)
    h_f = mlstm_chunkwise(q, k, v, li_f, lf_f)
    h_b = flip(mlstm_chunkwise(flip(q), flip(k), flip(v), flip(li_b), flip(lf_b)))
    h = (h_f + h_b).transpose(0, 2, 1, 3).reshape(b, s, MLSTM_HEADS * MLSTM_V)
    return (jax.nn.sigmoid(o.astype(jnp.float32)) * h).astype(z.dtype)


def diff_mixer(z, lam_vecs, subln, lam_init):
    b, s, _ = z.shape
    q, k, v = jnp.split(z, 3, axis=-1)
    q = q.reshape(b, s, DIFF_HEADS, 2, DIFF_HEAD_DIM)
    k = k.reshape(b, s, DIFF_HEADS, 2, DIFF_HEAD_DIM)
    v = v.reshape(b, s, DIFF_HEADS, 2 * DIFF_HEAD_DIM)
    k1, k2 = k[..., 0, :], k[..., 1, :]
    lv = lam_vecs.astype(jnp.float32)
    lam = jnp.exp(jnp.sum(lv[0] * lv[1])) - jnp.exp(jnp.sum(lv[2] * lv[3])) + lam_init
    slopes = alibi_slopes(DIFF_HEADS)
    kpos = jnp.arange(s)
    scale = DIFF_HEAD_DIM ** -0.5

    def block(args):
        q1b, q2b, start = args
        qpos = start + jnp.arange(Q_BLOCK)
        bias = -slopes[:, None, None] * jnp.abs(qpos[:, None] - kpos[None, :]).astype(jnp.float32)
        s1 = jnp.einsum('bqhd,bkhd->bhqk', q1b, k1).astype(jnp.float32) * scale + bias
        s2 = jnp.einsum('bqhd,bkhd->bhqk', q2b, k2).astype(jnp.float32) * scale + bias
        p = jax.nn.softmax(s1, axis=-1) - lam * jax.nn.softmax(s2, axis=-1)
        return jnp.einsum('bhqk,bkhd->bqhd', p.astype(v.dtype), v)

    starts = jnp.arange(s // Q_BLOCK, dtype=jnp.int32) * Q_BLOCK
    o = merge_blocks(lax.map(block, (query_blocks(q[..., 0, :]), query_blocks(q[..., 1, :]), starts)))
    o = rms_norm(o, subln) * (1.0 - lam_init)
    return o.reshape(b, s, DIFF_HEADS * 2 * DIFF_HEAD_DIM)


def swa_mixer(z, sink):
    b, s, _ = z.shape
    W = SWA_WINDOW
    nb = s // W
    rep = SWA_HEADS // SWA_KV_HEADS
    q_w = SWA_HEADS * SWA_HEAD_DIM
    kv_w = SWA_KV_HEADS * SWA_HEAD_DIM
    q, k, v = jnp.split(z, [q_w, q_w + kv_w], axis=-1)
    q = q.reshape(b, nb, W, SWA_KV_HEADS, rep, SWA_HEAD_DIM)

    def bands(t):
        t = t.reshape(b, s, SWA_KV_HEADS, SWA_HEAD_DIM)
        tp = jnp.pad(t, ((0, 0), (W, W), (0, 0), (0, 0))).reshape(b, nb + 2, W, SWA_KV_HEADS, SWA_HEAD_DIM)
        return jnp.concatenate([tp[:, :-2], tp[:, 1:-1], tp[:, 2:]], axis=2)

    kb, vb = bands(k), bands(v)
    qpos = jnp.arange(nb)[:, None] * W + jnp.arange(W)[None, :]
    kpos = (jnp.arange(nb)[:, None] - 1) * W + jnp.arange(3 * W)[None, :]
    rel = jnp.abs(qpos[:, :, None] - kpos[:, None, :])
    valid = (rel <= W) & (kpos >= 0)[:, None, :] & (kpos < s)[:, None, :]
    slopes = alibi_slopes(SWA_HEADS).reshape(SWA_KV_HEADS, rep)
    bias = jnp.where(valid[None, :, None, None],
                     -slopes[None, None, :, :, None, None] * rel[None, :, None, None].astype(jnp.float32),
                     -jnp.inf)
    sc = jnp.einsum('bnqgrd,bnkgd->bngrqk', q, kb).astype(jnp.float32) * (SWA_HEAD_DIM ** -0.5) + bias
    sink_l = jnp.broadcast_to(sink.astype(jnp.float32).reshape(SWA_KV_HEADS, rep)[None, None, :, :, None, None],
                              sc.shape[:-1] + (1,))
    p = jax.nn.softmax(jnp.concatenate([sc, sink_l], axis=-1), axis=-1)[..., :-1]
    o = jnp.einsum('bngrqk,bnkgd->bnqgrd', p.astype(vb.dtype), vb)
    return o.reshape(b, s, SWA_HEADS * SWA_HEAD_DIM)


def setup_inputs(seed: int = 0) -> dict:
    key = jax.random.key(seed)
    ks = jax.random.split(key, 20)
    f32 = jnp.float32
    nrm = lambda k, shape: jax.random.normal(k, shape, f32)
    dense = lambda k, shape, fan_in: nrm(k, shape) * (fan_in ** -0.5)
    gain = lambda k, shape: 1.0 + 0.05 * nrm(k, shape)
    forget_base = jnp.linspace(3.0, 6.0, MLSTM_HEADS, dtype=f32)
    zero_base = jnp.zeros((MLSTM_HEADS,), f32)
    gate_base = jnp.stack([zero_base, forget_base, zero_base, forget_base])
    return {
        "x": nrm(ks[0], (BATCH, SEQ, D_MODEL)),
        "norm_gains": gain(ks[1], (DEPTH, 6, D_MODEL)),
        "w_in": dense(ks[2], (DEPTH, D_MODEL, IN_COLS), D_MODEL),
        "mla_q_norm": gain(ks[3], (DEPTH, MLA_Q_LORA)),
        "mla_kv_norm": gain(ks[4], (DEPTH, MLA_KV_LORA)),
        "mla_w_uq": dense(ks[5], (DEPTH, MLA_Q_LORA, MLA_HEADS * (MLA_NOPE + MLA_ROPE)), MLA_Q_LORA),
        "mla_w_ukv": dense(ks[6], (DEPTH, MLA_KV_LORA, MLA_HEADS * (MLA_NOPE + MLA_V)), MLA_KV_LORA),
        "mlstm_conv_w": dense(ks[7], (DEPTH, MLSTM_CONV, 2 * MLSTM_HEADS * MLSTM_QK), MLSTM_CONV),
        "mlstm_conv_b": 0.01 * nrm(ks[8], (DEPTH, 2 * MLSTM_HEADS * MLSTM_QK)),
        "mlstm_gate_b": gate_base[None] + 0.1 * nrm(ks[9], (DEPTH, 4, MLSTM_HEADS)),
        "diff_lambda": 0.1 * nrm(ks[10], (DEPTH, 4, DIFF_HEAD_DIM)),
        "diff_subln": gain(ks[11], (DEPTH, 2 * DIFF_HEAD_DIM)),
        "swa_sink": 0.5 * nrm(ks[12], (DEPTH, SWA_HEADS)),
        "group_norm": gain(ks[13], (DEPTH, MIX_WIDTH)),
        "w_out": dense(ks[14], (DEPTH, MIX_WIDTH, D_MODEL), MIX_WIDTH),
        "ffn1_w_gu": dense(ks[15], (DEPTH, D_MODEL, 2 * D_FF), D_MODEL),
        "ffn1_w_down": dense(ks[16], (DEPTH, D_FF, D_MODEL), D_FF),
        "ffn2_w_gu": dense(ks[17], (DEPTH, D_MODEL, 2 * D_FF), D_MODEL),
        "ffn2_w_down": dense(ks[18], (DEPTH, D_FF, D_MODEL), D_FF),
    }


def reference(x, norm_gains, w_in, mla_q_norm, mla_kv_norm, mla_w_uq, mla_w_ukv, mlstm_conv_w,
              mlstm_conv_b, mlstm_gate_b, diff_lambda, diff_subln, swa_sink, group_norm, w_out,
              ffn1_w_gu, ffn1_w_down, ffn2_w_gu, ffn2_w_down):
    split_at = [MLA_COLS, MLA_COLS + MLSTM_COLS, MLA_COLS + MLSTM_COLS + DIFF_COLS]
    for l in range(DEPTH):
        g = norm_gains[l]
        h = swiglu(rms_norm(x, g[0]), ffn1_w_gu[l], ffn1_w_down[l])
        x = x + 0.5 * rms_norm(h, g[1])
        z = rms_norm(x, g[2]) @ w_in[l]
        z_a, z_b, z_c, z_d = jnp.split(z, split_at, axis=-1)
        lam_init = 0.8 - 0.6 * math.exp(-0.3 * l)
        y_a = mla_mixer(z_a, mla_q_norm[l], mla_kv_norm[l], mla_w_uq[l], mla_w_ukv[l])
        y_b = mlstm_mixer(z_b, mlstm_conv_w[l], mlstm_conv_b[l], mlstm_gate_b[l])
        y_c = diff_mixer(z_c, diff_lambda[l], diff_subln[l], lam_init)
        y_d = swa_mixer(z_d, swa_sink[l])
        gn = jnp.split(group_norm[l], 4)
        y = jnp.concatenate([rms_norm(y_a, gn[0]), rms_norm(y_b, gn[1]),
                             rms_norm(y_c, gn[2]), rms_norm(y_d, gn[3])], axis=-1) @ w_out[l]
        x = x + rms_norm(y, g[3])
        h = swiglu(rms_norm(x, g[4]), ffn2_w_gu[l], ffn2_w_down[l])
        x = x + 0.5 * rms_norm(h, g[5])
    return x
```

```python
import functools
import math

import jax
import jax.numpy as jnp
from jax import lax
from jax.experimental import pallas as pl
from jax.experimental.pallas import tpu as pltpu

F32 = jnp.float32
BF16 = jnp.bfloat16
NORM_EPS = 1e-6
ROPE_THETA = 10000.0
LANES = 128
VMEM_LIMIT_BYTES = 56 * 1024 * 1024

D_MODEL = 4096
GROUP_WIDTH = D_MODEL // 4
D_FF = (3 * D_MODEL) // 2
MLA_HEADS = GROUP_WIDTH // 128
MLA_Q_LORA = D_MODEL // 4
MLA_KV_LORA = D_MODEL // 8
MLA_NOPE = 128
MLA_ROPE = 64
MLA_V = GROUP_WIDTH // MLA_HEADS
MLSTM_HEADS = 4
MLSTM_V = GROUP_WIDTH // MLSTM_HEADS
MLSTM_QK = MLSTM_V // 2
MLSTM_CONV = 5
DIFF_HEADS = 8
DIFF_HEAD_DIM = GROUP_WIDTH // (2 * DIFF_HEADS)
SWA_HEADS = 16
SWA_KV_HEADS = 2
SWA_HEAD_DIM = GROUP_WIDTH // SWA_HEADS
SWA_WINDOW = 128
SWA_REP = SWA_HEADS // SWA_KV_HEADS

MLA_COLS = MLA_Q_LORA + MLA_KV_LORA + MLA_ROPE
MLSTM_COLS = 2 * MLSTM_HEADS * MLSTM_QK + 2 * MLSTM_HEADS * MLSTM_V + 4 * MLSTM_HEADS
DIFF_COLS = 3 * DIFF_HEADS * 2 * DIFF_HEAD_DIM
SWA_COLS = (SWA_HEADS + 2 * SWA_KV_HEADS) * SWA_HEAD_DIM

ZF_CQ = 0
ZF_MQK = 1024
ZF_MO = 2048
ZF_CKV = 3072
ZF_KPE = 3584
ZF_GATES = 3712
ZF_COLS = 3840
ZB_MV = 0
ZB_DQ = 1024
ZB_DK = 2048
ZB_DV = 3072
ZB_SQ = 4096
ZB_SK = 5120
ZB_SV = 5376
ZB_COLS = 5632


def _params(*sem):
    return pltpu.CompilerParams(dimension_semantics=sem, vmem_limit_bytes=VMEM_LIMIT_BYTES)


def _rms(x, g):
    return x * lax.rsqrt(jnp.mean(x * x, axis=-1, keepdims=True) + NORM_EPS) * g


def _sigmoid(x):
    return 1.0 / (1.0 + jnp.exp(-x))


_NT = (((1,), (1,)), ((), ()))


def _prenorm_kernel(x_ref, g_ref, o_ref):
    o_ref[...] = _rms(x_ref[...], g_ref[...]).astype(o_ref.dtype)


def prenorm(x, g, tm=512):
    t, d = x.shape
    return pl.pallas_call(
        _prenorm_kernel,
        grid=(t // tm,),
        in_specs=[pl.BlockSpec((tm, d), lambda i: (i, 0)), pl.BlockSpec((1, d), lambda i: (0, 0))],
        out_specs=pl.BlockSpec((tm, d), lambda i: (i, 0)),
        out_shape=jax.ShapeDtypeStruct((t, d), BF16),
        compiler_params=_params("parallel"),
        name="prenorm",
    )(x, g.reshape(1, d))


def _matmul_kernel(a_ref, w_ref, o_ref):
    o_ref[...] = jnp.dot(a_ref[...], w_ref[...], preferred_element_type=F32).astype(o_ref.dtype)


def matmul(a, w, out_dtype, tm, tn, name):
    m, k = a.shape
    n = w.shape[1]
    return pl.pallas_call(
        _matmul_kernel,
        grid=(m // tm, n // tn),
        in_specs=[pl.BlockSpec((tm, k), lambda i, j: (i, 0)), pl.BlockSpec((k, tn), lambda i, j: (0, j))],
        out_specs=pl.BlockSpec((tm, tn), lambda i, j: (i, j)),
        out_shape=jax.ShapeDtypeStruct((m, n), out_dtype),
        compiler_params=_params("parallel", "arbitrary"),
        name=name,
    )(a, w)


def _swiglu_kernel(a_ref, wg_ref, wu_ref, o_ref):
    a = a_ref[...]
    g = jnp.dot(a, wg_ref[...], preferred_element_type=F32)
    u = jnp.dot(a, wu_ref[...], preferred_element_type=F32)
    o_ref[...] = (g * _sigmoid(g) * u).astype(o_ref.dtype)


def swiglu_up(a, w_gu, tm=1024, tn=512):
    m, k = a.shape
    f = w_gu.shape[1] // 2
    nj = f // tn
    return pl.pallas_call(
        _swiglu_kernel,
        grid=(m // tm, nj),
        in_specs=[pl.BlockSpec((tm, k), lambda i, j: (i, 0)),
                  pl.BlockSpec((k, tn), lambda i, j: (0, j)),
                  pl.BlockSpec((k, tn), lambda i, j: (0, j + nj))],
        out_specs=pl.BlockSpec((tm, tn), lambda i, j: (i, j)),
        out_shape=jax.ShapeDtypeStruct((m, f), BF16),
        compiler_params=_params("parallel", "arbitrary"),
        name="swiglu_up",
    )(a, w_gu, w_gu)


def _resid_kernel(a_ref, w_ref, x_ref, gp_ref, gn_ref, xo_ref, xn_ref, acc_ref, *, coef):
    k = pl.program_id(1)

    @pl.when(k == 0)
    def _():
        acc_ref[...] = jnp.zeros_like(acc_ref)

    acc_ref[...] += jnp.dot(a_ref[...], w_ref[...], preferred_element_type=F32)

    @pl.when(k == pl.num_programs(1) - 1)
    def _():
        xnew = x_ref[...] + coef * _rms(acc_ref[...], gp_ref[...])
        xo_ref[...] = xnew
        xn_ref[...] = _rms(xnew, gn_ref[...]).astype(xn_ref.dtype)


def resid_update(a, w, x, g_post, g_next, coef, tm=256, tk=1024):
    m, kk = a.shape
    d = w.shape[1]
    return pl.pallas_call(
        functools.partial(_resid_kernel, coef=coef),
        grid=(m // tm, kk // tk),
        in_specs=[pl.BlockSpec((tm, tk), lambda i, k: (i, k)),
                  pl.BlockSpec((tk, d), lambda i, k: (k, 0)),
                  pl.BlockSpec((tm, d), lambda i, k: (i, 0)),
                  pl.BlockSpec((1, d), lambda i, k: (0, 0)),
                  pl.BlockSpec((1, d), lambda i, k: (0, 0))],
        out_specs=[pl.BlockSpec((tm, d), lambda i, k: (i, 0)),
                   pl.BlockSpec((tm, d), lambda i, k: (i, 0))],
        out_shape=[jax.ShapeDtypeStruct((m, d), F32), jax.ShapeDtypeStruct((m, d), BF16)],
        scratch_shapes=[pltpu.VMEM((tm, d), F32)],
        compiler_params=_params("parallel", "arbitrary"),
        name="resid_update",
    )(a, w, x, g_post.reshape(1, d), g_next.reshape(1, d))


def _group_norm_kernel(a_ref, b_ref, c_ref, d_ref, g_ref, o_ref):
    w = a_ref.shape[1]
    for n, r in enumerate((a_ref, b_ref, c_ref, d_ref)):
        o_ref[:, n * w:(n + 1) * w] = _rms(r[...], g_ref[:, n * w:(n + 1) * w]).astype(o_ref.dtype)


def group_rmsnorm(ya, yb, yc, yd, g, tm=512):
    t, w = ya.shape
    spec = pl.BlockSpec((tm, w), lambda i: (i, 0))
    return pl.pallas_call(
        _group_norm_kernel,
        grid=(t // tm,),
        in_specs=[spec, spec, spec, spec, pl.BlockSpec((1, 4 * w), lambda i: (0, 0))],
        out_specs=pl.BlockSpec((tm, 4 * w), lambda i: (i, 0)),
        out_shape=jax.ShapeDtypeStruct((t, 4 * w), BF16),
        compiler_params=_params("parallel"),
        name="group_norm",
    )(ya, yb, yc, yd, g.reshape(1, 4 * w))


def _mla_proj_kernel(cq_ref, ckv_ref, kpe_ref, qg_ref, kvg_ref, wq_ref, wkv_ref, cos_ref, sin_ref,
                     q_ref, k_ref, v_ref, cqn_ref, ckvn_ref, kper_ref):
    cos = cos_ref[...]
    sin = sin_ref[...]

    @pl.when(pl.program_id(1) == 0)
    def _():
        cqn_ref[...] = _rms(cq_ref[...], qg_ref[...]).astype(BF16)
        ckvn_ref[...] = _rms(ckv_ref[...], kvg_ref[...]).astype(BF16)
        kp = kpe_ref[...]
        kper_ref[...] = kp[:, :MLA_ROPE] * cos + kp[:, MLA_ROPE:] * sin

    qq = jnp.dot(cqn_ref[...], wq_ref[...], preferred_element_type=F32)
    q_ref[:, :MLA_NOPE] = qq[:, :MLA_NOPE].astype(q_ref.dtype)
    q_ref[:, MLA_NOPE:] = (qq[:, MLA_NOPE:MLA_NOPE + MLA_ROPE] * cos
                           + qq[:, MLA_NOPE + MLA_ROPE:] * sin).astype(q_ref.dtype)
    kv = jnp.dot(ckvn_ref[...], wkv_ref[...], preferred_element_type=F32)
    k_ref[:, :MLA_NOPE] = kv[:, :MLA_NOPE].astype(k_ref.dtype)
    k_ref[:, MLA_NOPE:] = kper_ref[...].astype(k_ref.dtype)
    v_ref[...] = kv[:, MLA_NOPE:].astype(v_ref.dtype)


def mla_proj(zf, q_gain, kv_gain, wq, wkv, cos, sin, b, s, tm=512):
    ns = s // tm
    dqk = MLA_NOPE + MLA_ROPE
    row = lambda i, h: (i, 0)
    out_map = lambda i, h: (i // ns, h, i % ns, 0)
    return pl.pallas_call(
        _mla_proj_kernel,
        grid=(b * ns, MLA_HEADS),
        in_specs=[pl.BlockSpec((tm, MLA_Q_LORA), lambda i, h: (i, ZF_CQ // MLA_Q_LORA)),
                  pl.BlockSpec((tm, MLA_KV_LORA), lambda i, h: (i, ZF_CKV // MLA_KV_LORA)),
                  pl.BlockSpec((tm, LANES), lambda i, h: (i, ZF_KPE // LANES)),
                  pl.BlockSpec((1, MLA_Q_LORA), lambda i, h: (0, 0)),
                  pl.BlockSpec((1, MLA_KV_LORA), lambda i, h: (0, 0)),
                  pl.BlockSpec((None, MLA_Q_LORA, 2 * LANES), lambda i, h: (h, 0, 0)),
                  pl.BlockSpec((None, MLA_KV_LORA, 2 * LANES), lambda i, h: (h, 0, 0)),
                  pl.BlockSpec((tm, MLA_ROPE), lambda i, h: (i % ns, 0)),
                  pl.BlockSpec((tm, MLA_ROPE), lambda i, h: (i % ns, 0))],
        out_specs=[pl.BlockSpec((None, None, tm, dqk), out_map),
                   pl.BlockSpec((None, None, tm, dqk), out_map),
                   pl.BlockSpec((None, None, tm, MLA_V), out_map)],
        out_shape=[jax.ShapeDtypeStruct((b, MLA_HEADS, s, dqk), BF16),
                   jax.ShapeDtypeStruct((b, MLA_HEADS, s, dqk), BF16),
                   jax.ShapeDtypeStruct((b, MLA_HEADS, s, MLA_V), BF16)],
        scratch_shapes=[pltpu.VMEM((tm, MLA_Q_LORA), BF16), pltpu.VMEM((tm, MLA_KV_LORA), BF16),
                        pltpu.VMEM((tm, MLA_ROPE), F32)],
        compiler_params=_params("parallel", "arbitrary"),
        name="mla_proj",
    )(zf, zf, zf, q_gain.reshape(1, -1), kv_gain.reshape(1, -1), wq, wkv, cos, sin)


def _mla_attn_kernel(q_ref, k_ref, v_ref, o_ref, *, scale):
    sc = lax.dot_general(q_ref[...], k_ref[...], _NT, preferred_element_type=F32) * scale
    p = jnp.exp(sc - jnp.max(sc, axis=-1, keepdims=True))
    l = jnp.sum(p, axis=-1, keepdims=True)
    o = jnp.dot(p.astype(BF16), v_ref[...], preferred_element_type=F32)
    o_ref[...] = (o / l).astype(o_ref.dtype)


def mla_attn(q, k, v, tq=256):
    b, h, s, dqk = q.shape
    nq = s // tq
    return pl.pallas_call(
        functools.partial(_mla_attn_kernel, scale=dqk ** -0.5),
        grid=(b, h, nq),
        in_specs=[pl.BlockSpec((None, None, tq, dqk), lambda b_, h_, i: (b_, h_, i, 0)),
                  pl.BlockSpec((None, None, s, dqk), lambda b_, h_, i: (b_, h_, 0, 0)),
                  pl.BlockSpec((None, None, s, MLA_V), lambda b_, h_, i: (b_, h_, 0, 0))],
        out_specs=pl.BlockSpec((tq, MLA_V), lambda b_, h_, i: (b_ * nq + i, h_)),
        out_shape=jax.ShapeDtypeStruct((b * s, h * MLA_V), F32),
        compiler_params=_params("parallel", "parallel", "arbitrary"),
        name="mla_attn",
    )(q, k, v)


def _diff_attn_kernel(slopes_ref, q_ref, k_ref, v_ref, lv_ref, sub_ref, o_ref, *, lam_init, tq):
    slope = slopes_ref[pl.program_id(1)]
    q = q_ref[...]
    k = k_ref[...]
    s = k.shape[0]
    d = DIFF_HEAD_DIM
    lane = lax.broadcasted_iota(jnp.int32, q.shape, 1)
    zero = jnp.zeros_like(q)
    q1 = jnp.where(lane < d, q, zero)
    q2 = jnp.where(lane >= d, q, zero)
    qpos = pl.program_id(2) * tq + lax.broadcasted_iota(jnp.int32, (tq, s), 0)
    kpos = lax.broadcasted_iota(jnp.int32, (tq, s), 1)
    bias = -slope * jnp.abs(qpos - kpos).astype(F32)
    scale = d ** -0.5
    s1 = lax.dot_general(q1, k, _NT, preferred_element_type=F32) * scale + bias
    s2 = lax.dot_general(q2, k, _NT, preferred_element_type=F32) * scale + bias
    e1 = jnp.exp(s1 - jnp.max(s1, axis=-1, keepdims=True))
    e2 = jnp.exp(s2 - jnp.max(s2, axis=-1, keepdims=True))
    lv = lv_ref[...]
    lam = (jnp.exp(jnp.sum(lv[0:1] * lv[1:2], axis=-1, keepdims=True))
           - jnp.exp(jnp.sum(lv[2:3] * lv[3:4], axis=-1, keepdims=True)) + lam_init)
    p = (e1 * (1.0 / jnp.sum(e1, axis=-1, keepdims=True))
         - e2 * (lam / jnp.sum(e2, axis=-1, keepdims=True)))
    o = jnp.dot(p.astype(BF16), v_ref[...], preferred_element_type=F32)
    o_ref[...] = (_rms(o, sub_ref[...]) * (1.0 - lam_init)).astype(o_ref.dtype)


def diff_attn(zb, lam_vecs, subln, lam_init, b, s, tq=128):
    nq = s // tq
    hd = 2 * DIFF_HEAD_DIM
    slopes = 2.0 ** (-8.0 * jnp.arange(1, DIFF_HEADS + 1, dtype=F32) / DIFF_HEADS)
    return pl.pallas_call(
        functools.partial(_diff_attn_kernel, lam_init=lam_init, tq=tq),
        grid=(b, DIFF_HEADS, nq),
        in_specs=[pl.BlockSpec(memory_space=pltpu.SMEM),
                  pl.BlockSpec((tq, hd), lambda b_, h, i: (b_ * nq + i, ZB_DQ // hd + h)),
                  pl.BlockSpec((s, hd), lambda b_, h, i: (b_, ZB_DK // hd + h)),
                  pl.BlockSpec((s, hd), lambda b_, h, i: (b_, ZB_DV // hd + h)),
                  pl.BlockSpec((4, DIFF_HEAD_DIM), lambda b_, h, i: (0, 0)),
                  pl.BlockSpec((1, hd), lambda b_, h, i: (0, 0))],
        out_specs=pl.BlockSpec((tq, hd), lambda b_, h, i: (b_ * nq + i, h)),
        out_shape=jax.ShapeDtypeStruct((b * s, DIFF_HEADS * hd), F32),
        compiler_params=_params("parallel", "parallel", "arbitrary"),
        name="diff_attn",
    )(slopes, zb, zb, zb, lam_vecs, subln.reshape(1, hd))


def _swa_kernel(sink_ref, q_ref, k_ref, v_ref, o_ref, *, seq):
    w = SWA_WINDOW
    d = SWA_HEAD_DIM
    n = pl.program_id(1)
    start = pl.multiple_of(jnp.clip((n - 1) * w, 0, seq - 3 * w), w)
    kband = k_ref[pl.ds(start, 3 * w), :]
    vband = v_ref[pl.ds(start, 3 * w), :]
    qpos = n * w + lax.broadcasted_iota(jnp.int32, (2 * w, 3 * w), 0) % w
    kpos = start + lax.broadcasted_iota(jnp.int32, (2 * w, 3 * w), 1)
    rel = jnp.abs(qpos - kpos)
    valid = rel <= w
    relf = rel.astype(F32)
    lane = lax.broadcasted_iota(jnp.int32, (w, 2 * d), 1)
    scale = d ** -0.5
    for pair in range(SWA_HEADS // 2):
        g = (2 * pair) // SWA_REP
        qp = q_ref[:, pair * 2 * d:(pair + 1) * 2 * d]
        zero = jnp.zeros_like(qp)
        qs = jnp.concatenate([jnp.where(lane < d, qp, zero), jnp.where(lane >= d, qp, zero)], axis=0)
        sc = lax.dot_general(qs, kband[:, g * 2 * d:(g + 1) * 2 * d], _NT,
                             preferred_element_type=F32) * scale
        row = lax.broadcasted_iota(jnp.int32, (2 * w, 1), 0)
        h0 = 2 * pair
        slope = jnp.where(row < w, 2.0 ** (-8.0 * (h0 + 1) / SWA_HEADS), 2.0 ** (-8.0 * (h0 + 2) / SWA_HEADS))
        sink = jnp.where(row < w, sink_ref[h0], sink_ref[h0 + 1])
        sc = jnp.where(valid, sc - slope * relf, -jnp.inf)
        m = jnp.maximum(jnp.max(sc, axis=-1, keepdims=True), sink)
        e = jnp.exp(sc - m)
        l = jnp.sum(e, axis=-1, keepdims=True) + jnp.exp(sink - m)
        p = (e * (1.0 / l)).astype(BF16)
        o = jnp.dot(p, vband[:, g * 2 * d:(g + 1) * 2 * d], preferred_element_type=F32)
        o_ref[:, pair * 2 * d:(pair + 1) * 2 * d] = jnp.where(lane < d, o[:w], o[w:]).astype(o_ref.dtype)


def swa_attn(zb, sink, b, s):
    w = SWA_WINDOW
    nb = s // w
    kvw = SWA_KV_HEADS * 2 * SWA_HEAD_DIM
    qw = SWA_HEADS * SWA_HEAD_DIM
    return pl.pallas_call(
        functools.partial(_swa_kernel, seq=s),
        grid=(b, nb),
        in_specs=[pl.BlockSpec(memory_space=pltpu.SMEM),
                  pl.BlockSpec((w, qw), lambda b_, n: (b_ * nb + n, ZB_SQ // qw)),
                  pl.BlockSpec((s, kvw), lambda b_, n: (b_, ZB_SK // kvw)),
                  pl.BlockSpec((s, kvw), lambda b_, n: (b_, ZB_SV // kvw))],
        out_specs=pl.BlockSpec((w, qw), lambda b_, n: (b_ * nb + n, 0)),
        out_shape=jax.ShapeDtypeStruct((b * s, qw), F32),
        compiler_params=_params("parallel", "arbitrary"),
        name="swa_attn",
    )(sink, zb, zb, zb)


def _conv_kernel(x_ref, w_ref, b_ref, o_ref, *, kscale):
    x = x_ref[...]
    s = x.shape[0]
    row = lax.broadcasted_iota(jnp.int32, x.shape, 0)
    pad = MLSTM_CONV // 2
    y = x * w_ref[pad:pad + 1, :] + b_ref[...]
    for j in range(MLSTM_CONV):
        off = j - pad
        if off == 0:
            continue
        shifted = pltpu.roll(x, (-off) % s, 0)
        ok = (row >= -off) if off < 0 else (row < s - off)
        y = y + jnp.where(ok, shifted, 0.0) * w_ref[j:j + 1, :]
    y = y * _sigmoid(y)
    y = y * jnp.where(pl.program_id(1) >= pl.num_programs(1) // 2, kscale, 1.0)
    o_ref[...] = y.astype(o_ref.dtype)


def mlstm_conv(zf, conv_w, conv_b, b, s, tc=256):
    c = conv_w.shape[1]
    nc = c // tc
    return pl.pallas_call(
        functools.partial(_conv_kernel, kscale=MLSTM_QK ** -0.5),
        grid=(b, nc),
        in_specs=[pl.BlockSpec((s, tc), lambda b_, j: (b_, ZF_MQK // tc + j)),
                  pl.BlockSpec((MLSTM_CONV, tc), lambda b_, j: (0, j)),
                  pl.BlockSpec((1, tc), lambda b_, j: (0, j))],
        out_specs=pl.BlockSpec((s, tc), lambda b_, j: (b_, j)),
        out_shape=jax.ShapeDtypeStruct((b * s, c), BF16),
        compiler_params=_params("parallel", "arbitrary"),
        name="mlstm_conv",
    )(zf, conv_w, conv_b.reshape(1, c))


def _scan(x, row, op, fill, reverse):
    s = x.shape[0]
    d = 1
    while d < s:
        if reverse:
            shifted = jnp.where(row < s - d, pltpu.roll(x, s - d, 0), fill)
        else:
            shifted = jnp.where(row >= d, pltpu.roll(x, d, 0), fill)
        x = op(x, shifted)
        d *= 2
    return x


def _gate_kernel(g_ref, gb_ref, col_ref, row_ref):
    nh = MLSTM_HEADS
    g = g_ref[...] + gb_ref[...]
    ls = jnp.minimum(g, 0.0) - jnp.log1p(jnp.exp(-jnp.abs(g)))
    row = lax.broadcasted_iota(jnp.int32, g.shape, 0)
    lane = lax.broadcasted_iota(jnp.int32, g.shape, 1)
    cum_f = pltpu.roll(_scan(ls, row, jnp.add, 0.0, False), LANES - nh, 1)
    cum_b = pltpu.roll(_scan(ls, row, jnp.add, 0.0, True), LANES - nh, 1)
    a_f = g - cum_f
    a_b = g - cum_b
    top_f = _scan(a_f, row, jnp.maximum, -jnp.inf, False)
    top_b = _scan(a_b, row, jnp.maximum, -jnp.inf, True)
    col = jnp.where(lane < nh, top_f, jnp.where(lane < 2 * nh, pltpu.roll(cum_f + top_f, nh, 1),
                    jnp.where(lane < 3 * nh, top_b, pltpu.roll(cum_b + top_b, nh, 1))))
    col_ref[...] = col
    src = jnp.where(lane < 2 * nh, a_f, a_b)
    row_ref[...] = src.T[:4 * nh, :]


def mlstm_gates(zf, gate_b, b, s):
    gb = jnp.zeros((1, LANES), F32).at[0, :4 * MLSTM_HEADS].set(gate_b.reshape(-1))
    return pl.pallas_call(
        _gate_kernel,
        grid=(b,),
        in_specs=[pl.BlockSpec((s, LANES), lambda b_: (b_, ZF_GATES // LANES)),
                  pl.BlockSpec((1, LANES), lambda b_: (0, 0))],
        out_specs=[pl.BlockSpec((s, LANES), lambda b_: (b_, 0)),
                   pl.BlockSpec((None, 4 * MLSTM_HEADS, s), lambda b_: (b_, 0, 0))],
        out_shape=[jax.ShapeDtypeStruct((b * s, LANES), F32),
                   jax.ShapeDtypeStruct((b, 4 * MLSTM_HEADS, s), F32)],
        compiler_params=_params("parallel"),
        name="mlstm_gates",
    )(zf, gb)


def _mlstm_attn_kernel(q_ref, k_ref, v_ref, col_ref, row_ref, og_ref, o_ref, num_ref, *, tq, nblk):
    nh = MLSTM_HEADS
    dk, dv = MLSTM_QK, MLSTM_V
    i = pl.program_id(1)
    tpos = lax.broadcasted_iota(jnp.int32, (tq, tq), 0)
    spos = lax.broadcasted_iota(jnp.int32, (tq, tq), 1)
    for h in range(nh):
        q = q_ref[:, h * dk:(h + 1) * dk]
        col = col_ref[...]

        def block(j, a_row, top, mask):
            j0 = pl.multiple_of(j * tq, tq)
            kj = k_ref[pl.ds(j0, tq), h * dk:(h + 1) * dk]
            vj = v_ref[pl.ds(j0, tq), h * dv:(h + 1) * dv]
            sc = lax.dot_general(q, kj, _NT, preferred_element_type=F32)
            logw = row_ref[a_row:a_row + 1, pl.ds(j0, tq)] - top
            if mask is not None:
                logw = jnp.where(mask, logw, -jnp.inf)
            wgt = jnp.exp(logw) * sc
            return (jnp.dot(wgt.astype(BF16), vj, preferred_element_type=F32),
                    jnp.sum(wgt, axis=-1, keepdims=True))

        def direction(a_row, top, m, lo, hi, diag_mask):
            num_ref[...] = jnp.zeros_like(num_ref)

            def body(j, den):
                dn, dd = block(j, a_row, top, None)
                num_ref[...] += dn
                return den + dd

            den = lax.fori_loop(lo, hi, body, jnp.zeros((tq, 1), F32))
            dn, dd = block(i, a_row, top, diag_mask)
            num = num_ref[...] + dn
            den = den + dd
            return num / jnp.maximum(jnp.abs(den), jnp.exp(-m))

        h_f = direction(h, col[:, h:h + 1], col[:, nh + h:nh + h + 1], 0, i, tpos >= spos)
        h_b = direction(2 * nh + h, col[:, 2 * nh + h:2 * nh + h + 1], col[:, 3 * nh + h:3 * nh + h + 1],
                        i + 1, nblk, tpos <= spos)
        gate = _sigmoid(og_ref[:, h * dv:(h + 1) * dv])
        o_ref[:, h * dv:(h + 1) * dv] = (gate * (h_f + h_b)).astype(o_ref.dtype)


def mlstm_attn(qk, zb, zf, col, row, b, s, tq=256):
    nq = s // tq
    qw = MLSTM_HEADS * MLSTM_QK
    vw = MLSTM_HEADS * MLSTM_V
    return pl.pallas_call(
        functools.partial(_mlstm_attn_kernel, tq=tq, nblk=nq),
        grid=(b, nq),
        in_specs=[pl.BlockSpec((tq, qw), lambda b_, i: (b_ * nq + i, 0)),
                  pl.BlockSpec((s, qw), lambda b_, i: (b_, 1)),
                  pl.BlockSpec((s, vw), lambda b_, i: (b_, ZB_MV // vw)),
                  pl.BlockSpec((tq, LANES), lambda b_, i: (b_ * nq + i, 0)),
                  pl.BlockSpec((None, 4 * MLSTM_HEADS, s), lambda b_, i: (b_, 0, 0)),
                  pl.BlockSpec((tq, vw), lambda b_, i: (b_ * nq + i, ZF_MO // vw))],
        out_specs=pl.BlockSpec((tq, vw), lambda b_, i: (b_ * nq + i, 0)),
        out_shape=jax.ShapeDtypeStruct((b * s, vw), F32),
        scratch_shapes=[pltpu.VMEM((tq, MLSTM_V), F32)],
        compiler_params=_params("parallel", "arbitrary"),
        name="mlstm_attn",
    )(qk, qk, zb, col, row, zf)


def _rot_half_cols(w):
    half = w.shape[-1] // 2
    return jnp.concatenate([-w[..., half:], w[..., :half]], axis=-1)


def _split_w_in(w):
    a0 = 0
    b0 = MLA_COLS
    c0 = b0 + MLSTM_COLS
    d0 = c0 + DIFF_COLS
    qk_w = 2 * MLSTM_HEADS * MLSTM_QK
    v_w = MLSTM_HEADS * MLSTM_V
    kpe = w[:, a0 + MLA_Q_LORA + MLA_KV_LORA:a0 + MLA_COLS]
    gates = w[:, b0 + qk_w + 2 * v_w:c0]
    pad = jnp.zeros((w.shape[0], LANES - gates.shape[1]), w.dtype)
    wf = jnp.concatenate([
        w[:, a0:a0 + MLA_Q_LORA],
        w[:, b0:b0 + qk_w],
        w[:, b0 + qk_w + v_w:b0 + qk_w + 2 * v_w],
        w[:, a0 + MLA_Q_LORA:a0 + MLA_Q_LORA + MLA_KV_LORA],
        kpe, _rot_half_cols(kpe), gates, pad], axis=1)
    q_w = SWA_HEADS * SWA_HEAD_DIM
    kv_w = SWA_KV_HEADS * SWA_HEAD_DIM

    def dup(t):
        t = t.reshape(t.shape[0], SWA_KV_HEADS, 1, SWA_HEAD_DIM)
        return jnp.broadcast_to(t, (t.shape[0], SWA_KV_HEADS, 2, SWA_HEAD_DIM)).reshape(t.shape[0], -1)

    wb = jnp.concatenate([
        w[:, b0 + qk_w:b0 + qk_w + v_w],
        w[:, c0:d0],
        w[:, d0:d0 + q_w],
        dup(w[:, d0 + q_w:d0 + q_w + kv_w]),
        dup(w[:, d0 + q_w + kv_w:d0 + q_w + 2 * kv_w])], axis=1)
    assert wf.shape[1] == ZF_COLS and wb.shape[1] == ZB_COLS
    return wf.astype(BF16), wb.astype(BF16)


def _mla_weights(w_uq, w_ukv):
    wq = w_uq.reshape(MLA_Q_LORA, MLA_HEADS, MLA_NOPE + MLA_ROPE)
    pe = wq[..., MLA_NOPE:]
    wq = jnp.concatenate([wq[..., :MLA_NOPE], pe, _rot_half_cols(pe)], axis=-1)
    wkv = w_ukv.reshape(MLA_KV_LORA, MLA_HEADS, MLA_NOPE + MLA_V)
    return wq.transpose(1, 0, 2).astype(BF16), wkv.transpose(1, 0, 2).astype(BF16)


def _rope_tables(s):
    half = MLA_ROPE // 2
    inv = ROPE_THETA ** (-jnp.arange(half, dtype=F32) / half)
    ang = jnp.arange(s).astype(F32)[:, None] * inv[None, :]
    cos, sin = jnp.cos(ang), jnp.sin(ang)
    return jnp.concatenate([cos, cos], axis=-1), jnp.concatenate([sin, sin], axis=-1)


def mixers(zf, zb, l, b, s, cos, sin, mla_q_norm, mla_kv_norm, mla_w_uq, mla_w_ukv, mlstm_conv_w,
           mlstm_conv_b, mlstm_gate_b, diff_lambda, diff_subln, swa_sink):
    wq, wkv = _mla_weights(mla_w_uq, mla_w_ukv)
    q, k, v = mla_proj(zf, mla_q_norm, mla_kv_norm, wq, wkv, cos, sin, b, s)
    y_a = mla_attn(q, k, v)
    qk = mlstm_conv(zf, mlstm_conv_w, mlstm_conv_b, b, s)
    col, row = mlstm_gates(zf, mlstm_gate_b, b, s)
    y_b = mlstm_attn(qk, zb, zf, col, row, b, s)
    lam_init = 0.8 - 0.6 * math.exp(-0.3 * l)
    y_c = diff_attn(zb, diff_lambda, diff_subln, lam_init, b, s)
    y_d = swa_attn(zb, swa_sink, b, s)
    return y_a, y_b, y_c, y_d


def kernel(x, norm_gains, w_in, mla_q_norm, mla_kv_norm, mla_w_uq, mla_w_ukv, mlstm_conv_w,
           mlstm_conv_b, mlstm_gate_b, diff_lambda, diff_subln, swa_sink, group_norm, w_out,
           ffn1_w_gu, ffn1_w_down, ffn2_w_gu, ffn2_w_down):
    b, s, d = x.shape
    depth = w_in.shape[0]
    cos, sin = _rope_tables(s)
    x = x.reshape(b * s, d)
    xn = prenorm(x, norm_gains[0, 0])
    for l in range(depth):
        g = norm_gains[l]
        h = swiglu_up(xn, ffn1_w_gu[l].astype(BF16))
        x, xn = resid_update(h, ffn1_w_down[l].astype(BF16), x, g[1], g[2], 0.5)
        wf, wb = _split_w_in(w_in[l])
        zf = matmul(xn, wf, F32, 1024, 768, "proj_f32")
        zb = matmul(xn, wb, BF16, 1024, 512, "proj_bf16")
        ys = mixers(zf, zb, l, b, s, cos, sin, mla_q_norm[l], mla_kv_norm[l], mla_w_uq[l], mla_w_ukv[l],
                    mlstm_conv_w[l], mlstm_conv_b[l], mlstm_gate_b[l], diff_lambda[l], diff_subln[l],
                    swa_sink[l])
        yn = group_rmsnorm(*ys, group_norm[l])
        x, xn = resid_update(yn, w_out[l].astype(BF16), x, g[3], g[4], 1.0)
        h = swiglu_up(xn, ffn2_w_gu[l].astype(BF16))
        g_next = norm_gains[l + 1, 0] if l + 1 < depth else g[5]
        x, xn = resid_update(h, ffn2_w_down[l].astype(BF16), x, g[5], g_next, 0.5)
    return x.reshape(b, s, d)
```

```python
import functools
import math

import jax
import jax.numpy as jnp
from jax import lax
from jax.experimental import pallas as pl
from jax.experimental.pallas import tpu as pltpu

F32 = jnp.float32
BF16 = jnp.bfloat16
NORM_EPS = 1e-6
ROPE_THETA = 10000.0
LOG2E = math.log2(math.e)
LANES = 128
ATTN_ROWS = 128
VMEM_LIMIT_BYTES = 60 * 1024 * 1024

D_MODEL = 4096
GROUP_WIDTH = D_MODEL // 4
D_FF = (3 * D_MODEL) // 2
MLA_HEADS = GROUP_WIDTH // 128
MLA_Q_LORA = D_MODEL // 4
MLA_KV_LORA = D_MODEL // 8
MLA_NOPE = 128
MLA_ROPE = 64
MLA_V = GROUP_WIDTH // MLA_HEADS
MLSTM_HEADS = 4
MLSTM_V = GROUP_WIDTH // MLSTM_HEADS
MLSTM_QK = MLSTM_V // 2
MLSTM_CONV = 5
DIFF_HEADS = 8
DIFF_HEAD_DIM = GROUP_WIDTH // (2 * DIFF_HEADS)
SWA_HEADS = 16
SWA_KV_HEADS = 2
SWA_HEAD_DIM = GROUP_WIDTH // SWA_HEADS
SWA_WINDOW = 128
SWA_REP = SWA_HEADS // SWA_KV_HEADS

MLA_Q_SCALE = (MLA_NOPE + MLA_ROPE) ** -0.5 * LOG2E
DIFF_Q_SCALE = DIFF_HEAD_DIM ** -0.5 * LOG2E

MLA_COLS = MLA_Q_LORA + MLA_KV_LORA + MLA_ROPE
MLSTM_COLS = 2 * MLSTM_HEADS * MLSTM_QK + 2 * MLSTM_HEADS * MLSTM_V + 4 * MLSTM_HEADS
DIFF_COLS = 3 * DIFF_HEADS * 2 * DIFF_HEAD_DIM
SWA_COLS = (SWA_HEADS + 2 * SWA_KV_HEADS) * SWA_HEAD_DIM

ZF_CQ = 0
ZF_MQK = 1024
ZF_MO = 2048
ZF_CKV = 3072
ZF_KPE = 3584
ZF_GATES = 3712
ZF_COLS = 3840
ZB_MV = 0
ZB_DQ = 1024
ZB_DK = 2048
ZB_DV = 3072
ZB_SQ = 4096
ZB_SK = 5120
ZB_SV = 5376
ZB_COLS = 5632


def _params(*sem):
    return pltpu.CompilerParams(dimension_semantics=sem, vmem_limit_bytes=VMEM_LIMIT_BYTES)


def _rms(x, g):
    return x * lax.rsqrt(jnp.mean(x * x, axis=-1, keepdims=True) + NORM_EPS) * g


def _sigmoid(x):
    return 1.0 / (1.0 + jnp.exp(-x))


_NT = (((1,), (1,)), ((), ()))


def _prenorm_kernel(x_ref, g_ref, o_ref):
    o_ref[...] = _rms(x_ref[...], g_ref[...]).astype(o_ref.dtype)


def prenorm(x, g, tm=512):
    t, d = x.shape
    return pl.pallas_call(
        _prenorm_kernel,
        grid=(t // tm,),
        in_specs=[pl.BlockSpec((tm, d), lambda i: (i, 0)), pl.BlockSpec((1, d), lambda i: (0, 0))],
        out_specs=pl.BlockSpec((tm, d), lambda i: (i, 0)),
        out_shape=jax.ShapeDtypeStruct((t, d), BF16),
        compiler_params=_params("parallel"),
        name="prenorm",
    )(x, g.reshape(1, d))


def _matmul_kernel(a_ref, w_ref, o_ref):
    o_ref[...] = jnp.dot(a_ref[...], w_ref[...], preferred_element_type=F32).astype(o_ref.dtype)


def matmul(a, w, out_dtype, tm, tn, name):
    m, k = a.shape
    n = w.shape[1]
    return pl.pallas_call(
        _matmul_kernel,
        grid=(m // tm, n // tn),
        in_specs=[pl.BlockSpec((tm, k), lambda i, j: (i, 0)), pl.BlockSpec((k, tn), lambda i, j: (0, j))],
        out_specs=pl.BlockSpec((tm, tn), lambda i, j: (i, j)),
        out_shape=jax.ShapeDtypeStruct((m, n), out_dtype),
        compiler_params=_params("parallel", "arbitrary"),
        name=name,
    )(a, w)


def _swiglu_kernel(a_ref, wg_ref, wu_ref, o_ref, wgb_ref, wub_ref):
    @pl.when(pl.program_id(1) == 0)
    def _():
        wgb_ref[...] = wg_ref[...].astype(BF16)
        wub_ref[...] = wu_ref[...].astype(BF16)

    a = a_ref[...]
    g = jnp.dot(a, wgb_ref[...], preferred_element_type=F32)
    u = jnp.dot(a, wub_ref[...], preferred_element_type=F32)
    o_ref[...] = (g * _sigmoid(g) * u).astype(o_ref.dtype)


def swiglu_up(a, w_gu, l, tm=512, tn=512):
    m, k = a.shape
    f = w_gu.shape[2] // 2
    nj = f // tn
    return pl.pallas_call(
        _swiglu_kernel,
        grid=(nj, m // tm),
        in_specs=[pl.BlockSpec((tm, k), lambda j, i: (i, 0)),
                  pl.BlockSpec((None, k, tn), lambda j, i: (l, 0, j)),
                  pl.BlockSpec((None, k, tn), lambda j, i: (l, 0, j + nj))],
        out_specs=pl.BlockSpec((tm, tn), lambda j, i: (i, j)),
        out_shape=jax.ShapeDtypeStruct((m, f), BF16),
        scratch_shapes=[pltpu.VMEM((k, tn), BF16), pltpu.VMEM((k, tn), BF16)],
        compiler_params=_params("parallel", "arbitrary"),
        name="swiglu_up",
    )(a, w_gu, w_gu)


def _resid_kernel(a_ref, w_ref, x_ref, gp_ref, gn_ref, xo_ref, xn_ref, *, coef):
    k = pl.program_id(1)

    @pl.when(k == 0)
    def _():
        xo_ref[...] = jnp.zeros_like(xo_ref)

    xo_ref[...] += jnp.dot(a_ref[...], w_ref[...], preferred_element_type=F32)

    @pl.when(k == pl.num_programs(1) - 1)
    def _():
        xnew = x_ref[...] + coef * _rms(xo_ref[...], gp_ref[...])
        xo_ref[...] = xnew
        xn_ref[...] = _rms(xnew, gn_ref[...]).astype(xn_ref.dtype)


def resid_update(a, w, x, g_post, g_next, coef, tm=512, tk=512):
    m, kk = a.shape
    d = w.shape[1]
    return pl.pallas_call(
        functools.partial(_resid_kernel, coef=coef),
        grid=(m // tm, kk // tk),
        in_specs=[pl.BlockSpec((tm, tk), lambda i, k: (i, k)),
                  pl.BlockSpec((tk, d), lambda i, k: (k, 0)),
                  pl.BlockSpec((tm, d), lambda i, k: (i, 0)),
                  pl.BlockSpec((1, d), lambda i, k: (0, 0)),
                  pl.BlockSpec((1, d), lambda i, k: (0, 0))],
        out_specs=[pl.BlockSpec((tm, d), lambda i, k: (i, 0)),
                   pl.BlockSpec((tm, d), lambda i, k: (i, 0))],
        out_shape=[jax.ShapeDtypeStruct((m, d), F32), jax.ShapeDtypeStruct((m, d), BF16)],
        compiler_params=_params("parallel", "arbitrary"),
        name="resid_update",
    )(a, w, x, g_post.reshape(1, d), g_next.reshape(1, d))


def _group_norm_kernel(a_ref, b_ref, c_ref, d_ref, g_ref, o_ref):
    w = a_ref.shape[1]
    for n, r in enumerate((a_ref, b_ref, c_ref, d_ref)):
        o_ref[:, n * w:(n + 1) * w] = _rms(r[...], g_ref[:, n * w:(n + 1) * w]).astype(o_ref.dtype)


def group_rmsnorm(ya, yb, yc, yd, g, tm=512):
    t, w = ya.shape
    spec = pl.BlockSpec((tm, w), lambda i: (i, 0))
    return pl.pallas_call(
        _group_norm_kernel,
        grid=(t // tm,),
        in_specs=[spec, spec, spec, spec, pl.BlockSpec((1, 4 * w), lambda i: (0, 0))],
        out_specs=pl.BlockSpec((tm, 4 * w), lambda i: (i, 0)),
        out_shape=jax.ShapeDtypeStruct((t, 4 * w), BF16),
        compiler_params=_params("parallel"),
        name="group_norm",
    )(ya, yb, yc, yd, g.reshape(1, 4 * w))


def _mla_proj_kernel(cq_ref, ckv_ref, kpe_ref, qg_ref, kvg_ref, wq_ref, wkv_ref, cos_ref, sin_ref,
                     q_ref, k_ref, v_ref, cqn_ref, ckvn_ref, kper_ref):
    cos = cos_ref[...]
    sin = sin_ref[...]

    @pl.when(pl.program_id(1) == 0)
    def _():
        cqn_ref[...] = _rms(cq_ref[...], qg_ref[...]).astype(BF16)
        ckvn_ref[...] = _rms(ckv_ref[...], kvg_ref[...]).astype(BF16)
        kp = kpe_ref[...]
        kper_ref[...] = kp[:, :MLA_ROPE] * cos + kp[:, MLA_ROPE:] * sin

    qq = jnp.dot(cqn_ref[...], wq_ref[...], preferred_element_type=F32) * MLA_Q_SCALE
    q_ref[:, :MLA_NOPE] = qq[:, :MLA_NOPE].astype(q_ref.dtype)
    q_ref[:, MLA_NOPE:] = (qq[:, MLA_NOPE:MLA_NOPE + MLA_ROPE] * cos
                           + qq[:, MLA_NOPE + MLA_ROPE:] * sin).astype(q_ref.dtype)
    kv = jnp.dot(ckvn_ref[...], wkv_ref[...], preferred_element_type=F32)
    k_ref[:, :MLA_NOPE] = kv[:, :MLA_NOPE].astype(k_ref.dtype)
    k_ref[:, MLA_NOPE:] = kper_ref[...].astype(k_ref.dtype)
    v_ref[...] = kv[:, MLA_NOPE:].astype(v_ref.dtype)


def mla_proj(zf, q_gain, kv_gain, wq, wkv, cos, sin, b, s, tm=512):
    ns = s // tm
    dqk = MLA_NOPE + MLA_ROPE
    row = lambda i, h: (i, 0)
    out_map = lambda i, h: (i // ns, h, i % ns, 0)
    return pl.pallas_call(
        _mla_proj_kernel,
        grid=(b * ns, MLA_HEADS),
        in_specs=[pl.BlockSpec((tm, MLA_Q_LORA), lambda i, h: (i, ZF_CQ // MLA_Q_LORA)),
                  pl.BlockSpec((tm, MLA_KV_LORA), lambda i, h: (i, ZF_CKV // MLA_KV_LORA)),
                  pl.BlockSpec((tm, LANES), lambda i, h: (i, ZF_KPE // LANES)),
                  pl.BlockSpec((1, MLA_Q_LORA), lambda i, h: (0, 0)),
                  pl.BlockSpec((1, MLA_KV_LORA), lambda i, h: (0, 0)),
                  pl.BlockSpec((None, MLA_Q_LORA, 2 * LANES), lambda i, h: (h, 0, 0)),
                  pl.BlockSpec((None, MLA_KV_LORA, 2 * LANES), lambda i, h: (h, 0, 0)),
                  pl.BlockSpec((tm, MLA_ROPE), lambda i, h: (i % ns, 0)),
                  pl.BlockSpec((tm, MLA_ROPE), lambda i, h: (i % ns, 0))],
        out_specs=[pl.BlockSpec((None, None, tm, dqk), out_map),
                   pl.BlockSpec((None, None, tm, dqk), out_map),
                   pl.BlockSpec((None, None, tm, MLA_V), out_map)],
        out_shape=[jax.ShapeDtypeStruct((b, MLA_HEADS, s, dqk), BF16),
                   jax.ShapeDtypeStruct((b, MLA_HEADS, s, dqk), BF16),
                   jax.ShapeDtypeStruct((b, MLA_HEADS, s, MLA_V), BF16)],
        scratch_shapes=[pltpu.VMEM((tm, MLA_Q_LORA), BF16), pltpu.VMEM((tm, MLA_KV_LORA), BF16),
                        pltpu.VMEM((tm, MLA_ROPE), F32)],
        compiler_params=_params("parallel", "arbitrary"),
        name="mla_proj",
    )(zf, zf, zf, q_gain.reshape(1, -1), kv_gain.reshape(1, -1), wq, wkv, cos, sin)


def _key_chunk(s):
    return min(1024, s)


def _fill_v_ones(vaug_ref, v_ref):
    dv = v_ref.shape[1]
    vaug_ref[:, :dv] = v_ref[...]
    vaug_ref[:, dv:] = jnp.ones((v_ref.shape[0], vaug_ref.shape[1] - dv), vaug_ref.dtype)


def _softmax_pv(lhs, k_ref, vaug_ref, dv, bias_fn=None):
    chunk = _key_chunk(k_ref.shape[0])
    m = acc = None
    for c in range(k_ref.shape[0] // chunk):
        ks = slice(c * chunk, (c + 1) * chunk)
        sc = lax.dot_general(lhs, k_ref[ks, :], _NT, preferred_element_type=F32)
        if bias_fn is not None:
            sc = sc + bias_fn(c)
        top = jnp.max(sc, axis=-1, keepdims=True)
        m_new = top if m is None else jnp.maximum(m, top)
        pv = jnp.dot(jnp.exp2(sc - m_new).astype(BF16), vaug_ref[ks, :], preferred_element_type=F32)
        acc = pv if acc is None else acc * jnp.exp2(m - m_new) + pv
        m = m_new
    return acc[:, :dv] / acc[:, dv:]


def _mla_attn_kernel(q_ref, k_ref, v_ref, o_ref, vaug_ref):
    @pl.when(pl.program_id(2) == 0)
    def _():
        _fill_v_ones(vaug_ref, v_ref)

    o_ref[...] = _softmax_pv(q_ref[...], k_ref, vaug_ref, MLA_V).astype(o_ref.dtype)


def mla_attn(q, k, v, tq=1024):
    b, h, s, dqk = q.shape
    tq = min(tq, s)
    nq = s // tq
    return pl.pallas_call(
        _mla_attn_kernel,
        grid=(b, h, nq),
        in_specs=[pl.BlockSpec((None, None, tq, dqk), lambda b_, h_, i: (b_, h_, i, 0)),
                  pl.BlockSpec((None, None, s, dqk), lambda b_, h_, i: (b_, h_, 0, 0)),
                  pl.BlockSpec((None, None, s, MLA_V), lambda b_, h_, i: (b_, h_, 0, 0))],
        out_specs=pl.BlockSpec((tq, MLA_V), lambda b_, h_, i: (b_ * nq + i, h_)),
        out_shape=jax.ShapeDtypeStruct((b * s, h * MLA_V), F32),
        scratch_shapes=[pltpu.VMEM((s, 2 * MLA_V), BF16)],
        compiler_params=_params("parallel", "parallel", "arbitrary"),
        name="mla_attn",
    )(q, k, v)


def _diff_attn_kernel(q_ref, k_ref, v_ref, bias_ref, lv_ref, sub_ref, o_ref, vaug_ref, *, lam_init, rows):
    i = pl.program_id(2)
    tq = q_ref.shape[0]
    s = k_ref.shape[0]
    d = DIFF_HEAD_DIM

    @pl.when(i == 0)
    def _():
        _fill_v_ones(vaug_ref, v_ref)

    lv = lv_ref[...]
    lam = (jnp.exp(jnp.sum(lv[0:1] * lv[1:2], axis=-1, keepdims=True))
           - jnp.exp(jnp.sum(lv[2:3] * lv[3:4], axis=-1, keepdims=True)) + lam_init)
    q = q_ref[...]
    lane = lax.broadcasted_iota(jnp.int32, q.shape, 1)
    zero = jnp.zeros_like(q)
    lhs = jnp.concatenate([jnp.where(lane < d, q, zero), jnp.where(lane >= d, q, zero)], axis=0)

    def bias(c):
        chunk = _key_chunk(s)
        groups = [bias_ref[:, pl.ds(pl.multiple_of(s - rows - (i * tq + g * rows), rows) + c * chunk, chunk)]
                  for g in range(tq // rows)]
        return jnp.concatenate(groups + groups, axis=0)

    o = _softmax_pv(lhs, k_ref, vaug_ref, 2 * d, bias)
    o = o[:tq] - lam * o[tq:]
    o_ref[...] = (_rms(o, sub_ref[...]) * (1.0 - lam_init)).astype(o_ref.dtype)


def _alibi_table(n_heads, s, rows):
    slopes = 2.0 ** (-8.0 * jnp.arange(1, n_heads + 1, dtype=F32) / n_heads)
    r = jnp.arange(rows, dtype=jnp.int32)[:, None]
    c = jnp.arange(2 * s - rows, dtype=jnp.int32)[None, :]
    dist = jnp.abs(r - c + (s - rows)).astype(F32)
    return (-LOG2E * slopes)[:, None, None] * dist[None]


def diff_attn(zb, lam_vecs, subln, lam_init, bias_tbl, b, s, tq=512):
    tq = min(tq, s)
    nq = s // tq
    hd = 2 * DIFF_HEAD_DIM
    rows, width = bias_tbl.shape[1:]
    return pl.pallas_call(
        functools.partial(_diff_attn_kernel, lam_init=lam_init, rows=rows),
        grid=(b, DIFF_HEADS, nq),
        in_specs=[pl.BlockSpec((tq, hd), lambda b_, h, i: (b_ * nq + i, ZB_DQ // hd + h)),
                  pl.BlockSpec((s, hd), lambda b_, h, i: (b_, ZB_DK // hd + h)),
                  pl.BlockSpec((s, hd), lambda b_, h, i: (b_, ZB_DV // hd + h)),
                  pl.BlockSpec((None, rows, width), lambda b_, h, i: (h, 0, 0)),
                  pl.BlockSpec((4, DIFF_HEAD_DIM), lambda b_, h, i: (0, 0)),
                  pl.BlockSpec((1, hd), lambda b_, h, i: (0, 0))],
        out_specs=pl.BlockSpec((tq, hd), lambda b_, h, i: (b_ * nq + i, h)),
        out_shape=jax.ShapeDtypeStruct((b * s, DIFF_HEADS * hd), F32),
        scratch_shapes=[pltpu.VMEM((s, 2 * hd), BF16)],
        compiler_params=_params("parallel", "parallel", "arbitrary"),
        name="diff_attn",
    )(zb, zb, zb, bias_tbl, lam_vecs, subln.reshape(1, hd))


def _swa_kernel(sink_ref, q_ref, k_ref, v_ref, o_ref, *, seq):
    w = SWA_WINDOW
    d = SWA_HEAD_DIM
    n = pl.program_id(1)
    start = pl.multiple_of(jnp.clip((n - 1) * w, 0, seq - 3 * w), w)
    kband = k_ref[pl.ds(start, 3 * w), :]
    vband = v_ref[pl.ds(start, 3 * w), :]
    qpos = n * w + lax.broadcasted_iota(jnp.int32, (2 * w, 3 * w), 0) % w
    kpos = start + lax.broadcasted_iota(jnp.int32, (2 * w, 3 * w), 1)
    rel = jnp.abs(qpos - kpos)
    valid = rel <= w
    relf = rel.astype(F32)
    lane = lax.broadcasted_iota(jnp.int32, (w, 2 * d), 1)
    scale = d ** -0.5
    for pair in range(SWA_HEADS // 2):
        g = (2 * pair) // SWA_REP
        qp = q_ref[:, pair * 2 * d:(pair + 1) * 2 * d]
        zero = jnp.zeros_like(qp)
        qs = jnp.concatenate([jnp.where(lane < d, qp, zero), jnp.where(lane >= d, qp, zero)], axis=0)
        sc = lax.dot_general(qs, kband[:, g * 2 * d:(g + 1) * 2 * d], _NT,
                             preferred_element_type=F32) * scale
        row = lax.broadcasted_iota(jnp.int32, (2 * w, 1), 0)
        h0 = 2 * pair
        slope = jnp.where(row < w, 2.0 ** (-8.0 * (h0 + 1) / SWA_HEADS), 2.0 ** (-8.0 * (h0 + 2) / SWA_HEADS))
        sink = jnp.where(row < w, sink_ref[h0], sink_ref[h0 + 1])
        sc = jnp.where(valid, sc - slope * relf, -jnp.inf)
        m = jnp.maximum(jnp.max(sc, axis=-1, keepdims=True), sink)
        e = jnp.exp(sc - m)
        l = jnp.sum(e, axis=-1, keepdims=True) + jnp.exp(sink - m)
        p = (e * (1.0 / l)).astype(BF16)
        o = jnp.dot(p, vband[:, g * 2 * d:(g + 1) * 2 * d], preferred_element_type=F32)
        o_ref[:, pair * 2 * d:(pair + 1) * 2 * d] = jnp.where(lane < d, o[:w], o[w:]).astype(o_ref.dtype)


def swa_attn(zb, sink, b, s):
    w = SWA_WINDOW
    nb = s // w
    kvw = SWA_KV_HEADS * 2 * SWA_HEAD_DIM
    qw = SWA_HEADS * SWA_HEAD_DIM
    return pl.pallas_call(
        functools.partial(_swa_kernel, seq=s),
        grid=(b, nb),
        in_specs=[pl.BlockSpec(memory_space=pltpu.SMEM),
                  pl.BlockSpec((w, qw), lambda b_, n: (b_ * nb + n, ZB_SQ // qw)),
                  pl.BlockSpec((s, kvw), lambda b_, n: (b_, ZB_SK // kvw)),
                  pl.BlockSpec((s, kvw), lambda b_, n: (b_, ZB_SV // kvw))],
        out_specs=pl.BlockSpec((w, qw), lambda b_, n: (b_ * nb + n, 0)),
        out_shape=jax.ShapeDtypeStruct((b * s, qw), F32),
        compiler_params=_params("parallel", "arbitrary"),
        name="swa_attn",
    )(sink, zb, zb, zb)


def _conv_kernel(x_ref, w_ref, b_ref, o_ref, *, kscale):
    x = x_ref[...]
    s = x.shape[0]
    row = lax.broadcasted_iota(jnp.int32, x.shape, 0)
    pad = MLSTM_CONV // 2
    y = x * w_ref[pad:pad + 1, :] + b_ref[...]
    for j in range(MLSTM_CONV):
        off = j - pad
        if off == 0:
            continue
        shifted = pltpu.roll(x, (-off) % s, 0)
        ok = (row >= -off) if off < 0 else (row < s - off)
        y = y + jnp.where(ok, shifted, 0.0) * w_ref[j:j + 1, :]
    y = y * _sigmoid(y)
    y = y * jnp.where(pl.program_id(1) >= pl.num_programs(1) // 2, kscale, 1.0)
    o_ref[...] = y.astype(o_ref.dtype)


def mlstm_conv(zf, conv_w, conv_b, b, s, tc=256):
    c = conv_w.shape[1]
    nc = c // tc
    return pl.pallas_call(
        functools.partial(_conv_kernel, kscale=MLSTM_QK ** -0.5),
        grid=(b, nc),
        in_specs=[pl.BlockSpec((s, tc), lambda b_, j: (b_, ZF_MQK // tc + j)),
                  pl.BlockSpec((MLSTM_CONV, tc), lambda b_, j: (0, j)),
                  pl.BlockSpec((1, tc), lambda b_, j: (0, j))],
        out_specs=pl.BlockSpec((s, tc), lambda b_, j: (b_, j)),
        out_shape=jax.ShapeDtypeStruct((b * s, c), BF16),
        compiler_params=_params("parallel", "arbitrary"),
        name="mlstm_conv",
    )(zf, conv_w, conv_b.reshape(1, c))


def _scan(x, row, op, fill, reverse):
    s = x.shape[0]
    d = 1
    while d < s:
        if reverse:
            shifted = jnp.where(row < s - d, pltpu.roll(x, s - d, 0), fill)
        else:
            shifted = jnp.where(row >= d, pltpu.roll(x, d, 0), fill)
        x = op(x, shifted)
        d *= 2
    return x


def _gate_kernel(g_ref, gb_ref, col_ref, row_ref):
    nh = MLSTM_HEADS
    g = g_ref[...] + gb_ref[...]
    ls = jnp.minimum(g, 0.0) - jnp.log1p(jnp.exp(-jnp.abs(g)))
    row = lax.broadcasted_iota(jnp.int32, g.shape, 0)
    lane = lax.broadcasted_iota(jnp.int32, g.shape, 1)
    cum_f = pltpu.roll(_scan(ls, row, jnp.add, 0.0, False), LANES - nh, 1)
    cum_b = pltpu.roll(_scan(ls, row, jnp.add, 0.0, True), LANES - nh, 1)
    a_f = g - cum_f
    a_b = g - cum_b
    top_f = _scan(a_f, row, jnp.maximum, -jnp.inf, False)
    top_b = _scan(a_b, row, jnp.maximum, -jnp.inf, True)
    m_f = cum_f + top_f
    m_b = cum_b + top_b
    src_t = (jnp.where(lane < 2 * nh, a_f, a_b) * LOG2E).T
    sub = lax.broadcasted_iota(jnp.int32, row_ref.shape[1:], 0)
    for h in range(nh):
        col_ref[h] = jnp.where(lane == 0, top_f[:, h:h + 1] * LOG2E,
                     jnp.where(lane == 1, m_f[:, h:h + 1],
                     jnp.where(lane == 2, top_b[:, 2 * nh + h:2 * nh + h + 1] * LOG2E,
                               m_b[:, 2 * nh + h:2 * nh + h + 1])))
        row_ref[h] = jnp.where(sub == 0, src_t[h:h + 1, :],
                     jnp.where(sub == 1, src_t[2 * nh + h:2 * nh + h + 1, :], 0.0))


def mlstm_gates(zf, gate_b, b, s):
    gb = jnp.zeros((1, LANES), F32).at[0, :4 * MLSTM_HEADS].set(gate_b.reshape(-1))
    nh = MLSTM_HEADS
    return pl.pallas_call(
        _gate_kernel,
        grid=(b,),
        in_specs=[pl.BlockSpec((s, LANES), lambda b_: (b_, ZF_GATES // LANES)),
                  pl.BlockSpec((1, LANES), lambda b_: (0, 0))],
        out_specs=[pl.BlockSpec((None, nh, s, LANES), lambda b_: (b_, 0, 0, 0)),
                   pl.BlockSpec((None, nh, 8, s), lambda b_: (b_, 0, 0, 0))],
        out_shape=[jax.ShapeDtypeStruct((b, nh, s, LANES), F32),
                   jax.ShapeDtypeStruct((b, nh, 8, s), F32)],
        compiler_params=_params("parallel"),
        name="mlstm_gates",
    )(zf, gb)


def _mlstm_attn_kernel(q_ref, k_ref, v_ref, col_ref, row_ref, og_ref, o_ref):
    tq = q_ref.shape[0]
    s = k_ref.shape[0]
    sc = lax.dot_general(q_ref[...], k_ref[...], _NT, preferred_element_type=F32)
    rel = (lax.broadcasted_iota(jnp.int32, (tq, s), 1)
           - lax.broadcasted_iota(jnp.int32, (tq, s), 0) - pl.program_id(2) * tq)
    v = v_ref[...]
    col = col_ref[...]

    def direction(mask, a_src, top, m):
        wgt = jnp.where(mask, jnp.exp2(a_src - top), 0.0) * sc
        num = jnp.dot(wgt.astype(BF16), v, preferred_element_type=F32)
        den = jnp.sum(wgt, axis=-1, keepdims=True)
        return num / jnp.maximum(jnp.abs(den), jnp.exp(-m))

    h_f = direction(rel <= 0, row_ref[0:1, :], col[:, 0:1], col[:, 1:2])
    h_b = direction(rel >= 0, row_ref[1:2, :], col[:, 2:3], col[:, 3:4])
    o_ref[...] = (_sigmoid(og_ref[...]) * (h_f + h_b)).astype(o_ref.dtype)


def mlstm_attn(qk, zb, zf, col, row, b, s, tq=256):
    nq = s // tq
    nh, dk, dv = MLSTM_HEADS, MLSTM_QK, MLSTM_V
    return pl.pallas_call(
        _mlstm_attn_kernel,
        grid=(b, nh, nq),
        in_specs=[pl.BlockSpec((tq, dk), lambda b_, h, i: (b_ * nq + i, h)),
                  pl.BlockSpec((s, dk), lambda b_, h, i: (b_, nh + h)),
                  pl.BlockSpec((s, dv), lambda b_, h, i: (b_, ZB_MV // dv + h)),
                  pl.BlockSpec((None, None, tq, LANES), lambda b_, h, i: (b_, h, i, 0)),
                  pl.BlockSpec((None, None, 8, s), lambda b_, h, i: (b_, h, 0, 0)),
                  pl.BlockSpec((tq, dv), lambda b_, h, i: (b_ * nq + i, ZF_MO // dv + h))],
        out_specs=pl.BlockSpec((tq, dv), lambda b_, h, i: (b_ * nq + i, h)),
        out_shape=jax.ShapeDtypeStruct((b * s, nh * dv), F32),
        compiler_params=_params("parallel", "parallel", "arbitrary"),
        name="mlstm_attn",
    )(qk, qk, zb, col, row, zf)


def _rot_half_cols(w):
    half = w.shape[-1] // 2
    return jnp.concatenate([-w[..., half:], w[..., :half]], axis=-1)


def _split_w_in(w):
    a0 = 0
    b0 = MLA_COLS
    c0 = b0 + MLSTM_COLS
    d0 = c0 + DIFF_COLS
    qk_w = 2 * MLSTM_HEADS * MLSTM_QK
    v_w = MLSTM_HEADS * MLSTM_V
    kpe = w[:, a0 + MLA_Q_LORA + MLA_KV_LORA:a0 + MLA_COLS]
    gates = w[:, b0 + qk_w + 2 * v_w:c0]
    pad = jnp.zeros((w.shape[0], LANES - gates.shape[1]), w.dtype)
    wf = jnp.concatenate([
        w[:, a0:a0 + MLA_Q_LORA],
        w[:, b0:b0 + qk_w],
        w[:, b0 + qk_w + v_w:b0 + qk_w + 2 * v_w],
        w[:, a0 + MLA_Q_LORA:a0 + MLA_Q_LORA + MLA_KV_LORA],
        kpe, _rot_half_cols(kpe), gates, pad], axis=1)
    q_w = SWA_HEADS * SWA_HEAD_DIM
    kv_w = SWA_KV_HEADS * SWA_HEAD_DIM

    def dup(t):
        t = t.reshape(t.shape[0], SWA_KV_HEADS, 1, SWA_HEAD_DIM)
        return jnp.broadcast_to(t, (t.shape[0], SWA_KV_HEADS, 2, SWA_HEAD_DIM)).reshape(t.shape[0], -1)

    dq_w = DIFF_HEADS * 2 * DIFF_HEAD_DIM
    wb = jnp.concatenate([
        w[:, b0 + qk_w:b0 + qk_w + v_w],
        w[:, c0:c0 + dq_w] * DIFF_Q_SCALE,
        w[:, c0 + dq_w:d0],
        w[:, d0:d0 + q_w],
        dup(w[:, d0 + q_w:d0 + q_w + kv_w]),
        dup(w[:, d0 + q_w + kv_w:d0 + q_w + 2 * kv_w])], axis=1)
    assert wf.shape[1] == ZF_COLS and wb.shape[1] == ZB_COLS
    return wf.astype(BF16), wb.astype(BF16)


def _mla_weights(w_uq, w_ukv):
    wq = w_uq.reshape(MLA_Q_LORA, MLA_HEADS, MLA_NOPE + MLA_ROPE)
    pe = wq[..., MLA_NOPE:]
    wq = jnp.concatenate([wq[..., :MLA_NOPE], pe, _rot_half_cols(pe)], axis=-1)
    wkv = w_ukv.reshape(MLA_KV_LORA, MLA_HEADS, MLA_NOPE + MLA_V)
    return wq.transpose(1, 0, 2).astype(BF16), wkv.transpose(1, 0, 2).astype(BF16)


def _rope_tables(s):
    half = MLA_ROPE // 2
    inv = ROPE_THETA ** (-jnp.arange(half, dtype=F32) / half)
    ang = jnp.arange(s).astype(F32)[:, None] * inv[None, :]
    cos, sin = jnp.cos(ang), jnp.sin(ang)
    return jnp.concatenate([cos, cos], axis=-1), jnp.concatenate([sin, sin], axis=-1)


def mixers(zf, zb, l, b, s, cos, sin, bias_tbl, mla_q_norm, mla_kv_norm, mla_w_uq, mla_w_ukv, mlstm_conv_w,
           mlstm_conv_b, mlstm_gate_b, diff_lambda, diff_subln, swa_sink):
    wq, wkv = _mla_weights(mla_w_uq, mla_w_ukv)
    q, k, v = mla_proj(zf, mla_q_norm, mla_kv_norm, wq, wkv, cos, sin, b, s)
    y_a = mla_attn(q, k, v)
    qk = mlstm_conv(zf, mlstm_conv_w, mlstm_conv_b, b, s)
    col, row = mlstm_gates(zf, mlstm_gate_b, b, s)
    y_b = mlstm_attn(qk, zb, zf, col, row, b, s)
    lam_init = 0.8 - 0.6 * math.exp(-0.3 * l)
    y_c = diff_attn(zb, diff_lambda, diff_subln, lam_init, bias_tbl, b, s)
    y_d = swa_attn(zb, swa_sink, b, s)
    return y_a, y_b, y_c, y_d


def kernel(x, norm_gains, w_in, mla_q_norm, mla_kv_norm, mla_w_uq, mla_w_ukv, mlstm_conv_w,
           mlstm_conv_b, mlstm_gate_b, diff_lambda, diff_subln, swa_sink, group_norm, w_out,
           ffn1_w_gu, ffn1_w_down, ffn2_w_gu, ffn2_w_down):
    b, s, d = x.shape
    depth = w_in.shape[0]
    cos, sin = _rope_tables(s)
    bias_tbl = _alibi_table(DIFF_HEADS, s, ATTN_ROWS)
    x = x.reshape(b * s, d)
    xn = prenorm(x, norm_gains[0, 0])
    for l in range(depth):
        g = norm_gains[l]
        h = swiglu_up(xn, ffn1_w_gu, l)
        x, xn = resid_update(h, ffn1_w_down[l].astype(BF16), x, g[1], g[2], 0.5)
        wf, wb = _split_w_in(w_in[l])
        zf = matmul(xn, wf, F32, 1024, 768, "proj_f32")
        zb = matmul(xn, wb, BF16, 1024, 512, "proj_bf16")
        ys = mixers(zf, zb, l, b, s, cos, sin, bias_tbl, mla_q_norm[l], mla_kv_norm[l], mla_w_uq[l], mla_w_ukv[l],
                    mlstm_conv_w[l], mlstm_conv_b[l], mlstm_gate_b[l], diff_lambda[l], diff_subln[l],
                    swa_sink[l])
        yn = group_rmsnorm(*ys, group_norm[l])
        x, xn = resid_update(yn, w_out[l].astype(BF16), x, g[3], g[4], 1.0)
        h = swiglu_up(xn, ffn2_w_gu, l)
        g_next = norm_gains[l + 1, 0] if l + 1 < depth else g[5]
        x, xn = resid_update(h, ffn2_w_down[l].astype(BF16), x, g[5], g_next, 0.5)
    return x.reshape(b, s, d)
```

```python
import functools
import math

import jax
import jax.numpy as jnp
from jax import lax
from jax.experimental import pallas as pl
from jax.experimental.pallas import tpu as pltpu

F32 = jnp.float32
BF16 = jnp.bfloat16
NORM_EPS = 1e-6
ROPE_THETA = 10000.0
LOG2E = math.log2(math.e)
LANES = 128
ATTN_ROWS = 128
VMEM_LIMIT_BYTES = 60 * 1024 * 1024

D_MODEL = 4096
GROUP_WIDTH = D_MODEL // 4
D_FF = (3 * D_MODEL) // 2
MLA_HEADS = GROUP_WIDTH // 128
MLA_Q_LORA = D_MODEL // 4
MLA_KV_LORA = D_MODEL // 8
MLA_NOPE = 128
MLA_ROPE = 64
MLA_V = GROUP_WIDTH // MLA_HEADS
MLSTM_HEADS = 4
MLSTM_V = GROUP_WIDTH // MLSTM_HEADS
MLSTM_QK = MLSTM_V // 2
MLSTM_CONV = 5
DIFF_HEADS = 8
DIFF_HEAD_DIM = GROUP_WIDTH // (2 * DIFF_HEADS)
SWA_HEADS = 16
SWA_KV_HEADS = 2
SWA_HEAD_DIM = GROUP_WIDTH // SWA_HEADS
SWA_WINDOW = 128
SWA_REP = SWA_HEADS // SWA_KV_HEADS

MLA_Q_SCALE = (MLA_NOPE + MLA_ROPE) ** -0.5 * LOG2E
DIFF_Q_SCALE = DIFF_HEAD_DIM ** -0.5 * LOG2E

MLA_COLS = MLA_Q_LORA + MLA_KV_LORA + MLA_ROPE
MLSTM_COLS = 2 * MLSTM_HEADS * MLSTM_QK + 2 * MLSTM_HEADS * MLSTM_V + 4 * MLSTM_HEADS
DIFF_COLS = 3 * DIFF_HEADS * 2 * DIFF_HEAD_DIM
SWA_COLS = (SWA_HEADS + 2 * SWA_KV_HEADS) * SWA_HEAD_DIM

ZF_CQ = 0
ZF_MQK = 1024
ZF_MO = 2048
ZF_CKV = 3072
ZF_KPE = 3584
ZF_GATES = 3712
ZF_COLS = 3840
ZB_MV = 0
ZB_DQ = 1024
ZB_DK = 2048
ZB_DV = 3072
ZB_SQ = 4096
ZB_SK = 5120
ZB_SV = 5376
ZB_COLS = 5632


def _params(*sem):
    return pltpu.CompilerParams(dimension_semantics=sem, vmem_limit_bytes=VMEM_LIMIT_BYTES)


def _rms(x, g):
    return x * lax.rsqrt(jnp.mean(x * x, axis=-1, keepdims=True) + NORM_EPS) * g


def _sigmoid(x):
    return 1.0 / (1.0 + jnp.exp(-x))


_NT = (((1,), (1,)), ((), ()))


def _prenorm_kernel(x_ref, g_ref, o_ref):
    o_ref[...] = _rms(x_ref[...], g_ref[...]).astype(o_ref.dtype)


def prenorm(x, g, tm=512):
    t, d = x.shape
    return pl.pallas_call(
        _prenorm_kernel,
        grid=(t // tm,),
        in_specs=[pl.BlockSpec((tm, d), lambda i: (i, 0)), pl.BlockSpec((1, d), lambda i: (0, 0))],
        out_specs=pl.BlockSpec((tm, d), lambda i: (i, 0)),
        out_shape=jax.ShapeDtypeStruct((t, d), BF16),
        compiler_params=_params("parallel"),
        name="prenorm",
    )(x, g.reshape(1, d))


def _matmul_kernel(a_ref, w_ref, o_ref):
    o_ref[...] = jnp.dot(a_ref[...], w_ref[...], preferred_element_type=F32).astype(o_ref.dtype)


def matmul(a, w, out_dtype, tm, tn, name):
    m, k = a.shape
    n = w.shape[1]
    return pl.pallas_call(
        _matmul_kernel,
        grid=(m // tm, n // tn),
        in_specs=[pl.BlockSpec((tm, k), lambda i, j: (i, 0)), pl.BlockSpec((k, tn), lambda i, j: (0, j))],
        out_specs=pl.BlockSpec((tm, tn), lambda i, j: (i, j)),
        out_shape=jax.ShapeDtypeStruct((m, n), out_dtype),
        compiler_params=_params("parallel", "arbitrary"),
        name=name,
    )(a, w)


def _swiglu_kernel(a_ref, wg_ref, wu_ref, o_ref, wgb_ref, wub_ref):
    @pl.when(pl.program_id(1) == 0)
    def _():
        wgb_ref[...] = wg_ref[...].astype(BF16)
        wub_ref[...] = wu_ref[...].astype(BF16)

    a = a_ref[...]
    g = jnp.dot(a, wgb_ref[...], preferred_element_type=F32)
    u = jnp.dot(a, wub_ref[...], preferred_element_type=F32)
    o_ref[...] = (g * _sigmoid(g) * u).astype(o_ref.dtype)


def swiglu_up(a, w_gu, l, tm=2048, tn=256):
    m, k = a.shape
    f = w_gu.shape[2] // 2
    nj = f // tn
    return pl.pallas_call(
        _swiglu_kernel,
        grid=(nj, m // tm),
        in_specs=[pl.BlockSpec((tm, k), lambda j, i: (i, 0)),
                  pl.BlockSpec((None, k, tn), lambda j, i: (l, 0, j)),
                  pl.BlockSpec((None, k, tn), lambda j, i: (l, 0, j + nj))],
        out_specs=pl.BlockSpec((tm, tn), lambda j, i: (i, j)),
        out_shape=jax.ShapeDtypeStruct((m, f), BF16),
        scratch_shapes=[pltpu.VMEM((k, tn), BF16), pltpu.VMEM((k, tn), BF16)],
        compiler_params=_params("parallel", "arbitrary"),
        name="swiglu_up",
    )(a, w_gu, w_gu)


def _cast_kernel(x_ref, o_ref):
    o_ref[...] = x_ref[...].astype(o_ref.dtype)


def cast_bf16(w, tr=512):
    nl, r, c = w.shape
    spec = pl.BlockSpec((None, tr, c), lambda l, i: (l, i, 0))
    return pl.pallas_call(
        _cast_kernel,
        grid=(nl, r // tr),
        in_specs=[spec],
        out_specs=spec,
        out_shape=jax.ShapeDtypeStruct(w.shape, BF16),
        compiler_params=_params("parallel", "parallel"),
        name="cast_bf16",
    )(w)


RESID_STEPS = 8


def _resid_kernel(a_ref, w_ref, x_ref, gp_ref, gn_ref, xo_ref, xn_ref, acc0_ref, acc1_ref, *, coef, n_tiles):
    i = pl.program_id(0)
    k = pl.program_id(1)
    rs = x_ref.shape[0]

    def epilogue(prev_ref):
        rows = pl.ds(pl.multiple_of(k * rs, rs), rs)
        xnew = x_ref[...] + coef * _rms(prev_ref[rows, :], gp_ref[...])
        xo_ref[...] = xnew
        xn_ref[...] = _rms(xnew, gn_ref[...]).astype(xn_ref.dtype)
        prev_ref[rows, :] = jnp.zeros((rs, prev_ref.shape[1]), F32)

    @pl.when((i == 0) & (k == 0))
    def _():
        acc0_ref[...] = jnp.zeros_like(acc0_ref)
        acc1_ref[...] = jnp.zeros_like(acc1_ref)

    for parity, (cur_ref, prev_ref) in enumerate(((acc0_ref, acc1_ref), (acc1_ref, acc0_ref))):
        @pl.when((i % 2 == parity) & (i < n_tiles))
        def _(cur_ref=cur_ref, prev_ref=prev_ref):
            epilogue(prev_ref)
            cur_ref[...] += jnp.dot(a_ref[...], w_ref[...], preferred_element_type=F32)

        @pl.when((i % 2 == parity) & (i == n_tiles))
        def _(prev_ref=prev_ref):
            epilogue(prev_ref)


def resid_update(a, w, l, x, g_post, g_next, coef, tm=512):
    m, kk = a.shape
    d = w.shape[2]
    n_tiles = m // tm
    tk = kk // RESID_STEPS
    rs = tm // RESID_STEPS
    row_map = lambda i, k: (jnp.where(i == 0, 0, (i - 1) * RESID_STEPS + k), 0)
    return pl.pallas_call(
        functools.partial(_resid_kernel, coef=coef, n_tiles=n_tiles),
        grid=(n_tiles + 1, RESID_STEPS),
        in_specs=[pl.BlockSpec((tm, tk), lambda i, k: (jnp.minimum(i, n_tiles - 1), k)),
                  pl.BlockSpec((None, tk, d), lambda i, k: (l, k, 0)),
                  pl.BlockSpec((rs, d), row_map),
                  pl.BlockSpec((1, d), lambda i, k: (0, 0)),
                  pl.BlockSpec((1, d), lambda i, k: (0, 0))],
        out_specs=[pl.BlockSpec((rs, d), row_map), pl.BlockSpec((rs, d), row_map)],
        out_shape=[jax.ShapeDtypeStruct((m, d), F32), jax.ShapeDtypeStruct((m, d), BF16)],
        scratch_shapes=[pltpu.VMEM((tm, d), F32), pltpu.VMEM((tm, d), F32)],
        compiler_params=_params("arbitrary", "arbitrary"),
        name="resid_update",
    )(a, w, x, g_post.reshape(1, d), g_next.reshape(1, d))


def _group_norm_kernel(a_ref, b_ref, c_ref, d_ref, g_ref, o_ref):
    w = a_ref.shape[1]
    for n, r in enumerate((a_ref, b_ref, c_ref, d_ref)):
        o_ref[:, n * w:(n + 1) * w] = _rms(r[...], g_ref[:, n * w:(n + 1) * w]).astype(o_ref.dtype)


def group_rmsnorm(ya, yb, yc, yd, g, tm=512):
    t, w = ya.shape
    spec = pl.BlockSpec((tm, w), lambda i: (i, 0))
    return pl.pallas_call(
        _group_norm_kernel,
        grid=(t // tm,),
        in_specs=[spec, spec, spec, spec, pl.BlockSpec((1, 4 * w), lambda i: (0, 0))],
        out_specs=pl.BlockSpec((tm, 4 * w), lambda i: (i, 0)),
        out_shape=jax.ShapeDtypeStruct((t, 4 * w), BF16),
        compiler_params=_params("parallel"),
        name="group_norm",
    )(ya, yb, yc, yd, g.reshape(1, 4 * w))


def _mla_proj_kernel(cq_ref, ckv_ref, kpe_ref, qg_ref, kvg_ref, wq_ref, wkv_ref, cos_ref, sin_ref,
                     q_ref, k_ref, v_ref):
    cos = cos_ref[...]
    sin = sin_ref[...]
    kp = kpe_ref[...]
    k_rope = (kp[:, :MLA_ROPE] * cos + kp[:, MLA_ROPE:] * sin).astype(k_ref.dtype)
    qq = jnp.dot(_rms(cq_ref[...], qg_ref[...]).astype(BF16), wq_ref[...],
                 preferred_element_type=F32) * MLA_Q_SCALE
    kv = jnp.dot(_rms(ckv_ref[...], kvg_ref[...]).astype(BF16), wkv_ref[...], preferred_element_type=F32)
    hw = 2 * LANES
    for h in range(MLA_HEADS):
        qh = qq[:, h * hw:(h + 1) * hw]
        q_ref[h, :, :MLA_NOPE] = qh[:, :MLA_NOPE].astype(q_ref.dtype)
        q_ref[h, :, MLA_NOPE:] = (qh[:, MLA_NOPE:MLA_NOPE + MLA_ROPE] * cos
                                  + qh[:, MLA_NOPE + MLA_ROPE:] * sin).astype(q_ref.dtype)
        k_ref[h, :, :MLA_NOPE] = kv[:, h * hw:h * hw + MLA_NOPE].astype(k_ref.dtype)
        k_ref[h, :, MLA_NOPE:] = k_rope
        v_ref[h] = kv[:, h * hw + MLA_NOPE:(h + 1) * hw].astype(v_ref.dtype)


def mla_proj(zf, q_gain, kv_gain, wq, wkv, cos, sin, b, s, tm=512):
    ns = s // tm
    nh = MLA_HEADS
    dqk = MLA_NOPE + MLA_ROPE
    out_map = lambda i: (i // ns, 0, i % ns, 0)
    return pl.pallas_call(
        _mla_proj_kernel,
        grid=(b * ns,),
        in_specs=[pl.BlockSpec((tm, MLA_Q_LORA), lambda i: (i, ZF_CQ // MLA_Q_LORA)),
                  pl.BlockSpec((tm, MLA_KV_LORA), lambda i: (i, ZF_CKV // MLA_KV_LORA)),
                  pl.BlockSpec((tm, LANES), lambda i: (i, ZF_KPE // LANES)),
                  pl.BlockSpec((1, MLA_Q_LORA), lambda i: (0, 0)),
                  pl.BlockSpec((1, MLA_KV_LORA), lambda i: (0, 0)),
                  pl.BlockSpec(wq.shape, lambda i: (0, 0)),
                  pl.BlockSpec(wkv.shape, lambda i: (0, 0)),
                  pl.BlockSpec((tm, MLA_ROPE), lambda i: (i % ns, 0)),
                  pl.BlockSpec((tm, MLA_ROPE), lambda i: (i % ns, 0))],
        out_specs=[pl.BlockSpec((None, nh, tm, dqk), out_map),
                   pl.BlockSpec((None, nh, tm, dqk), out_map),
                   pl.BlockSpec((None, nh, tm, MLA_V), out_map)],
        out_shape=[jax.ShapeDtypeStruct((b, nh, s, dqk), BF16),
                   jax.ShapeDtypeStruct((b, nh, s, dqk), BF16),
                   jax.ShapeDtypeStruct((b, nh, s, MLA_V), BF16)],
        compiler_params=_params("parallel"),
        name="mla_proj",
    )(zf, zf, zf, q_gain.reshape(1, -1), kv_gain.reshape(1, -1), wq, wkv, cos, sin)


def _key_chunk(s):
    return min(1024, s)


def _fill_v_ones(vaug_ref, v_ref):
    dv = v_ref.shape[1]
    vaug_ref[:, :dv] = v_ref[...]
    vaug_ref[:, dv:] = jnp.ones((v_ref.shape[0], vaug_ref.shape[1] - dv), vaug_ref.dtype)


def _softmax_pv(lhs, k_ref, vaug_ref, dv, bias_fn=None):
    chunk = _key_chunk(k_ref.shape[0])
    m = acc = None
    for c in range(k_ref.shape[0] // chunk):
        ks = slice(c * chunk, (c + 1) * chunk)
        sc = lax.dot_general(lhs, k_ref[ks, :], _NT, preferred_element_type=F32)
        if bias_fn is not None:
            sc = sc + bias_fn(c)
        top = jnp.max(sc, axis=-1, keepdims=True)
        m_new = top if m is None else jnp.maximum(m, top)
        pv = jnp.dot(jnp.exp2(sc - m_new).astype(BF16), vaug_ref[ks, :], preferred_element_type=F32)
        acc = pv if acc is None else acc * jnp.exp2(m - m_new) + pv
        m = m_new
    return acc[:, :dv] / acc[:, dv:]


def _mla_attn_kernel(q_ref, k_ref, v_ref, o_ref, vaug_ref):
    @pl.when(pl.program_id(2) == 0)
    def _():
        _fill_v_ones(vaug_ref, v_ref)

    o_ref[...] = _softmax_pv(q_ref[...], k_ref, vaug_ref, MLA_V).astype(o_ref.dtype)


def mla_attn(q, k, v, tq=1024):
    b, h, s, dqk = q.shape
    tq = min(tq, s)
    nq = s // tq
    return pl.pallas_call(
        _mla_attn_kernel,
        grid=(b, h, nq),
        in_specs=[pl.BlockSpec((None, None, tq, dqk), lambda b_, h_, i: (b_, h_, i, 0)),
                  pl.BlockSpec((None, None, s, dqk), lambda b_, h_, i: (b_, h_, 0, 0)),
                  pl.BlockSpec((None, None, s, MLA_V), lambda b_, h_, i: (b_, h_, 0, 0))],
        out_specs=pl.BlockSpec((tq, MLA_V), lambda b_, h_, i: (b_ * nq + i, h_)),
        out_shape=jax.ShapeDtypeStruct((b * s, h * MLA_V), F32),
        scratch_shapes=[pltpu.VMEM((s, 2 * MLA_V), BF16)],
        compiler_params=_params("parallel", "parallel", "arbitrary"),
        name="mla_attn",
    )(q, k, v)


def _diff_attn_kernel(q_ref, k_ref, v_ref, bias_ref, lv_ref, sub_ref, o_ref, vaug_ref, *, lam_init, rows):
    i = pl.program_id(2)
    tq = q_ref.shape[0]
    s = k_ref.shape[0]
    d = DIFF_HEAD_DIM

    @pl.when(i == 0)
    def _():
        _fill_v_ones(vaug_ref, v_ref)

    lv = lv_ref[...]
    lam = (jnp.exp(jnp.sum(lv[0:1] * lv[1:2], axis=-1, keepdims=True))
           - jnp.exp(jnp.sum(lv[2:3] * lv[3:4], axis=-1, keepdims=True)) + lam_init)
    q = q_ref[...]
    lane = lax.broadcasted_iota(jnp.int32, q.shape, 1)
    zero = jnp.zeros_like(q)
    lhs = jnp.concatenate([jnp.where(lane < d, q, zero), jnp.where(lane >= d, q, zero)], axis=0)

    def bias(c):
        chunk = _key_chunk(s)
        groups = [bias_ref[:, pl.ds(pl.multiple_of(s - rows - (i * tq + g * rows), rows) + c * chunk, chunk)]
                  for g in range(tq // rows)]
        return jnp.concatenate(groups + groups, axis=0)

    o = _softmax_pv(lhs, k_ref, vaug_ref, 2 * d, bias)
    o = o[:tq] - lam * o[tq:]
    o_ref[...] = (_rms(o, sub_ref[...]) * (1.0 - lam_init)).astype(o_ref.dtype)


def _alibi_table(n_heads, s, rows):
    slopes = 2.0 ** (-8.0 * jnp.arange(1, n_heads + 1, dtype=F32) / n_heads)
    r = jnp.arange(rows, dtype=jnp.int32)[:, None]
    c = jnp.arange(2 * s - rows, dtype=jnp.int32)[None, :]
    dist = jnp.abs(r - c + (s - rows)).astype(F32)
    return (-LOG2E * slopes)[:, None, None] * dist[None]


def diff_attn(zb, lam_vecs, subln, lam_init, bias_tbl, b, s, tq=512):
    tq = min(tq, s)
    nq = s // tq
    hd = 2 * DIFF_HEAD_DIM
    rows, width = bias_tbl.shape[1:]
    return pl.pallas_call(
        functools.partial(_diff_attn_kernel, lam_init=lam_init, rows=rows),
        grid=(b, DIFF_HEADS, nq),
        in_specs=[pl.BlockSpec((tq, hd), lambda b_, h, i: (b_ * nq + i, ZB_DQ // hd + h)),
                  pl.BlockSpec((s, hd), lambda b_, h, i: (b_, ZB_DK // hd + h)),
                  pl.BlockSpec((s, hd), lambda b_, h, i: (b_, ZB_DV // hd + h)),
                  pl.BlockSpec((None, rows, width), lambda b_, h, i: (h, 0, 0)),
                  pl.BlockSpec((4, DIFF_HEAD_DIM), lambda b_, h, i: (0, 0)),
                  pl.BlockSpec((1, hd), lambda b_, h, i: (0, 0))],
        out_specs=pl.BlockSpec((tq, hd), lambda b_, h, i: (b_ * nq + i, h)),
        out_shape=jax.ShapeDtypeStruct((b * s, DIFF_HEADS * hd), F32),
        scratch_shapes=[pltpu.VMEM((s, 2 * hd), BF16)],
        compiler_params=_params("parallel", "parallel", "arbitrary"),
        name="diff_attn",
    )(zb, zb, zb, bias_tbl, lam_vecs, subln.reshape(1, hd))


def _swa_kernel(sink_ref, q_ref, k_ref, v_ref, o_ref, *, seq):
    w = SWA_WINDOW
    d = SWA_HEAD_DIM
    n = pl.program_id(1)
    start = pl.multiple_of(jnp.clip((n - 1) * w, 0, seq - 3 * w), w)
    kband = k_ref[pl.ds(start, 3 * w), :]
    vband = v_ref[pl.ds(start, 3 * w), :]
    qpos = n * w + lax.broadcasted_iota(jnp.int32, (2 * w, 3 * w), 0) % w
    kpos = start + lax.broadcasted_iota(jnp.int32, (2 * w, 3 * w), 1)
    rel = jnp.abs(qpos - kpos)
    valid = rel <= w
    relf = rel.astype(F32)
    lane = lax.broadcasted_iota(jnp.int32, (w, 2 * d), 1)
    scale = d ** -0.5
    for pair in range(SWA_HEADS // 2):
        g = (2 * pair) // SWA_REP
        qp = q_ref[:, pair * 2 * d:(pair + 1) * 2 * d]
        zero = jnp.zeros_like(qp)
        qs = jnp.concatenate([jnp.where(lane < d, qp, zero), jnp.where(lane >= d, qp, zero)], axis=0)
        sc = lax.dot_general(qs, kband[:, g * 2 * d:(g + 1) * 2 * d], _NT,
                             preferred_element_type=F32) * scale
        row = lax.broadcasted_iota(jnp.int32, (2 * w, 1), 0)
        h0 = 2 * pair
        slope = jnp.where(row < w, 2.0 ** (-8.0 * (h0 + 1) / SWA_HEADS), 2.0 ** (-8.0 * (h0 + 2) / SWA_HEADS))
        sink = jnp.where(row < w, sink_ref[h0], sink_ref[h0 + 1])
        sc = jnp.where(valid, sc - slope * relf, -jnp.inf)
        m = jnp.maximum(jnp.max(sc, axis=-1, keepdims=True), sink)
        e = jnp.exp(sc - m)
        l = jnp.sum(e, axis=-1, keepdims=True) + jnp.exp(sink - m)
        p = (e * (1.0 / l)).astype(BF16)
        o = jnp.dot(p, vband[:, g * 2 * d:(g + 1) * 2 * d], preferred_element_type=F32)
        o_ref[:, pair * 2 * d:(pair + 1) * 2 * d] = jnp.where(lane < d, o[:w], o[w:]).astype(o_ref.dtype)


def swa_attn(zb, sink, b, s):
    w = SWA_WINDOW
    nb = s // w
    kvw = SWA_KV_HEADS * 2 * SWA_HEAD_DIM
    qw = SWA_HEADS * SWA_HEAD_DIM
    return pl.pallas_call(
        functools.partial(_swa_kernel, seq=s),
        grid=(b, nb),
        in_specs=[pl.BlockSpec(memory_space=pltpu.SMEM),
                  pl.BlockSpec((w, qw), lambda b_, n: (b_ * nb + n, ZB_SQ // qw)),
                  pl.BlockSpec((s, kvw), lambda b_, n: (b_, ZB_SK // kvw)),
                  pl.BlockSpec((s, kvw), lambda b_, n: (b_, ZB_SV // kvw))],
        out_specs=pl.BlockSpec((w, qw), lambda b_, n: (b_ * nb + n, 0)),
        out_shape=jax.ShapeDtypeStruct((b * s, qw), F32),
        compiler_params=_params("parallel", "arbitrary"),
        name="swa_attn",
    )(sink, zb, zb, zb)


def _conv_kernel(x_ref, w_ref, b_ref, o_ref, *, kscale):
    x = x_ref[...]
    s = x.shape[0]
    row = lax.broadcasted_iota(jnp.int32, x.shape, 0)
    pad = MLSTM_CONV // 2
    y = x * w_ref[pad:pad + 1, :] + b_ref[...]
    for j in range(MLSTM_CONV):
        off = j - pad
        if off == 0:
            continue
        shifted = pltpu.roll(x, (-off) % s, 0)
        ok = (row >= -off) if off < 0 else (row < s - off)
        y = y + jnp.where(ok, shifted, 0.0) * w_ref[j:j + 1, :]
    y = y * _sigmoid(y)
    y = y * jnp.where(pl.program_id(1) >= pl.num_programs(1) // 2, kscale, 1.0)
    o_ref[...] = y.astype(o_ref.dtype)


def mlstm_conv(zf, conv_w, conv_b, b, s, tc=256):
    c = conv_w.shape[1]
    nc = c // tc
    return pl.pallas_call(
        functools.partial(_conv_kernel, kscale=MLSTM_QK ** -0.5),
        grid=(b, nc),
        in_specs=[pl.BlockSpec((s, tc), lambda b_, j: (b_, ZF_MQK // tc + j)),
                  pl.BlockSpec((MLSTM_CONV, tc), lambda b_, j: (0, j)),
                  pl.BlockSpec((1, tc), lambda b_, j: (0, j))],
        out_specs=pl.BlockSpec((s, tc), lambda b_, j: (b_, j)),
        out_shape=jax.ShapeDtypeStruct((b * s, c), BF16),
        compiler_params=_params("parallel", "arbitrary"),
        name="mlstm_conv",
    )(zf, conv_w, conv_b.reshape(1, c))


def _scan(x, row, op, fill, reverse):
    s = x.shape[0]
    d = 1
    while d < s:
        if reverse:
            shifted = jnp.where(row < s - d, pltpu.roll(x, s - d, 0), fill)
        else:
            shifted = jnp.where(row >= d, pltpu.roll(x, d, 0), fill)
        x = op(x, shifted)
        d *= 2
    return x


def _gate_kernel(g_ref, gb_ref, col_ref, row_ref):
    nh = MLSTM_HEADS
    g = g_ref[...] + gb_ref[...]
    ls = jnp.minimum(g, 0.0) - jnp.log1p(jnp.exp(-jnp.abs(g)))
    row = lax.broadcasted_iota(jnp.int32, g.shape, 0)
    lane = lax.broadcasted_iota(jnp.int32, g.shape, 1)
    cum_f = pltpu.roll(_scan(ls, row, jnp.add, 0.0, False), LANES - nh, 1)
    cum_b = pltpu.roll(_scan(ls, row, jnp.add, 0.0, True), LANES - nh, 1)
    a_f = g - cum_f
    a_b = g - cum_b
    top_f = _scan(a_f, row, jnp.maximum, -jnp.inf, False)
    top_b = _scan(a_b, row, jnp.maximum, -jnp.inf, True)
    m_f = cum_f + top_f
    m_b = cum_b + top_b
    src_t = (jnp.where(lane < 2 * nh, a_f, a_b) * LOG2E).T
    sub = lax.broadcasted_iota(jnp.int32, row_ref.shape[1:], 0)
    for h in range(nh):
        col_ref[h] = jnp.where(lane == 0, top_f[:, h:h + 1] * LOG2E,
                     jnp.where(lane == 1, m_f[:, h:h + 1],
                     jnp.where(lane == 2, top_b[:, 2 * nh + h:2 * nh + h + 1] * LOG2E,
                               m_b[:, 2 * nh + h:2 * nh + h + 1])))
        row_ref[h] = jnp.where(sub == 0, src_t[h:h + 1, :],
                     jnp.where(sub == 1, src_t[2 * nh + h:2 * nh + h + 1, :], 0.0))


def mlstm_gates(zf, gate_b, b, s):
    gb = jnp.zeros((1, LANES), F32).at[0, :4 * MLSTM_HEADS].set(gate_b.reshape(-1))
    nh = MLSTM_HEADS
    return pl.pallas_call(
        _gate_kernel,
        grid=(b,),
        in_specs=[pl.BlockSpec((s, LANES), lambda b_: (b_, ZF_GATES // LANES)),
                  pl.BlockSpec((1, LANES), lambda b_: (0, 0))],
        out_specs=[pl.BlockSpec((None, nh, s, LANES), lambda b_: (b_, 0, 0, 0)),
                   pl.BlockSpec((None, nh, 8, s), lambda b_: (b_, 0, 0, 0))],
        out_shape=[jax.ShapeDtypeStruct((b, nh, s, LANES), F32),
                   jax.ShapeDtypeStruct((b, nh, 8, s), F32)],
        compiler_params=_params("parallel"),
        name="mlstm_gates",
    )(zf, gb)


def _mlstm_attn_kernel(q_ref, k_ref, v_ref, col_ref, row_ref, og_ref, o_ref):
    tq = q_ref.shape[0]
    s = k_ref.shape[0]
    chunk = _key_chunk(s)
    q = q_ref[...]
    col = col_ref[...]
    rel0 = (lax.broadcasted_iota(jnp.int32, (tq, chunk), 1)
            - lax.broadcasted_iota(jnp.int32, (tq, chunk), 0) - pl.program_id(2) * tq)
    num = den = None
    for c in range(s // chunk):
        ks = slice(c * chunk, (c + 1) * chunk)
        sc = lax.dot_general(q, k_ref[ks, :], _NT, preferred_element_type=F32)
        rel = rel0 + c * chunk
        wgt = jnp.concatenate(
            [jnp.where(rel <= 0, jnp.exp2(row_ref[0:1, ks] - col[:, 0:1]), 0.0) * sc,
             jnp.where(rel >= 0, jnp.exp2(row_ref[1:2, ks] - col[:, 2:3]), 0.0) * sc], axis=0)
        pv = jnp.dot(wgt.astype(BF16), v_ref[ks, :], preferred_element_type=F32)
        ws = jnp.sum(wgt, axis=-1, keepdims=True)
        num = pv if num is None else num + pv
        den = ws if den is None else den + ws
    h_f = num[:tq] / jnp.maximum(jnp.abs(den[:tq]), jnp.exp(-col[:, 1:2]))
    h_b = num[tq:] / jnp.maximum(jnp.abs(den[tq:]), jnp.exp(-col[:, 3:4]))
    o_ref[...] = (_sigmoid(og_ref[...]) * (h_f + h_b)).astype(o_ref.dtype)


def mlstm_attn(qk, zb, zf, col, row, b, s, tq=512):
    tq = min(tq, s)
    nq = s // tq
    nh, dk, dv = MLSTM_HEADS, MLSTM_QK, MLSTM_V
    return pl.pallas_call(
        _mlstm_attn_kernel,
        grid=(b, nh, nq),
        in_specs=[pl.BlockSpec((tq, dk), lambda b_, h, i: (b_ * nq + i, h)),
                  pl.BlockSpec((s, dk), lambda b_, h, i: (b_, nh + h)),
                  pl.BlockSpec((s, dv), lambda b_, h, i: (b_, ZB_MV // dv + h)),
                  pl.BlockSpec((None, None, tq, LANES), lambda b_, h, i: (b_, h, i, 0)),
                  pl.BlockSpec((None, None, 8, s), lambda b_, h, i: (b_, h, 0, 0)),
                  pl.BlockSpec((tq, dv), lambda b_, h, i: (b_ * nq + i, ZF_MO // dv + h))],
        out_specs=pl.BlockSpec((tq, dv), lambda b_, h, i: (b_ * nq + i, h)),
        out_shape=jax.ShapeDtypeStruct((b * s, nh * dv), F32),
        compiler_params=_params("parallel", "parallel", "arbitrary"),
        name="mlstm_attn",
    )(qk, qk, zb, col, row, zf)


def _rot_half_cols(w):
    half = w.shape[-1] // 2
    return jnp.concatenate([-w[..., half:], w[..., :half]], axis=-1)


def _split_w_in(w):
    a0 = 0
    b0 = MLA_COLS
    c0 = b0 + MLSTM_COLS
    d0 = c0 + DIFF_COLS
    qk_w = 2 * MLSTM_HEADS * MLSTM_QK
    v_w = MLSTM_HEADS * MLSTM_V
    kpe = w[:, a0 + MLA_Q_LORA + MLA_KV_LORA:a0 + MLA_COLS]
    gates = w[:, b0 + qk_w + 2 * v_w:c0]
    pad = jnp.zeros((w.shape[0], LANES - gates.shape[1]), w.dtype)
    wf = jnp.concatenate([
        w[:, a0:a0 + MLA_Q_LORA],
        w[:, b0:b0 + qk_w],
        w[:, b0 + qk_w + v_w:b0 + qk_w + 2 * v_w],
        w[:, a0 + MLA_Q_LORA:a0 + MLA_Q_LORA + MLA_KV_LORA],
        kpe, _rot_half_cols(kpe), gates, pad], axis=1)
    q_w = SWA_HEADS * SWA_HEAD_DIM
    kv_w = SWA_KV_HEADS * SWA_HEAD_DIM

    def dup(t):
        t = t.reshape(t.shape[0], SWA_KV_HEADS, 1, SWA_HEAD_DIM)
        return jnp.broadcast_to(t, (t.shape[0], SWA_KV_HEADS, 2, SWA_HEAD_DIM)).reshape(t.shape[0], -1)

    dq_w = DIFF_HEADS * 2 * DIFF_HEAD_DIM
    wb = jnp.concatenate([
        w[:, b0 + qk_w:b0 + qk_w + v_w],
        w[:, c0:c0 + dq_w] * DIFF_Q_SCALE,
        w[:, c0 + dq_w:d0],
        w[:, d0:d0 + q_w],
        dup(w[:, d0 + q_w:d0 + q_w + kv_w]),
        dup(w[:, d0 + q_w + kv_w:d0 + q_w + 2 * kv_w])], axis=1)
    assert wf.shape[1] == ZF_COLS and wb.shape[1] == ZB_COLS
    return wf.astype(BF16), wb.astype(BF16)


def _mla_weights(w_uq, w_ukv):
    wq = w_uq.reshape(MLA_Q_LORA, MLA_HEADS, MLA_NOPE + MLA_ROPE)
    pe = wq[..., MLA_NOPE:]
    wq = jnp.concatenate([wq[..., :MLA_NOPE], pe, _rot_half_cols(pe)], axis=-1)
    return wq.reshape(MLA_Q_LORA, -1).astype(BF16), w_ukv.astype(BF16)


def _rope_tables(s):
    half = MLA_ROPE // 2
    inv = ROPE_THETA ** (-jnp.arange(half, dtype=F32) / half)
    ang = jnp.arange(s).astype(F32)[:, None] * inv[None, :]
    cos, sin = jnp.cos(ang), jnp.sin(ang)
    return jnp.concatenate([cos, cos], axis=-1), jnp.concatenate([sin, sin], axis=-1)


def mixers(zf, zb, l, b, s, cos, sin, bias_tbl, mla_q_norm, mla_kv_norm, mla_w_uq, mla_w_ukv, mlstm_conv_w,
           mlstm_conv_b, mlstm_gate_b, diff_lambda, diff_subln, swa_sink):
    wq, wkv = _mla_weights(mla_w_uq, mla_w_ukv)
    q, k, v = mla_proj(zf, mla_q_norm, mla_kv_norm, wq, wkv, cos, sin, b, s)
    y_a = mla_attn(q, k, v)
    qk = mlstm_conv(zf, mlstm_conv_w, mlstm_conv_b, b, s)
    col, row = mlstm_gates(zf, mlstm_gate_b, b, s)
    y_b = mlstm_attn(qk, zb, zf, col, row, b, s)
    lam_init = 0.8 - 0.6 * math.exp(-0.3 * l)
    y_c = diff_attn(zb, diff_lambda, diff_subln, lam_init, bias_tbl, b, s)
    y_d = swa_attn(zb, swa_sink, b, s)
    return y_a, y_b, y_c, y_d


def kernel(x, norm_gains, w_in, mla_q_norm, mla_kv_norm, mla_w_uq, mla_w_ukv, mlstm_conv_w,
           mlstm_conv_b, mlstm_gate_b, diff_lambda, diff_subln, swa_sink, group_norm, w_out,
           ffn1_w_gu, ffn1_w_down, ffn2_w_gu, ffn2_w_down):
    b, s, d = x.shape
    depth = w_in.shape[0]
    cos, sin = _rope_tables(s)
    bias_tbl = _alibi_table(DIFF_HEADS, s, ATTN_ROWS)
    w_down1, w_down2, w_o = cast_bf16(ffn1_w_down), cast_bf16(ffn2_w_down), cast_bf16(w_out)
    x = x.reshape(b * s, d)
    xn = prenorm(x, norm_gains[0, 0])
    for l in range(depth):
        g = norm_gains[l]
        h = swiglu_up(xn, ffn1_w_gu, l)
        x, xn = resid_update(h, w_down1, l, x, g[1], g[2], 0.5)
        wf, wb = _split_w_in(w_in[l])
        zf = matmul(xn, wf, F32, 1024, 768, "proj_f32")
        zb = matmul(xn, wb, BF16, 1024, 512, "proj_bf16")
        ys = mixers(zf, zb, l, b, s, cos, sin, bias_tbl, mla_q_norm[l], mla_kv_norm[l], mla_w_uq[l], mla_w_ukv[l],
                    mlstm_conv_w[l], mlstm_conv_b[l], mlstm_gate_b[l], diff_lambda[l], diff_subln[l],
                    swa_sink[l])
        yn = group_rmsnorm(*ys, group_norm[l])
        x, xn = resid_update(yn, w_o, l, x, g[3], g[4], 1.0)
        h = swiglu_up(xn, ffn2_w_gu, l)
        g_next = norm_gains[l + 1, 0] if l + 1 < depth else g[5]
        x, xn = resid_update(h, w_down2, l, x, g[5], g_next, 0.5)
    return x.reshape(b, s, d)
```

```python
import functools
import math

import jax
import jax.numpy as jnp
from jax import lax
from jax.experimental import pallas as pl
from jax.experimental.pallas import tpu as pltpu

F32 = jnp.float32
BF16 = jnp.bfloat16
NORM_EPS = 1e-6
ROPE_THETA = 10000.0
LOG2E = math.log2(math.e)
LANES = 128
ATTN_ROWS = 128
VMEM_LIMIT_BYTES = 60 * 1024 * 1024

D_MODEL = 4096
GROUP_WIDTH = D_MODEL // 4
D_FF = (3 * D_MODEL) // 2
MLA_HEADS = GROUP_WIDTH // 128
MLA_Q_LORA = D_MODEL // 4
MLA_KV_LORA = D_MODEL // 8
MLA_NOPE = 128
MLA_ROPE = 64
MLA_V = GROUP_WIDTH // MLA_HEADS
MLSTM_HEADS = 4
MLSTM_V = GROUP_WIDTH // MLSTM_HEADS
MLSTM_QK = MLSTM_V // 2
MLSTM_CONV = 5
DIFF_HEADS = 8
DIFF_HEAD_DIM = GROUP_WIDTH // (2 * DIFF_HEADS)
SWA_HEADS = 16
SWA_KV_HEADS = 2
SWA_HEAD_DIM = GROUP_WIDTH // SWA_HEADS
SWA_WINDOW = 128
SWA_REP = SWA_HEADS // SWA_KV_HEADS

MLA_Q_SCALE = (MLA_NOPE + MLA_ROPE) ** -0.5 * LOG2E
DIFF_Q_SCALE = DIFF_HEAD_DIM ** -0.5 * LOG2E

MLA_COLS = MLA_Q_LORA + MLA_KV_LORA + MLA_ROPE
MLSTM_COLS = 2 * MLSTM_HEADS * MLSTM_QK + 2 * MLSTM_HEADS * MLSTM_V + 4 * MLSTM_HEADS
DIFF_COLS = 3 * DIFF_HEADS * 2 * DIFF_HEAD_DIM
SWA_COLS = (SWA_HEADS + 2 * SWA_KV_HEADS) * SWA_HEAD_DIM

ZF_CQ = 0
ZF_MQK = 1024
ZF_MO = 2048
ZF_CKV = 3072
ZF_KPE = 3584
ZF_GATES = 3712
ZF_COLS = 3840
ZB_MV = 0
ZB_DQ = 1024
ZB_DK = 2048
ZB_DV = 3072
ZB_SQ = 4096
ZB_SK = 5120
ZB_SV = 5376
ZB_COLS = 5632


def _params(*sem):
    return pltpu.CompilerParams(dimension_semantics=sem, vmem_limit_bytes=VMEM_LIMIT_BYTES)


def _rms(x, g):
    return x * lax.rsqrt(jnp.mean(x * x, axis=-1, keepdims=True) + NORM_EPS) * g


def _sigmoid(x):
    return 1.0 / (1.0 + jnp.exp(-x))


_NT = (((1,), (1,)), ((), ()))


def _prenorm_kernel(x_ref, g_ref, o_ref):
    o_ref[...] = _rms(x_ref[...], g_ref[...]).astype(o_ref.dtype)


def prenorm(x, g, tm=512):
    t, d = x.shape
    return pl.pallas_call(
        _prenorm_kernel,
        grid=(t // tm,),
        in_specs=[pl.BlockSpec((tm, d), lambda i: (i, 0)), pl.BlockSpec((1, d), lambda i: (0, 0))],
        out_specs=pl.BlockSpec((tm, d), lambda i: (i, 0)),
        out_shape=jax.ShapeDtypeStruct((t, d), BF16),
        compiler_params=_params("parallel"),
        name="prenorm",
    )(x, g.reshape(1, d))


def _matmul_kernel(a_ref, w_ref, o_ref):
    o_ref[...] = jnp.dot(a_ref[...], w_ref[...], preferred_element_type=F32).astype(o_ref.dtype)


def matmul(a, w, out_dtype, tm, tn, name):
    m, k = a.shape
    n = w.shape[1]
    return pl.pallas_call(
        _matmul_kernel,
        grid=(m // tm, n // tn),
        in_specs=[pl.BlockSpec((tm, k), lambda i, j: (i, 0)), pl.BlockSpec((k, tn), lambda i, j: (0, j))],
        out_specs=pl.BlockSpec((tm, tn), lambda i, j: (i, j)),
        out_shape=jax.ShapeDtypeStruct((m, n), out_dtype),
        compiler_params=_params("parallel", "arbitrary"),
        name=name,
    )(a, w)


def _swiglu_kernel(a_ref, wg_ref, wu_ref, o_ref, wgb_ref, wub_ref):
    @pl.when(pl.program_id(1) == 0)
    def _():
        wgb_ref[...] = wg_ref[...].astype(BF16)
        wub_ref[...] = wu_ref[...].astype(BF16)

    a = a_ref[...]
    g = jnp.dot(a, wgb_ref[...], preferred_element_type=F32)
    u = jnp.dot(a, wub_ref[...], preferred_element_type=F32)
    o_ref[...] = (g * _sigmoid(g) * u).astype(o_ref.dtype)


def swiglu_up(a, w_gu, l, tm=2048, tn=256):
    m, k = a.shape
    f = w_gu.shape[2] // 2
    nj = f // tn
    return pl.pallas_call(
        _swiglu_kernel,
        grid=(nj, m // tm),
        in_specs=[pl.BlockSpec((tm, k), lambda j, i: (i, 0)),
                  pl.BlockSpec((None, k, tn), lambda j, i: (l, 0, j)),
                  pl.BlockSpec((None, k, tn), lambda j, i: (l, 0, j + nj))],
        out_specs=pl.BlockSpec((tm, tn), lambda j, i: (i, j)),
        out_shape=jax.ShapeDtypeStruct((m, f), BF16),
        scratch_shapes=[pltpu.VMEM((k, tn), BF16), pltpu.VMEM((k, tn), BF16)],
        compiler_params=_params("parallel", "arbitrary"),
        name="swiglu_up",
    )(a, w_gu, w_gu)


def _cast_kernel(x_ref, o_ref):
    o_ref[...] = x_ref[...].astype(o_ref.dtype)


def cast_bf16(w, tr=512):
    nl, r, c = w.shape
    spec = pl.BlockSpec((None, tr, c), lambda l, i: (l, i, 0))
    return pl.pallas_call(
        _cast_kernel,
        grid=(nl, r // tr),
        in_specs=[spec],
        out_specs=spec,
        out_shape=jax.ShapeDtypeStruct(w.shape, BF16),
        compiler_params=_params("parallel", "parallel"),
        name="cast_bf16",
    )(w)


RESID_STEPS = 8


def _resid_kernel(a_ref, w_ref, x_ref, gp_ref, gn_ref, xo_ref, xn_ref, acc0_ref, acc1_ref, *, coef, n_tiles):
    i = pl.program_id(0)
    k = pl.program_id(1)
    rs = x_ref.shape[0]

    def epilogue(prev_ref):
        rows = pl.ds(pl.multiple_of(k * rs, rs), rs)
        xnew = x_ref[...] + coef * _rms(prev_ref[rows, :], gp_ref[...])
        xo_ref[...] = xnew
        xn_ref[...] = _rms(xnew, gn_ref[...]).astype(xn_ref.dtype)
        prev_ref[rows, :] = jnp.zeros((rs, prev_ref.shape[1]), F32)

    @pl.when((i == 0) & (k == 0))
    def _():
        acc0_ref[...] = jnp.zeros_like(acc0_ref)
        acc1_ref[...] = jnp.zeros_like(acc1_ref)

    for parity, (cur_ref, prev_ref) in enumerate(((acc0_ref, acc1_ref), (acc1_ref, acc0_ref))):
        @pl.when((i % 2 == parity) & (i < n_tiles))
        def _(cur_ref=cur_ref, prev_ref=prev_ref):
            epilogue(prev_ref)
            cur_ref[...] += jnp.dot(a_ref[...], w_ref[...], preferred_element_type=F32)

        @pl.when((i % 2 == parity) & (i == n_tiles))
        def _(prev_ref=prev_ref):
            epilogue(prev_ref)


def resid_update(a, w, l, x, g_post, g_next, coef, tm=1024):
    m, kk = a.shape
    d = w.shape[2]
    n_tiles = m // tm
    tk = kk // RESID_STEPS
    rs = tm // RESID_STEPS
    row_map = lambda i, k: (jnp.where(i == 0, 0, (i - 1) * RESID_STEPS + k), 0)
    return pl.pallas_call(
        functools.partial(_resid_kernel, coef=coef, n_tiles=n_tiles),
        grid=(n_tiles + 1, RESID_STEPS),
        in_specs=[pl.BlockSpec((tm, tk), lambda i, k: (jnp.minimum(i, n_tiles - 1), k)),
                  pl.BlockSpec((None, tk, d), lambda i, k: (l, k, 0)),
                  pl.BlockSpec((rs, d), row_map),
                  pl.BlockSpec((1, d), lambda i, k: (0, 0)),
                  pl.BlockSpec((1, d), lambda i, k: (0, 0))],
        out_specs=[pl.BlockSpec((rs, d), row_map), pl.BlockSpec((rs, d), row_map)],
        out_shape=[jax.ShapeDtypeStruct((m, d), F32), jax.ShapeDtypeStruct((m, d), BF16)],
        scratch_shapes=[pltpu.VMEM((tm, d), F32), pltpu.VMEM((tm, d), F32)],
        compiler_params=_params("arbitrary", "arbitrary"),
        name="resid_update",
    )(a, w, x, g_post.reshape(1, d), g_next.reshape(1, d))


def _group_norm_kernel(a_ref, b_ref, c_ref, d_ref, g_ref, o_ref):
    w = a_ref.shape[1]
    for n, r in enumerate((a_ref, b_ref, c_ref, d_ref)):
        o_ref[:, n * w:(n + 1) * w] = _rms(r[...], g_ref[:, n * w:(n + 1) * w]).astype(o_ref.dtype)


def group_rmsnorm(ya, yb, yc, yd, g, tm=512):
    t, w = ya.shape
    spec = pl.BlockSpec((tm, w), lambda i: (i, 0))
    return pl.pallas_call(
        _group_norm_kernel,
        grid=(t // tm,),
        in_specs=[spec, spec, spec, spec, pl.BlockSpec((1, 4 * w), lambda i: (0, 0))],
        out_specs=pl.BlockSpec((tm, 4 * w), lambda i: (i, 0)),
        out_shape=jax.ShapeDtypeStruct((t, 4 * w), BF16),
        compiler_params=_params("parallel"),
        name="group_norm",
    )(ya, yb, yc, yd, g.reshape(1, 4 * w))


def _mla_proj_kernel(cq_ref, ckv_ref, kpe_ref, qg_ref, kvg_ref, wq_ref, wkv_ref, cos_ref, sin_ref,
                     q_ref, k_ref, v_ref):
    cos = cos_ref[...]
    sin = sin_ref[...]
    kp = kpe_ref[...]
    k_rope = (kp[:, :MLA_ROPE] * cos + kp[:, MLA_ROPE:] * sin).astype(k_ref.dtype)
    qq = jnp.dot(_rms(cq_ref[...], qg_ref[...]).astype(BF16), wq_ref[...],
                 preferred_element_type=F32) * MLA_Q_SCALE
    kv = jnp.dot(_rms(ckv_ref[...], kvg_ref[...]).astype(BF16), wkv_ref[...], preferred_element_type=F32)
    hw = 2 * LANES
    for h in range(MLA_HEADS):
        qh = qq[:, h * hw:(h + 1) * hw]
        q_ref[h, :, :MLA_NOPE] = qh[:, :MLA_NOPE].astype(q_ref.dtype)
        q_ref[h, :, MLA_NOPE:] = (qh[:, MLA_NOPE:MLA_NOPE + MLA_ROPE] * cos
                                  + qh[:, MLA_NOPE + MLA_ROPE:] * sin).astype(q_ref.dtype)
        k_ref[h, :, :MLA_NOPE] = kv[:, h * hw:h * hw + MLA_NOPE].astype(k_ref.dtype)
        k_ref[h, :, MLA_NOPE:] = k_rope
        v_ref[h] = kv[:, h * hw + MLA_NOPE:(h + 1) * hw].astype(v_ref.dtype)


def mla_proj(zf, q_gain, kv_gain, wq, wkv, cos, sin, b, s, tm=512):
    ns = s // tm
    nh = MLA_HEADS
    dqk = MLA_NOPE + MLA_ROPE
    out_map = lambda i: (i // ns, 0, i % ns, 0)
    return pl.pallas_call(
        _mla_proj_kernel,
        grid=(b * ns,),
        in_specs=[pl.BlockSpec((tm, MLA_Q_LORA), lambda i: (i, ZF_CQ // MLA_Q_LORA)),
                  pl.BlockSpec((tm, MLA_KV_LORA), lambda i: (i, ZF_CKV // MLA_KV_LORA)),
                  pl.BlockSpec((tm, LANES), lambda i: (i, ZF_KPE // LANES)),
                  pl.BlockSpec((1, MLA_Q_LORA), lambda i: (0, 0)),
                  pl.BlockSpec((1, MLA_KV_LORA), lambda i: (0, 0)),
                  pl.BlockSpec(wq.shape, lambda i: (0, 0)),
                  pl.BlockSpec(wkv.shape, lambda i: (0, 0)),
                  pl.BlockSpec((tm, MLA_ROPE), lambda i: (i % ns, 0)),
                  pl.BlockSpec((tm, MLA_ROPE), lambda i: (i % ns, 0))],
        out_specs=[pl.BlockSpec((None, nh, tm, dqk), out_map),
                   pl.BlockSpec((None, nh, tm, dqk), out_map),
                   pl.BlockSpec((None, nh, tm, MLA_V), out_map)],
        out_shape=[jax.ShapeDtypeStruct((b, nh, s, dqk), BF16),
                   jax.ShapeDtypeStruct((b, nh, s, dqk), BF16),
                   jax.ShapeDtypeStruct((b, nh, s, MLA_V), BF16)],
        compiler_params=_params("parallel"),
        name="mla_proj",
    )(zf, zf, zf, q_gain.reshape(1, -1), kv_gain.reshape(1, -1), wq, wkv, cos, sin)


def _key_chunk(s):
    return min(1024, s)


def _fill_v_ones(vaug_ref, v_ref):
    dv = v_ref.shape[1]
    vaug_ref[:, :dv] = v_ref[...]
    vaug_ref[:, dv:] = jnp.ones((v_ref.shape[0], vaug_ref.shape[1] - dv), vaug_ref.dtype)


def _softmax_pv(lhs, k_ref, vaug_ref, dv, bias_fn=None):
    chunk = _key_chunk(k_ref.shape[0])
    m = acc = None
    for c in range(k_ref.shape[0] // chunk):
        ks = slice(c * chunk, (c + 1) * chunk)
        sc = lax.dot_general(lhs, k_ref[ks, :], _NT, preferred_element_type=F32)
        if bias_fn is not None:
            sc = sc + bias_fn(c)
        top = jnp.max(sc, axis=-1, keepdims=True)
        m_new = top if m is None else jnp.maximum(m, top)
        pv = jnp.dot(jnp.exp2(sc - m_new).astype(BF16), vaug_ref[ks, :], preferred_element_type=F32)
        acc = pv if acc is None else acc * jnp.exp2(m - m_new) + pv
        m = m_new
    return acc[:, :dv] / acc[:, dv:]


def _mla_attn_kernel(q_ref, k_ref, v_ref, o_ref, vaug_ref):
    @pl.when(pl.program_id(2) == 0)
    def _():
        _fill_v_ones(vaug_ref, v_ref)

    o_ref[...] = _softmax_pv(q_ref[...], k_ref, vaug_ref, MLA_V).astype(o_ref.dtype)


def mla_attn(q, k, v, tq=1024):
    b, h, s, dqk = q.shape
    tq = min(tq, s)
    nq = s // tq
    return pl.pallas_call(
        _mla_attn_kernel,
        grid=(b, h, nq),
        in_specs=[pl.BlockSpec((None, None, tq, dqk), lambda b_, h_, i: (b_, h_, i, 0)),
                  pl.BlockSpec((None, None, s, dqk), lambda b_, h_, i: (b_, h_, 0, 0)),
                  pl.BlockSpec((None, None, s, MLA_V), lambda b_, h_, i: (b_, h_, 0, 0))],
        out_specs=pl.BlockSpec((tq, MLA_V), lambda b_, h_, i: (b_ * nq + i, h_)),
        out_shape=jax.ShapeDtypeStruct((b * s, h * MLA_V), F32),
        scratch_shapes=[pltpu.VMEM((s, 2 * MLA_V), BF16)],
        compiler_params=_params("parallel", "parallel", "arbitrary"),
        name="mla_attn",
    )(q, k, v)


def _diff_attn_kernel(q_ref, k_ref, v_ref, bias_ref, lv_ref, sub_ref, o_ref, vaug_ref, *, lam_init, rows):
    i = pl.program_id(2)
    tq = q_ref.shape[0]
    s = k_ref.shape[0]
    d = DIFF_HEAD_DIM

    @pl.when(i == 0)
    def _():
        _fill_v_ones(vaug_ref, v_ref)

    lv = lv_ref[...]
    lam = (jnp.exp(jnp.sum(lv[0:1] * lv[1:2], axis=-1, keepdims=True))
           - jnp.exp(jnp.sum(lv[2:3] * lv[3:4], axis=-1, keepdims=True)) + lam_init)
    q = q_ref[...]
    lane = lax.broadcasted_iota(jnp.int32, q.shape, 1)
    zero = jnp.zeros_like(q)
    lhs = jnp.concatenate([jnp.where(lane < d, q, zero), jnp.where(lane >= d, q, zero)], axis=0)

    def bias(c):
        chunk = _key_chunk(s)
        groups = [bias_ref[:, pl.ds(pl.multiple_of(s - rows - (i * tq + g * rows), rows) + c * chunk, chunk)]
                  for g in range(tq // rows)]
        return jnp.concatenate(groups + groups, axis=0)

    o = _softmax_pv(lhs, k_ref, vaug_ref, 2 * d, bias)
    o = o[:tq] - lam * o[tq:]
    o_ref[...] = (_rms(o, sub_ref[...]) * (1.0 - lam_init)).astype(o_ref.dtype)


def _alibi_table(n_heads, s, rows):
    slopes = 2.0 ** (-8.0 * jnp.arange(1, n_heads + 1, dtype=F32) / n_heads)
    r = jnp.arange(rows, dtype=jnp.int32)[:, None]
    c = jnp.arange(2 * s - rows, dtype=jnp.int32)[None, :]
    dist = jnp.abs(r - c + (s - rows)).astype(F32)
    return (-LOG2E * slopes)[:, None, None] * dist[None]


def diff_attn(zb, lam_vecs, subln, lam_init, bias_tbl, b, s, tq=512):
    tq = min(tq, s)
    nq = s // tq
    hd = 2 * DIFF_HEAD_DIM
    rows, width = bias_tbl.shape[1:]
    return pl.pallas_call(
        functools.partial(_diff_attn_kernel, lam_init=lam_init, rows=rows),
        grid=(b, DIFF_HEADS, nq),
        in_specs=[pl.BlockSpec((tq, hd), lambda b_, h, i: (b_ * nq + i, ZB_DQ // hd + h)),
                  pl.BlockSpec((s, hd), lambda b_, h, i: (b_, ZB_DK // hd + h)),
                  pl.BlockSpec((s, hd), lambda b_, h, i: (b_, ZB_DV // hd + h)),
                  pl.BlockSpec((None, rows, width), lambda b_, h, i: (h, 0, 0)),
                  pl.BlockSpec((4, DIFF_HEAD_DIM), lambda b_, h, i: (0, 0)),
                  pl.BlockSpec((1, hd), lambda b_, h, i: (0, 0))],
        out_specs=pl.BlockSpec((tq, hd), lambda b_, h, i: (b_ * nq + i, h)),
        out_shape=jax.ShapeDtypeStruct((b * s, DIFF_HEADS * hd), F32),
        scratch_shapes=[pltpu.VMEM((s, 2 * hd), BF16)],
        compiler_params=_params("parallel", "parallel", "arbitrary"),
        name="diff_attn",
    )(zb, zb, zb, bias_tbl, lam_vecs, subln.reshape(1, hd))


def _swa_kernel(sink_ref, q_ref, k_ref, v_ref, o_ref, *, seq):
    w = SWA_WINDOW
    d = SWA_HEAD_DIM
    n = pl.program_id(1)
    start = pl.multiple_of(jnp.clip((n - 1) * w, 0, seq - 3 * w), w)
    kband = k_ref[pl.ds(start, 3 * w), :]
    vband = v_ref[pl.ds(start, 3 * w), :]
    qpos = n * w + lax.broadcasted_iota(jnp.int32, (2 * w, 3 * w), 0) % w
    kpos = start + lax.broadcasted_iota(jnp.int32, (2 * w, 3 * w), 1)
    rel = jnp.abs(qpos - kpos)
    valid = rel <= w
    relf = rel.astype(F32)
    lane = lax.broadcasted_iota(jnp.int32, (w, 2 * d), 1)
    scale = d ** -0.5
    for pair in range(SWA_HEADS // 2):
        g = (2 * pair) // SWA_REP
        qp = q_ref[:, pair * 2 * d:(pair + 1) * 2 * d]
        zero = jnp.zeros_like(qp)
        qs = jnp.concatenate([jnp.where(lane < d, qp, zero), jnp.where(lane >= d, qp, zero)], axis=0)
        sc = lax.dot_general(qs, kband[:, g * 2 * d:(g + 1) * 2 * d], _NT,
                             preferred_element_type=F32) * scale
        row = lax.broadcasted_iota(jnp.int32, (2 * w, 1), 0)
        h0 = 2 * pair
        slope = jnp.where(row < w, 2.0 ** (-8.0 * (h0 + 1) / SWA_HEADS), 2.0 ** (-8.0 * (h0 + 2) / SWA_HEADS))
        sink = jnp.where(row < w, sink_ref[h0], sink_ref[h0 + 1])
        sc = jnp.where(valid, sc - slope * relf, -jnp.inf)
        m = jnp.maximum(jnp.max(sc, axis=-1, keepdims=True), sink)
        e = jnp.exp(sc - m)
        l = jnp.sum(e, axis=-1, keepdims=True) + jnp.exp(sink - m)
        p = (e * (1.0 / l)).astype(BF16)
        o = jnp.dot(p, vband[:, g * 2 * d:(g + 1) * 2 * d], preferred_element_type=F32)
        o_ref[:, pair * 2 * d:(pair + 1) * 2 * d] = jnp.where(lane < d, o[:w], o[w:]).astype(o_ref.dtype)


def swa_attn(zb, sink, b, s):
    w = SWA_WINDOW
    nb = s // w
    kvw = SWA_KV_HEADS * 2 * SWA_HEAD_DIM
    qw = SWA_HEADS * SWA_HEAD_DIM
    return pl.pallas_call(
        functools.partial(_swa_kernel, seq=s),
        grid=(b, nb),
        in_specs=[pl.BlockSpec(memory_space=pltpu.SMEM),
                  pl.BlockSpec((w, qw), lambda b_, n: (b_ * nb + n, ZB_SQ // qw)),
                  pl.BlockSpec((s, kvw), lambda b_, n: (b_, ZB_SK // kvw)),
                  pl.BlockSpec((s, kvw), lambda b_, n: (b_, ZB_SV // kvw))],
        out_specs=pl.BlockSpec((w, qw), lambda b_, n: (b_ * nb + n, 0)),
        out_shape=jax.ShapeDtypeStruct((b * s, qw), F32),
        compiler_params=_params("parallel", "arbitrary"),
        name="swa_attn",
    )(sink, zb, zb, zb)


def _conv_kernel(x_ref, w_ref, b_ref, o_ref, *, kscale):
    x = x_ref[...]
    s = x.shape[0]
    row = lax.broadcasted_iota(jnp.int32, x.shape, 0)
    pad = MLSTM_CONV // 2
    y = x * w_ref[pad:pad + 1, :] + b_ref[...]
    for j in range(MLSTM_CONV):
        off = j - pad
        if off == 0:
            continue
        shifted = pltpu.roll(x, (-off) % s, 0)
        ok = (row >= -off) if off < 0 else (row < s - off)
        y = y + jnp.where(ok, shifted, 0.0) * w_ref[j:j + 1, :]
    y = y * _sigmoid(y)
    y = y * jnp.where(pl.program_id(1) >= pl.num_programs(1) // 2, kscale, 1.0)
    o_ref[...] = y.astype(o_ref.dtype)


def mlstm_conv(zf, conv_w, conv_b, b, s, tc=256):
    c = conv_w.shape[1]
    nc = c // tc
    return pl.pallas_call(
        functools.partial(_conv_kernel, kscale=MLSTM_QK ** -0.5),
        grid=(b, nc),
        in_specs=[pl.BlockSpec((s, tc), lambda b_, j: (b_, ZF_MQK // tc + j)),
                  pl.BlockSpec((MLSTM_CONV, tc), lambda b_, j: (0, j)),
                  pl.BlockSpec((1, tc), lambda b_, j: (0, j))],
        out_specs=pl.BlockSpec((s, tc), lambda b_, j: (b_, j)),
        out_shape=jax.ShapeDtypeStruct((b * s, c), BF16),
        compiler_params=_params("parallel", "arbitrary"),
        name="mlstm_conv",
    )(zf, conv_w, conv_b.reshape(1, c))


def _scan(x, row, op, fill, reverse):
    s = x.shape[0]
    d = 1
    while d < s:
        if reverse:
            shifted = jnp.where(row < s - d, pltpu.roll(x, s - d, 0), fill)
        else:
            shifted = jnp.where(row >= d, pltpu.roll(x, d, 0), fill)
        x = op(x, shifted)
        d *= 2
    return x


def _gate_kernel(g_ref, gb_ref, col_ref, row_ref):
    nh = MLSTM_HEADS
    g = g_ref[...] + gb_ref[...]
    ls = jnp.minimum(g, 0.0) - jnp.log1p(jnp.exp(-jnp.abs(g)))
    row = lax.broadcasted_iota(jnp.int32, g.shape, 0)
    lane = lax.broadcasted_iota(jnp.int32, g.shape, 1)
    cum_f = pltpu.roll(_scan(ls, row, jnp.add, 0.0, False), LANES - nh, 1)
    cum_b = pltpu.roll(_scan(ls, row, jnp.add, 0.0, True), LANES - nh, 1)
    a_f = g - cum_f
    a_b = g - cum_b
    top_f = _scan(a_f, row, jnp.maximum, -jnp.inf, False)
    top_b = _scan(a_b, row, jnp.maximum, -jnp.inf, True)
    m_f = cum_f + top_f
    m_b = cum_b + top_b
    src_t = (jnp.where(lane < 2 * nh, a_f, a_b) * LOG2E).T
    sub = lax.broadcasted_iota(jnp.int32, row_ref.shape[1:], 0)
    for h in range(nh):
        hb = 2 * nh + h
        col_ref[h] = jnp.where(lane == 0, top_f[:, h:h + 1] * LOG2E,
                     jnp.where(lane == 1, m_f[:, h:h + 1],
                     jnp.where(lane == 2, top_b[:, hb:hb + 1] * LOG2E,
                     jnp.where(lane == 3, m_b[:, hb:hb + 1],
                     jnp.where(lane == 4, a_f[:, h:h + 1] * LOG2E, a_b[:, hb:hb + 1] * LOG2E)))))
        row_ref[h] = jnp.where(sub == 0, src_t[h:h + 1, :],
                     jnp.where(sub == 1, src_t[2 * nh + h:2 * nh + h + 1, :], 0.0))


def mlstm_gates(zf, gate_b, b, s):
    gb = jnp.zeros((1, LANES), F32).at[0, :4 * MLSTM_HEADS].set(gate_b.reshape(-1))
    nh = MLSTM_HEADS
    return pl.pallas_call(
        _gate_kernel,
        grid=(b,),
        in_specs=[pl.BlockSpec((s, LANES), lambda b_: (b_, ZF_GATES // LANES)),
                  pl.BlockSpec((1, LANES), lambda b_: (0, 0))],
        out_specs=[pl.BlockSpec((None, nh, s, LANES), lambda b_: (b_, 0, 0, 0)),
                   pl.BlockSpec((None, nh, 8, s), lambda b_: (b_, 0, 0, 0))],
        out_shape=[jax.ShapeDtypeStruct((b, nh, s, LANES), F32),
                   jax.ShapeDtypeStruct((b, nh, 8, s), F32)],
        compiler_params=_params("parallel"),
        name="mlstm_gates",
    )(zf, gb)


MLSTM_SCAN_CHUNK = 512


def _mlstm_scan_kernel(q_ref, k_ref, v_ref, col_ref, row_ref, og_ref, o_ref, hf_ref):
    s = q_ref.shape[0]
    ch = min(MLSTM_SCAN_CHUNK, s)
    nc = s // ch
    tpos = lax.broadcasted_iota(jnp.int32, (ch, ch), 0)
    spos = lax.broadcasted_iota(jnp.int32, (ch, ch), 1)

    def sweep(order, lane0, src_row, mask, edge_row, emit):
        state = zsum = edge = None
        for n, j in enumerate(order):
            rows = slice(j * ch, (j + 1) * ch)
            q, k, v = q_ref[rows, :], k_ref[rows, :], v_ref[rows, :]
            col = col_ref[rows, :]
            top = col[:, lane0:lane0 + 1]
            sc = lax.dot_general(q, k, _NT, preferred_element_type=F32)
            wgt = jnp.where(mask, jnp.exp2(row_ref[src_row:src_row + 1, rows] - top), 0.0) * sc
            num = jnp.dot(wgt.astype(BF16), v, preferred_element_type=F32)
            den = jnp.sum(wgt, axis=-1, keepdims=True)
            if state is not None:
                carry = jnp.exp2(edge - top)
                num = num + carry * jnp.dot(q, state.astype(BF16), preferred_element_type=F32)
                den = den + carry * jnp.sum(q.astype(F32) * zsum, axis=-1, keepdims=True)
            emit(rows, num / jnp.maximum(jnp.abs(den), jnp.exp(-col[:, lane0 + 1:lane0 + 2])))
            if n + 1 < nc:
                new_edge = col_ref[edge_row(j):edge_row(j) + 1, lane0:lane0 + 1]
                a_col = col[:, lane0 // 2 + 4:lane0 // 2 + 5]
                kw = k.astype(F32) * jnp.exp2(a_col - new_edge)
                upd = jnp.dot(kw.T.astype(BF16), v, preferred_element_type=F32)
                zupd = jnp.sum(kw.astype(BF16).astype(F32), axis=0, keepdims=True)
                if state is None:
                    state, zsum = upd, zupd
                else:
                    decay = jnp.exp2(edge - new_edge)
                    state, zsum = decay * state + upd, decay * zsum + zupd
                edge = new_edge

    def emit_forward(rows, h):
        hf_ref[rows, :] = h

    def emit_backward(rows, h):
        o_ref[rows, :] = (_sigmoid(og_ref[rows, :]) * (hf_ref[rows, :] + h)).astype(o_ref.dtype)

    sweep(range(nc), 0, 0, tpos >= spos, lambda j: (j + 1) * ch - 1, emit_forward)
    sweep(range(nc - 1, -1, -1), 2, 1, tpos <= spos, lambda j: j * ch, emit_backward)


def mlstm_attn(qk, zb, zf, col, row, b, s):
    nh, dk, dv = MLSTM_HEADS, MLSTM_QK, MLSTM_V
    return pl.pallas_call(
        _mlstm_scan_kernel,
        grid=(b, nh),
        in_specs=[pl.BlockSpec((s, dk), lambda b_, h: (b_, h)),
                  pl.BlockSpec((s, dk), lambda b_, h: (b_, nh + h)),
                  pl.BlockSpec((s, dv), lambda b_, h: (b_, ZB_MV // dv + h)),
                  pl.BlockSpec((None, None, s, LANES), lambda b_, h: (b_, h, 0, 0)),
                  pl.BlockSpec((None, None, 8, s), lambda b_, h: (b_, h, 0, 0)),
                  pl.BlockSpec((s, dv), lambda b_, h: (b_, ZF_MO // dv + h))],
        out_specs=pl.BlockSpec((s, dv), lambda b_, h: (b_, h)),
        out_shape=jax.ShapeDtypeStruct((b * s, nh * dv), F32),
        scratch_shapes=[pltpu.VMEM((s, dv), F32)],
        compiler_params=_params("parallel", "arbitrary"),
        name="mlstm_attn",
    )(qk, qk, zb, col, row, zf)


def _rot_half_cols(w):
    half = w.shape[-1] // 2
    return jnp.concatenate([-w[..., half:], w[..., :half]], axis=-1)


def _split_w_in(w):
    a0 = 0
    b0 = MLA_COLS
    c0 = b0 + MLSTM_COLS
    d0 = c0 + DIFF_COLS
    qk_w = 2 * MLSTM_HEADS * MLSTM_QK
    v_w = MLSTM_HEADS * MLSTM_V
    kpe = w[:, a0 + MLA_Q_LORA + MLA_KV_LORA:a0 + MLA_COLS]
    gates = w[:, b0 + qk_w + 2 * v_w:c0]
    pad = jnp.zeros((w.shape[0], LANES - gates.shape[1]), w.dtype)
    wf = jnp.concatenate([
        w[:, a0:a0 + MLA_Q_LORA],
        w[:, b0:b0 + qk_w],
        w[:, b0 + qk_w + v_w:b0 + qk_w + 2 * v_w],
        w[:, a0 + MLA_Q_LORA:a0 + MLA_Q_LORA + MLA_KV_LORA],
        kpe, _rot_half_cols(kpe), gates, pad], axis=1)
    q_w = SWA_HEADS * SWA_HEAD_DIM
    kv_w = SWA_KV_HEADS * SWA_HEAD_DIM

    def dup(t):
        t = t.reshape(t.shape[0], SWA_KV_HEADS, 1, SWA_HEAD_DIM)
        return jnp.broadcast_to(t, (t.shape[0], SWA_KV_HEADS, 2, SWA_HEAD_DIM)).reshape(t.shape[0], -1)

    dq_w = DIFF_HEADS * 2 * DIFF_HEAD_DIM
    wb = jnp.concatenate([
        w[:, b0 + qk_w:b0 + qk_w + v_w],
        w[:, c0:c0 + dq_w] * DIFF_Q_SCALE,
        w[:, c0 + dq_w:d0],
        w[:, d0:d0 + q_w],
        dup(w[:, d0 + q_w:d0 + q_w + kv_w]),
        dup(w[:, d0 + q_w + kv_w:d0 + q_w + 2 * kv_w])], axis=1)
    assert wf.shape[1] == ZF_COLS and wb.shape[1] == ZB_COLS
    return wf.astype(BF16), wb.astype(BF16)


def _mla_weights(w_uq, w_ukv):
    wq = w_uq.reshape(MLA_Q_LORA, MLA_HEADS, MLA_NOPE + MLA_ROPE)
    pe = wq[..., MLA_NOPE:]
    wq = jnp.concatenate([wq[..., :MLA_NOPE], pe, _rot_half_cols(pe)], axis=-1)
    return wq.reshape(MLA_Q_LORA, -1).astype(BF16), w_ukv.astype(BF16)


def _rope_tables(s):
    half = MLA_ROPE // 2
    inv = ROPE_THETA ** (-jnp.arange(half, dtype=F32) / half)
    ang = jnp.arange(s).astype(F32)[:, None] * inv[None, :]
    cos, sin = jnp.cos(ang), jnp.sin(ang)
    return jnp.concatenate([cos, cos], axis=-1), jnp.concatenate([sin, sin], axis=-1)


def mixers(zf, zb, l, b, s, cos, sin, bias_tbl, mla_q_norm, mla_kv_norm, mla_w_uq, mla_w_ukv, mlstm_conv_w,
           mlstm_conv_b, mlstm_gate_b, diff_lambda, diff_subln, swa_sink):
    wq, wkv = _mla_weights(mla_w_uq, mla_w_ukv)
    q, k, v = mla_proj(zf, mla_q_norm, mla_kv_norm, wq, wkv, cos, sin, b, s)
    y_a = mla_attn(q, k, v)
    qk = mlstm_conv(zf, mlstm_conv_w, mlstm_conv_b, b, s)
    col, row = mlstm_gates(zf, mlstm_gate_b, b, s)
    y_b = mlstm_attn(qk, zb, zf, col, row, b, s)
    lam_init = 0.8 - 0.6 * math.exp(-0.3 * l)
    y_c = diff_attn(zb, diff_lambda, diff_subln, lam_init, bias_tbl, b, s)
    y_d = swa_attn(zb, swa_sink, b, s)
    return y_a, y_b, y_c, y_d


def kernel(x, norm_gains, w_in, mla_q_norm, mla_kv_norm, mla_w_uq, mla_w_ukv, mlstm_conv_w,
           mlstm_conv_b, mlstm_gate_b, diff_lambda, diff_subln, swa_sink, group_norm, w_out,
           ffn1_w_gu, ffn1_w_down, ffn2_w_gu, ffn2_w_down):
    b, s, d = x.shape
    depth = w_in.shape[0]
    cos, sin = _rope_tables(s)
    bias_tbl = _alibi_table(DIFF_HEADS, s, ATTN_ROWS)
    w_down1, w_down2, w_o = cast_bf16(ffn1_w_down), cast_bf16(ffn2_w_down), cast_bf16(w_out)
    x = x.reshape(b * s, d)
    xn = prenorm(x, norm_gains[0, 0])
    for l in range(depth):
        g = norm_gains[l]
        h = swiglu_up(xn, ffn1_w_gu, l)
        x, xn = resid_update(h, w_down1, l, x, g[1], g[2], 0.5)
        wf, wb = _split_w_in(w_in[l])
        zf = matmul(xn, wf, F32, 1024, 768, "proj_f32")
        zb = matmul(xn, wb, BF16, 1024, 512, "proj_bf16")
        ys = mixers(zf, zb, l, b, s, cos, sin, bias_tbl, mla_q_norm[l], mla_kv_norm[l], mla_w_uq[l], mla_w_ukv[l],
                    mlstm_conv_w[l], mlstm_conv_b[l], mlstm_gate_b[l], diff_lambda[l], diff_subln[l],
                    swa_sink[l])
        yn = group_rmsnorm(*ys, group_norm[l])
        x, xn = resid_update(yn, w_o, l, x, g[3], g[4], 1.0)
        h = swiglu_up(xn, ffn2_w_gu, l)
        g_next = norm_gains[l + 1, 0] if l + 1 < depth else g[5]
        x, xn = resid_update(h, w_down2, l, x, g[5], g_next, 0.5)
    return x.reshape(b, s, d)
```

```python
import functools
import math

import jax
import jax.numpy as jnp
from jax import lax
from jax.experimental import pallas as pl
from jax.experimental.pallas import tpu as pltpu

F32 = jnp.float32
BF16 = jnp.bfloat16
NORM_EPS = 1e-6
ROPE_THETA = 10000.0
LOG2E = math.log2(math.e)
LANES = 128
ATTN_ROWS = 128
VMEM_LIMIT_BYTES = 60 * 1024 * 1024

D_MODEL = 4096
GROUP_WIDTH = D_MODEL // 4
D_FF = (3 * D_MODEL) // 2
MLA_HEADS = GROUP_WIDTH // 128
MLA_Q_LORA = D_MODEL // 4
MLA_KV_LORA = D_MODEL // 8
MLA_NOPE = 128
MLA_ROPE = 64
MLA_V = GROUP_WIDTH // MLA_HEADS
MLSTM_HEADS = 4
MLSTM_V = GROUP_WIDTH // MLSTM_HEADS
MLSTM_QK = MLSTM_V // 2
MLSTM_CONV = 5
DIFF_HEADS = 8
DIFF_HEAD_DIM = GROUP_WIDTH // (2 * DIFF_HEADS)
SWA_HEADS = 16
SWA_KV_HEADS = 2
SWA_HEAD_DIM = GROUP_WIDTH // SWA_HEADS
SWA_WINDOW = 128
SWA_REP = SWA_HEADS // SWA_KV_HEADS

MLA_Q_SCALE = (MLA_NOPE + MLA_ROPE) ** -0.5 * LOG2E
DIFF_Q_SCALE = DIFF_HEAD_DIM ** -0.5 * LOG2E

MLA_COLS = MLA_Q_LORA + MLA_KV_LORA + MLA_ROPE
MLSTM_COLS = 2 * MLSTM_HEADS * MLSTM_QK + 2 * MLSTM_HEADS * MLSTM_V + 4 * MLSTM_HEADS
DIFF_COLS = 3 * DIFF_HEADS * 2 * DIFF_HEAD_DIM
SWA_COLS = (SWA_HEADS + 2 * SWA_KV_HEADS) * SWA_HEAD_DIM

ZF_CQ = 0
ZF_MQK = 1024
ZF_MO = 2048
ZF_CKV = 3072
ZF_KPE = 3584
ZF_GATES = 3712
ZF_COLS = 3840
ZB_MV = 0
ZB_DQ = 1024
ZB_DK = 2048
ZB_DV = 3072
ZB_SQ = 4096
ZB_SK = 5120
ZB_SV = 5376
ZB_COLS = 5632


def _params(*sem):
    return pltpu.CompilerParams(dimension_semantics=sem, vmem_limit_bytes=VMEM_LIMIT_BYTES)


def _rms(x, g):
    return x * lax.rsqrt(jnp.mean(x * x, axis=-1, keepdims=True) + NORM_EPS) * g


def _sigmoid(x):
    return 1.0 / (1.0 + jnp.exp(-x))


_NT = (((1,), (1,)), ((), ()))


def _prenorm_kernel(x_ref, g_ref, o_ref):
    o_ref[...] = _rms(x_ref[...], g_ref[...]).astype(o_ref.dtype)


def prenorm(x, g, tm=512):
    t, d = x.shape
    return pl.pallas_call(
        _prenorm_kernel,
        grid=(t // tm,),
        in_specs=[pl.BlockSpec((tm, d), lambda i: (i, 0)), pl.BlockSpec((1, d), lambda i: (0, 0))],
        out_specs=pl.BlockSpec((tm, d), lambda i: (i, 0)),
        out_shape=jax.ShapeDtypeStruct((t, d), BF16),
        compiler_params=_params("parallel"),
        name="prenorm",
    )(x, g.reshape(1, d))


def _matmul_kernel(a_ref, w_ref, o_ref):
    o_ref[...] = jnp.dot(a_ref[...], w_ref[...], preferred_element_type=F32).astype(o_ref.dtype)


def matmul(a, w, out_dtype, tm, tn, name):
    m, k = a.shape
    n = w.shape[1]
    return pl.pallas_call(
        _matmul_kernel,
        grid=(m // tm, n // tn),
        in_specs=[pl.BlockSpec((tm, k), lambda i, j: (i, 0)), pl.BlockSpec((k, tn), lambda i, j: (0, j))],
        out_specs=pl.BlockSpec((tm, tn), lambda i, j: (i, j)),
        out_shape=jax.ShapeDtypeStruct((m, n), out_dtype),
        compiler_params=_params("parallel", "arbitrary"),
        name=name,
    )(a, w)


def _swiglu_kernel(a_ref, wg_ref, wu_ref, o_ref, wgb_ref, wub_ref):
    @pl.when(pl.program_id(1) == 0)
    def _():
        wgb_ref[...] = wg_ref[...].astype(BF16)
        wub_ref[...] = wu_ref[...].astype(BF16)

    a = a_ref[...]
    g = jnp.dot(a, wgb_ref[...], preferred_element_type=F32)
    u = jnp.dot(a, wub_ref[...], preferred_element_type=F32)
    o_ref[...] = (g * _sigmoid(g) * u).astype(o_ref.dtype)


def swiglu_up(a, w_gu, l, tm=2048, tn=256):
    m, k = a.shape
    f = w_gu.shape[2] // 2
    nj = f // tn
    return pl.pallas_call(
        _swiglu_kernel,
        grid=(nj, m // tm),
        in_specs=[pl.BlockSpec((tm, k), lambda j, i: (i, 0)),
                  pl.BlockSpec((None, k, tn), lambda j, i: (l, 0, j)),
                  pl.BlockSpec((None, k, tn), lambda j, i: (l, 0, j + nj))],
        out_specs=pl.BlockSpec((tm, tn), lambda j, i: (i, j)),
        out_shape=jax.ShapeDtypeStruct((m, f), BF16),
        scratch_shapes=[pltpu.VMEM((k, tn), BF16), pltpu.VMEM((k, tn), BF16)],
        compiler_params=_params("parallel", "arbitrary"),
        name="swiglu_up",
    )(a, w_gu, w_gu)


def _cast_kernel(x_ref, o_ref):
    o_ref[...] = x_ref[...].astype(o_ref.dtype)


def cast_bf16(w, tr=512):
    nl, r, c = w.shape
    spec = pl.BlockSpec((None, tr, c), lambda l, i: (l, i, 0))
    return pl.pallas_call(
        _cast_kernel,
        grid=(nl, r // tr),
        in_specs=[spec],
        out_specs=spec,
        out_shape=jax.ShapeDtypeStruct(w.shape, BF16),
        compiler_params=_params("parallel", "parallel"),
        name="cast_bf16",
    )(w)


RESID_STEPS = 8


def _resid_kernel(a_ref, w_ref, x_ref, gp_ref, gn_ref, xo_ref, xn_ref, acc0_ref, acc1_ref, *, coef, n_tiles):
    i = pl.program_id(0)
    k = pl.program_id(1)
    rs = x_ref.shape[0]

    def epilogue(prev_ref):
        rows = pl.ds(pl.multiple_of(k * rs, rs), rs)
        xnew = x_ref[...] + coef * _rms(prev_ref[rows, :], gp_ref[...])
        xo_ref[...] = xnew
        xn_ref[...] = _rms(xnew, gn_ref[...]).astype(xn_ref.dtype)
        prev_ref[rows, :] = jnp.zeros((rs, prev_ref.shape[1]), F32)

    @pl.when((i == 0) & (k == 0))
    def _():
        acc0_ref[...] = jnp.zeros_like(acc0_ref)
        acc1_ref[...] = jnp.zeros_like(acc1_ref)

    for parity, (cur_ref, prev_ref) in enumerate(((acc0_ref, acc1_ref), (acc1_ref, acc0_ref))):
        @pl.when((i % 2 == parity) & (i < n_tiles))
        def _(cur_ref=cur_ref, prev_ref=prev_ref):
            epilogue(prev_ref)
            cur_ref[...] += jnp.dot(a_ref[...], w_ref[...], preferred_element_type=F32)

        @pl.when((i % 2 == parity) & (i == n_tiles))
        def _(prev_ref=prev_ref):
            epilogue(prev_ref)


def resid_update(a, w, l, x, g_post, g_next, coef, tm=1024):
    m, kk = a.shape
    d = w.shape[2]
    n_tiles = m // tm
    tk = kk // RESID_STEPS
    rs = tm // RESID_STEPS
    row_map = lambda i, k: (jnp.where(i == 0, 0, (i - 1) * RESID_STEPS + k), 0)
    return pl.pallas_call(
        functools.partial(_resid_kernel, coef=coef, n_tiles=n_tiles),
        grid=(n_tiles + 1, RESID_STEPS),
        in_specs=[pl.BlockSpec((tm, tk), lambda i, k: (jnp.minimum(i, n_tiles - 1), k)),
                  pl.BlockSpec((None, tk, d), lambda i, k: (l, k, 0)),
                  pl.BlockSpec((rs, d), row_map),
                  pl.BlockSpec((1, d), lambda i, k: (0, 0)),
                  pl.BlockSpec((1, d), lambda i, k: (0, 0))],
        out_specs=[pl.BlockSpec((rs, d), row_map), pl.BlockSpec((rs, d), row_map)],
        out_shape=[jax.ShapeDtypeStruct((m, d), F32), jax.ShapeDtypeStruct((m, d), BF16)],
        scratch_shapes=[pltpu.VMEM((tm, d), F32), pltpu.VMEM((tm, d), F32)],
        compiler_params=_params("arbitrary", "arbitrary"),
        name="resid_update",
    )(a, w, x, g_post.reshape(1, d), g_next.reshape(1, d))


def _group_norm_kernel(a_ref, b_ref, c_ref, d_ref, g_ref, o_ref):
    w = a_ref.shape[1]
    for n, r in enumerate((a_ref, b_ref, c_ref, d_ref)):
        o_ref[:, n * w:(n + 1) * w] = _rms(r[...], g_ref[:, n * w:(n + 1) * w]).astype(o_ref.dtype)


def group_rmsnorm(ya, yb, yc, yd, g, tm=512):
    t, w = ya.shape
    spec = pl.BlockSpec((tm, w), lambda i: (i, 0))
    return pl.pallas_call(
        _group_norm_kernel,
        grid=(t // tm,),
        in_specs=[spec, spec, spec, spec, pl.BlockSpec((1, 4 * w), lambda i: (0, 0))],
        out_specs=pl.BlockSpec((tm, 4 * w), lambda i: (i, 0)),
        out_shape=jax.ShapeDtypeStruct((t, 4 * w), BF16),
        compiler_params=_params("parallel"),
        name="group_norm",
    )(ya, yb, yc, yd, g.reshape(1, 4 * w))


def _mla_proj_kernel(cq_ref, ckv_ref, kpe_ref, qg_ref, kvg_ref, wq_ref, wkv_ref, cos_ref, sin_ref,
                     q_ref, k_ref, v_ref):
    cos = cos_ref[...]
    sin = sin_ref[...]
    kp = kpe_ref[...]
    k_rope = (kp[:, :MLA_ROPE] * cos + kp[:, MLA_ROPE:] * sin).astype(k_ref.dtype)
    qq = jnp.dot(_rms(cq_ref[...], qg_ref[...]).astype(BF16), wq_ref[...],
                 preferred_element_type=F32) * MLA_Q_SCALE
    kv = jnp.dot(_rms(ckv_ref[...], kvg_ref[...]).astype(BF16), wkv_ref[...], preferred_element_type=F32)
    hw = 2 * LANES
    for h in range(MLA_HEADS):
        qh = qq[:, h * hw:(h + 1) * hw]
        q_ref[h, :, :MLA_NOPE] = qh[:, :MLA_NOPE].astype(q_ref.dtype)
        q_ref[h, :, MLA_NOPE:] = (qh[:, MLA_NOPE:MLA_NOPE + MLA_ROPE] * cos
                                  + qh[:, MLA_NOPE + MLA_ROPE:] * sin).astype(q_ref.dtype)
        k_ref[h, :, :MLA_NOPE] = kv[:, h * hw:h * hw + MLA_NOPE].astype(k_ref.dtype)
        k_ref[h, :, MLA_NOPE:] = k_rope
        v_ref[h] = kv[:, h * hw + MLA_NOPE:(h + 1) * hw].astype(v_ref.dtype)


def mla_proj(zf, q_gain, kv_gain, wq, wkv, cos, sin, b, s, tm=512):
    ns = s // tm
    nh = MLA_HEADS
    dqk = MLA_NOPE + MLA_ROPE
    out_map = lambda i: (i // ns, 0, i % ns, 0)
    return pl.pallas_call(
        _mla_proj_kernel,
        grid=(b * ns,),
        in_specs=[pl.BlockSpec((tm, MLA_Q_LORA), lambda i: (i, ZF_CQ // MLA_Q_LORA)),
                  pl.BlockSpec((tm, MLA_KV_LORA), lambda i: (i, ZF_CKV // MLA_KV_LORA)),
                  pl.BlockSpec((tm, LANES), lambda i: (i, ZF_KPE // LANES)),
                  pl.BlockSpec((1, MLA_Q_LORA), lambda i: (0, 0)),
                  pl.BlockSpec((1, MLA_KV_LORA), lambda i: (0, 0)),
                  pl.BlockSpec(wq.shape, lambda i: (0, 0)),
                  pl.BlockSpec(wkv.shape, lambda i: (0, 0)),
                  pl.BlockSpec((tm, MLA_ROPE), lambda i: (i % ns, 0)),
                  pl.BlockSpec((tm, MLA_ROPE), lambda i: (i % ns, 0))],
        out_specs=[pl.BlockSpec((None, nh, tm, dqk), out_map),
                   pl.BlockSpec((None, nh, tm, dqk), out_map),
                   pl.BlockSpec((None, nh, tm, MLA_V), out_map)],
        out_shape=[jax.ShapeDtypeStruct((b, nh, s, dqk), BF16),
                   jax.ShapeDtypeStruct((b, nh, s, dqk), BF16),
                   jax.ShapeDtypeStruct((b, nh, s, MLA_V), BF16)],
        compiler_params=_params("parallel"),
        name="mla_proj",
    )(zf, zf, zf, q_gain.reshape(1, -1), kv_gain.reshape(1, -1), wq, wkv, cos, sin)


def _key_chunk(s):
    return min(256, s)


def _fill_v_ones(vaug_ref, v_ref):
    dv = v_ref.shape[1]
    vaug_ref[:, :dv] = v_ref[...]
    vaug_ref[:, dv:] = jnp.ones((v_ref.shape[0], vaug_ref.shape[1] - dv), vaug_ref.dtype)


def _softmax_pv(lhs, k_ref, vaug_ref, dv, bias_fn=None):
    chunk = _key_chunk(k_ref.shape[0])
    m = acc = None
    for c in range(k_ref.shape[0] // chunk):
        ks = slice(c * chunk, (c + 1) * chunk)
        sc = lax.dot_general(lhs, k_ref[ks, :], _NT, preferred_element_type=F32)
        if bias_fn is not None:
            sc = sc + bias_fn(c)
        top = jnp.max(sc, axis=-1, keepdims=True)
        m_new = top if m is None else jnp.maximum(m, top)
        pv = jnp.dot(jnp.exp2(sc - m_new).astype(BF16), vaug_ref[ks, :], preferred_element_type=F32)
        acc = pv if acc is None else acc * jnp.exp2(m - m_new) + pv
        m = m_new
    return acc[:, :dv] / acc[:, dv:]


def _mla_attn_kernel(q_ref, k_ref, v_ref, o_ref, vaug_ref):
    @pl.when(pl.program_id(2) == 0)
    def _():
        _fill_v_ones(vaug_ref, v_ref)

    o_ref[...] = _softmax_pv(q_ref[...], k_ref, vaug_ref, MLA_V).astype(o_ref.dtype)


def mla_attn(q, k, v, tq=1024):
    b, h, s, dqk = q.shape
    tq = min(tq, s)
    nq = s // tq
    return pl.pallas_call(
        _mla_attn_kernel,
        grid=(b, h, nq),
        in_specs=[pl.BlockSpec((None, None, tq, dqk), lambda b_, h_, i: (b_, h_, i, 0)),
                  pl.BlockSpec((None, None, s, dqk), lambda b_, h_, i: (b_, h_, 0, 0)),
                  pl.BlockSpec((None, None, s, MLA_V), lambda b_, h_, i: (b_, h_, 0, 0))],
        out_specs=pl.BlockSpec((tq, MLA_V), lambda b_, h_, i: (b_ * nq + i, h_)),
        out_shape=jax.ShapeDtypeStruct((b * s, h * MLA_V), F32),
        scratch_shapes=[pltpu.VMEM((s, 2 * MLA_V), BF16)],
        compiler_params=_params("parallel", "parallel", "arbitrary"),
        name="mla_attn",
    )(q, k, v)


def _diff_attn_kernel(q_ref, k_ref, v_ref, bias_ref, lv_ref, sub_ref, o_ref, vaug_ref, *, lam_init, rows):
    i = pl.program_id(2)
    tq = q_ref.shape[0]
    s = k_ref.shape[0]
    d = DIFF_HEAD_DIM

    @pl.when(i == 0)
    def _():
        _fill_v_ones(vaug_ref, v_ref)

    lv = lv_ref[...]
    lam = (jnp.exp(jnp.sum(lv[0:1] * lv[1:2], axis=-1, keepdims=True))
           - jnp.exp(jnp.sum(lv[2:3] * lv[3:4], axis=-1, keepdims=True)) + lam_init)
    q = q_ref[...]
    lane = lax.broadcasted_iota(jnp.int32, q.shape, 1)
    zero = jnp.zeros_like(q)
    lhs = jnp.concatenate([jnp.where(lane < d, q, zero), jnp.where(lane >= d, q, zero)], axis=0)

    def bias(c):
        chunk = _key_chunk(s)
        groups = [bias_ref[:, pl.ds(pl.multiple_of(s - rows - (i * tq + g * rows), rows) + c * chunk, chunk)]
                  for g in range(tq // rows)]
        return jnp.concatenate(groups + groups, axis=0)

    o = _softmax_pv(lhs, k_ref, vaug_ref, 2 * d, bias)
    o = o[:tq] - lam * o[tq:]
    o_ref[...] = (_rms(o, sub_ref[...]) * (1.0 - lam_init)).astype(o_ref.dtype)


def _alibi_table(n_heads, s, rows):
    slopes = 2.0 ** (-8.0 * jnp.arange(1, n_heads + 1, dtype=F32) / n_heads)
    r = jnp.arange(rows, dtype=jnp.int32)[:, None]
    c = jnp.arange(2 * s - rows, dtype=jnp.int32)[None, :]
    dist = jnp.abs(r - c + (s - rows)).astype(F32)
    return (-LOG2E * slopes)[:, None, None] * dist[None]


def diff_attn(zb, lam_vecs, subln, lam_init, bias_tbl, b, s, tq=512):
    tq = min(tq, s)
    nq = s // tq
    hd = 2 * DIFF_HEAD_DIM
    rows, width = bias_tbl.shape[1:]
    return pl.pallas_call(
        functools.partial(_diff_attn_kernel, lam_init=lam_init, rows=rows),
        grid=(b, DIFF_HEADS, nq),
        in_specs=[pl.BlockSpec((tq, hd), lambda b_, h, i: (b_ * nq + i, ZB_DQ // hd + h)),
                  pl.BlockSpec((s, hd), lambda b_, h, i: (b_, ZB_DK // hd + h)),
                  pl.BlockSpec((s, hd), lambda b_, h, i: (b_, ZB_DV // hd + h)),
                  pl.BlockSpec((None, rows, width), lambda b_, h, i: (h, 0, 0)),
                  pl.BlockSpec((4, DIFF_HEAD_DIM), lambda b_, h, i: (0, 0)),
                  pl.BlockSpec((1, hd), lambda b_, h, i: (0, 0))],
        out_specs=pl.BlockSpec((tq, hd), lambda b_, h, i: (b_ * nq + i, h)),
        out_shape=jax.ShapeDtypeStruct((b * s, DIFF_HEADS * hd), F32),
        scratch_shapes=[pltpu.VMEM((s, 2 * hd), BF16)],
        compiler_params=_params("parallel", "parallel", "arbitrary"),
        name="diff_attn",
    )(zb, zb, zb, bias_tbl, lam_vecs, subln.reshape(1, hd))


def _swa_kernel(sink_ref, q_ref, k_ref, v_ref, o_ref, *, seq):
    w = SWA_WINDOW
    d = SWA_HEAD_DIM
    qb = q_ref.shape[0]
    band = qb + 2 * w
    n = pl.program_id(1)
    start = pl.multiple_of(jnp.clip(n * qb - w, 0, seq - band), w)
    kband = k_ref[pl.ds(start, band), :]
    vband = v_ref[pl.ds(start, band), :]
    qpos = n * qb + lax.broadcasted_iota(jnp.int32, (2 * qb, band), 0) % qb
    kpos = start + lax.broadcasted_iota(jnp.int32, (2 * qb, band), 1)
    rel = jnp.abs(qpos - kpos)
    neg_rel = jnp.where(rel <= w, -rel.astype(F32), -jnp.inf)
    lane = lax.broadcasted_iota(jnp.int32, (qb, 2 * d), 1)
    row = lax.broadcasted_iota(jnp.int32, (2 * qb, 1), 0)
    for pair in range(SWA_HEADS // 2):
        g = (2 * pair) // SWA_REP
        qp = q_ref[:, pair * 2 * d:(pair + 1) * 2 * d]
        zero = jnp.zeros_like(qp)
        qs = jnp.concatenate([jnp.where(lane < d, qp, zero), jnp.where(lane >= d, qp, zero)], axis=0)
        sc = lax.dot_general(qs, kband[:, g * 2 * d:(g + 1) * 2 * d], _NT, preferred_element_type=F32)
        h0 = 2 * pair
        slope = jnp.where(row < qb, 2.0 ** (-8.0 * (h0 + 1) / SWA_HEADS), 2.0 ** (-8.0 * (h0 + 2) / SWA_HEADS))
        sink = jnp.where(row < qb, sink_ref[h0], sink_ref[h0 + 1])
        sc = sc + slope * neg_rel
        m = jnp.maximum(jnp.max(sc, axis=-1, keepdims=True), sink)
        e = jnp.exp(sc - m)
        l = jnp.sum(e, axis=-1, keepdims=True) + jnp.exp(sink - m)
        p = (e * (1.0 / l)).astype(BF16)
        o = jnp.dot(p, vband[:, g * 2 * d:(g + 1) * 2 * d], preferred_element_type=F32)
        o_ref[:, pair * 2 * d:(pair + 1) * 2 * d] = jnp.where(lane < d, o[:qb], o[qb:]).astype(o_ref.dtype)


def swa_attn(zb, sink, b, s, qb=256):
    nb = s // qb
    kvw = SWA_KV_HEADS * 2 * SWA_HEAD_DIM
    qw = SWA_HEADS * SWA_HEAD_DIM
    return pl.pallas_call(
        functools.partial(_swa_kernel, seq=s),
        grid=(b, nb),
        in_specs=[pl.BlockSpec(memory_space=pltpu.SMEM),
                  pl.BlockSpec((qb, qw), lambda b_, n: (b_ * nb + n, ZB_SQ // qw)),
                  pl.BlockSpec((s, kvw), lambda b_, n: (b_, ZB_SK // kvw)),
                  pl.BlockSpec((s, kvw), lambda b_, n: (b_, ZB_SV // kvw))],
        out_specs=pl.BlockSpec((qb, qw), lambda b_, n: (b_ * nb + n, 0)),
        out_shape=jax.ShapeDtypeStruct((b * s, qw), F32),
        compiler_params=_params("parallel", "arbitrary"),
        name="swa_attn",
    )(sink, zb, zb, zb)


def _conv_kernel(x_ref, w_ref, b_ref, o_ref, *, kscale):
    x = x_ref[...]
    s = x.shape[0]
    row = lax.broadcasted_iota(jnp.int32, x.shape, 0)
    pad = MLSTM_CONV // 2
    y = x * w_ref[pad:pad + 1, :] + b_ref[...]
    for j in range(MLSTM_CONV):
        off = j - pad
        if off == 0:
            continue
        shifted = pltpu.roll(x, (-off) % s, 0)
        ok = (row >= -off) if off < 0 else (row < s - off)
        y = y + jnp.where(ok, shifted, 0.0) * w_ref[j:j + 1, :]
    y = y * _sigmoid(y)
    y = y * jnp.where(pl.program_id(1) >= pl.num_programs(1) // 2, kscale, 1.0)
    o_ref[...] = y.astype(o_ref.dtype)


def mlstm_conv(zf, conv_w, conv_b, b, s, tc=256):
    c = conv_w.shape[1]
    nc = c // tc
    return pl.pallas_call(
        functools.partial(_conv_kernel, kscale=MLSTM_QK ** -0.5),
        grid=(b, nc),
        in_specs=[pl.BlockSpec((s, tc), lambda b_, j: (b_, ZF_MQK // tc + j)),
                  pl.BlockSpec((MLSTM_CONV, tc), lambda b_, j: (0, j)),
                  pl.BlockSpec((1, tc), lambda b_, j: (0, j))],
        out_specs=pl.BlockSpec((s, tc), lambda b_, j: (b_, j)),
        out_shape=jax.ShapeDtypeStruct((b * s, c), BF16),
        compiler_params=_params("parallel", "arbitrary"),
        name="mlstm_conv",
    )(zf, conv_w, conv_b.reshape(1, c))


def _scan(x, row, op, fill, reverse):
    s = x.shape[0]
    d = 1
    while d < s:
        if reverse:
            shifted = jnp.where(row < s - d, pltpu.roll(x, s - d, 0), fill)
        else:
            shifted = jnp.where(row >= d, pltpu.roll(x, d, 0), fill)
        x = op(x, shifted)
        d *= 2
    return x


def _gate_kernel(g_ref, gb_ref, col_ref, row_ref):
    nh = MLSTM_HEADS
    g = g_ref[...] + gb_ref[...]
    ls = jnp.minimum(g, 0.0) - jnp.log1p(jnp.exp(-jnp.abs(g)))
    row = lax.broadcasted_iota(jnp.int32, g.shape, 0)
    lane = lax.broadcasted_iota(jnp.int32, g.shape, 1)
    cum_f = pltpu.roll(_scan(ls, row, jnp.add, 0.0, False), LANES - nh, 1)
    cum_b = pltpu.roll(_scan(ls, row, jnp.add, 0.0, True), LANES - nh, 1)
    a_f = g - cum_f
    a_b = g - cum_b
    top_f = _scan(a_f, row, jnp.maximum, -jnp.inf, False)
    top_b = _scan(a_b, row, jnp.maximum, -jnp.inf, True)
    m_f = cum_f + top_f
    m_b = cum_b + top_b
    src_t = (jnp.where(lane < 2 * nh, a_f, a_b) * LOG2E).T
    sub = lax.broadcasted_iota(jnp.int32, row_ref.shape[1:], 0)
    for h in range(nh):
        hb = 2 * nh + h
        col_ref[h] = jnp.where(lane == 0, top_f[:, h:h + 1] * LOG2E,
                     jnp.where(lane == 1, m_f[:, h:h + 1],
                     jnp.where(lane == 2, top_b[:, hb:hb + 1] * LOG2E,
                     jnp.where(lane == 3, m_b[:, hb:hb + 1],
                     jnp.where(lane == 4, a_f[:, h:h + 1] * LOG2E, a_b[:, hb:hb + 1] * LOG2E)))))
        row_ref[h] = jnp.where(sub == 0, src_t[h:h + 1, :],
                     jnp.where(sub == 1, src_t[2 * nh + h:2 * nh + h + 1, :], 0.0))


def mlstm_gates(zf, gate_b, b, s):
    gb = jnp.zeros((1, LANES), F32).at[0, :4 * MLSTM_HEADS].set(gate_b.reshape(-1))
    nh = MLSTM_HEADS
    return pl.pallas_call(
        _gate_kernel,
        grid=(b,),
        in_specs=[pl.BlockSpec((s, LANES), lambda b_: (b_, ZF_GATES // LANES)),
                  pl.BlockSpec((1, LANES), lambda b_: (0, 0))],
        out_specs=[pl.BlockSpec((None, nh, s, LANES), lambda b_: (b_, 0, 0, 0)),
                   pl.BlockSpec((None, nh, 8, s), lambda b_: (b_, 0, 0, 0))],
        out_shape=[jax.ShapeDtypeStruct((b, nh, s, LANES), F32),
                   jax.ShapeDtypeStruct((b, nh, 8, s), F32)],
        compiler_params=_params("parallel"),
        name="mlstm_gates",
    )(zf, gb)


MLSTM_SCAN_CHUNK = 512


def _mlstm_scan_kernel(q_ref, k_ref, v_ref, col_ref, row_ref, og_ref, o_ref, hf_ref):
    s = q_ref.shape[0]
    ch = min(MLSTM_SCAN_CHUNK, s)
    nc = s // ch
    tpos = lax.broadcasted_iota(jnp.int32, (ch, ch), 0)
    spos = lax.broadcasted_iota(jnp.int32, (ch, ch), 1)

    def sweep(order, lane0, src_row, mask, edge_row, emit):
        state = zsum = edge = None
        for n, j in enumerate(order):
            rows = slice(j * ch, (j + 1) * ch)
            q, k, v = q_ref[rows, :], k_ref[rows, :], v_ref[rows, :]
            col = col_ref[rows, :]
            top = col[:, lane0:lane0 + 1]
            sc = lax.dot_general(q, k, _NT, preferred_element_type=F32)
            wgt = jnp.where(mask, jnp.exp2(row_ref[src_row:src_row + 1, rows] - top), 0.0) * sc
            num = jnp.dot(wgt.astype(BF16), v, preferred_element_type=F32)
            den = jnp.sum(wgt, axis=-1, keepdims=True)
            if state is not None:
                carry = jnp.exp2(edge - top)
                num = num + carry * jnp.dot(q, state.astype(BF16), preferred_element_type=F32)
                den = den + carry * jnp.sum(q.astype(F32) * zsum, axis=-1, keepdims=True)
            emit(rows, num / jnp.maximum(jnp.abs(den), jnp.exp(-col[:, lane0 + 1:lane0 + 2])))
            if n + 1 < nc:
                new_edge = col_ref[edge_row(j):edge_row(j) + 1, lane0:lane0 + 1]
                a_col = col[:, lane0 // 2 + 4:lane0 // 2 + 5]
                kw = k.astype(F32) * jnp.exp2(a_col - new_edge)
                upd = jnp.dot(kw.T.astype(BF16), v, preferred_element_type=F32)
                zupd = jnp.sum(kw.astype(BF16).astype(F32), axis=0, keepdims=True)
                if state is None:
                    state, zsum = upd, zupd
                else:
                    decay = jnp.exp2(edge - new_edge)
                    state, zsum = decay * state + upd, decay * zsum + zupd
                edge = new_edge

    def emit_forward(rows, h):
        hf_ref[rows, :] = h

    def emit_backward(rows, h):
        o_ref[rows, :] = (_sigmoid(og_ref[rows, :]) * (hf_ref[rows, :] + h)).astype(o_ref.dtype)

    sweep(range(nc), 0, 0, tpos >= spos, lambda j: (j + 1) * ch - 1, emit_forward)
    sweep(range(nc - 1, -1, -1), 2, 1, tpos <= spos, lambda j: j * ch, emit_backward)


def mlstm_attn(qk, zb, zf, col, row, b, s):
    nh, dk, dv = MLSTM_HEADS, MLSTM_QK, MLSTM_V
    return pl.pallas_call(
        _mlstm_scan_kernel,
        grid=(b, nh),
        in_specs=[pl.BlockSpec((s, dk), lambda b_, h: (b_, h)),
                  pl.BlockSpec((s, dk), lambda b_, h: (b_, nh + h)),
                  pl.BlockSpec((s, dv), lambda b_, h: (b_, ZB_MV // dv + h)),
                  pl.BlockSpec((None, None, s, LANES), lambda b_, h: (b_, h, 0, 0)),
                  pl.BlockSpec((None, None, 8, s), lambda b_, h: (b_, h, 0, 0)),
                  pl.BlockSpec((s, dv), lambda b_, h: (b_, ZF_MO // dv + h))],
        out_specs=pl.BlockSpec((s, dv), lambda b_, h: (b_, h)),
        out_shape=jax.ShapeDtypeStruct((b * s, nh * dv), F32),
        scratch_shapes=[pltpu.VMEM((s, dv), F32)],
        compiler_params=_params("parallel", "arbitrary"),
        name="mlstm_attn",
    )(qk, qk, zb, col, row, zf)


def _rot_half_cols(w):
    half = w.shape[-1] // 2
    return jnp.concatenate([-w[..., half:], w[..., :half]], axis=-1)


def _split_w_in(w, w16=None):
    w16 = w.astype(BF16) if w16 is None else w16
    a0 = 0
    b0 = MLA_COLS
    c0 = b0 + MLSTM_COLS
    d0 = c0 + DIFF_COLS
    qk_w = 2 * MLSTM_HEADS * MLSTM_QK
    v_w = MLSTM_HEADS * MLSTM_V
    kpe = w16[:, a0 + MLA_Q_LORA + MLA_KV_LORA:a0 + MLA_COLS]
    gates = w16[:, b0 + qk_w + 2 * v_w:c0]
    pad = jnp.zeros((w.shape[0], LANES - gates.shape[1]), w16.dtype)
    wf = jnp.concatenate([
        w16[:, a0:a0 + MLA_Q_LORA],
        w16[:, b0:b0 + qk_w],
        w16[:, b0 + qk_w + v_w:b0 + qk_w + 2 * v_w],
        w16[:, a0 + MLA_Q_LORA:a0 + MLA_Q_LORA + MLA_KV_LORA],
        kpe, _rot_half_cols(kpe), gates, pad], axis=1)
    q_w = SWA_HEADS * SWA_HEAD_DIM
    kv_w = SWA_KV_HEADS * SWA_HEAD_DIM

    def dup(t):
        t = t.reshape(t.shape[0], SWA_KV_HEADS, 1, SWA_HEAD_DIM)
        return jnp.broadcast_to(t, (t.shape[0], SWA_KV_HEADS, 2, SWA_HEAD_DIM)).reshape(t.shape[0], -1)

    dq_w = DIFF_HEADS * 2 * DIFF_HEAD_DIM
    wb = jnp.concatenate([
        w16[:, b0 + qk_w:b0 + qk_w + v_w],
        (w[:, c0:c0 + dq_w] * DIFF_Q_SCALE).astype(w16.dtype),
        w16[:, c0 + dq_w:d0],
        w16[:, d0:d0 + q_w] * SWA_HEAD_DIM ** -0.5,
        dup(w16[:, d0 + q_w:d0 + q_w + kv_w]),
        dup(w16[:, d0 + q_w + kv_w:d0 + q_w + 2 * kv_w])], axis=1)
    assert wf.shape[1] == ZF_COLS and wb.shape[1] == ZB_COLS
    return wf, wb


def _mla_weights(w_uq, w_ukv):
    wq = w_uq.reshape(MLA_Q_LORA, MLA_HEADS, MLA_NOPE + MLA_ROPE)
    pe = wq[..., MLA_NOPE:]
    wq = jnp.concatenate([wq[..., :MLA_NOPE], pe, _rot_half_cols(pe)], axis=-1)
    return wq.reshape(MLA_Q_LORA, -1).astype(BF16), w_ukv.astype(BF16)


def _rope_tables(s):
    half = MLA_ROPE // 2
    inv = ROPE_THETA ** (-jnp.arange(half, dtype=F32) / half)
    ang = jnp.arange(s).astype(F32)[:, None] * inv[None, :]
    cos, sin = jnp.cos(ang), jnp.sin(ang)
    return jnp.concatenate([cos, cos], axis=-1), jnp.concatenate([sin, sin], axis=-1)


def mixers(zf, zb, l, b, s, cos, sin, bias_tbl, mla_q_norm, mla_kv_norm, mla_w_uq, mla_w_ukv, mlstm_conv_w,
           mlstm_conv_b, mlstm_gate_b, diff_lambda, diff_subln, swa_sink):
    wq, wkv = _mla_weights(mla_w_uq, mla_w_ukv)
    q, k, v = mla_proj(zf, mla_q_norm, mla_kv_norm, wq, wkv, cos, sin, b, s)
    y_a = mla_attn(q, k, v)
    qk = mlstm_conv(zf, mlstm_conv_w, mlstm_conv_b, b, s)
    col, row = mlstm_gates(zf, mlstm_gate_b, b, s)
    y_b = mlstm_attn(qk, zb, zf, col, row, b, s)
    lam_init = 0.8 - 0.6 * math.exp(-0.3 * l)
    y_c = diff_attn(zb, diff_lambda, diff_subln, lam_init, bias_tbl, b, s)
    y_d = swa_attn(zb, swa_sink, b, s)
    return y_a, y_b, y_c, y_d


def kernel(x, norm_gains, w_in, mla_q_norm, mla_kv_norm, mla_w_uq, mla_w_ukv, mlstm_conv_w,
           mlstm_conv_b, mlstm_gate_b, diff_lambda, diff_subln, swa_sink, group_norm, w_out,
           ffn1_w_gu, ffn1_w_down, ffn2_w_gu, ffn2_w_down):
    b, s, d = x.shape
    depth = w_in.shape[0]
    cos, sin = _rope_tables(s)
    bias_tbl = _alibi_table(DIFF_HEADS, s, ATTN_ROWS)
    w_down1, w_down2, w_o = cast_bf16(ffn1_w_down), cast_bf16(ffn2_w_down), cast_bf16(w_out)
    w_in16 = cast_bf16(w_in, tr=256)
    x = x.reshape(b * s, d)
    xn = prenorm(x, norm_gains[0, 0])
    for l in range(depth):
        g = norm_gains[l]
        h = swiglu_up(xn, ffn1_w_gu, l)
        x, xn = resid_update(h, w_down1, l, x, g[1], g[2], 0.5)
        wf, wb = _split_w_in(w_in[l], w_in16[l])
        zf = matmul(xn, wf, F32, 1024, 768, "proj_f32")
        zb = matmul(xn, wb, BF16, 1024, 512, "proj_bf16")
        ys = mixers(zf, zb, l, b, s, cos, sin, bias_tbl, mla_q_norm[l], mla_kv_norm[l], mla_w_uq[l], mla_w_ukv[l],
                    mlstm_conv_w[l], mlstm_conv_b[l], mlstm_gate_b[l], diff_lambda[l], diff_subln[l],
                    swa_sink[l])
        yn = group_rmsnorm(*ys, group_norm[l])
        x, xn = resid_update(yn, w_o, l, x, g[3], g[4], 1.0)
        h = swiglu_up(xn, ffn2_w_gu, l)
        g_next = norm_gains[l + 1, 0] if l + 1 < depth else g[5]
        x, xn = resid_update(h, w_down2, l, x, g[5], g_next, 0.5)
    return x.reshape(b, s, d)
```

```python
import functools
import math

import jax
import jax.numpy as jnp
from jax import lax
from jax.experimental import pallas as pl
from jax.experimental.pallas import tpu as pltpu

F32 = jnp.float32
BF16 = jnp.bfloat16
NORM_EPS = 1e-6
ROPE_THETA = 10000.0
LOG2E = math.log2(math.e)
LANES = 128
ATTN_ROWS = 128
VMEM_LIMIT_BYTES = 60 * 1024 * 1024

D_MODEL = 4096
GROUP_WIDTH = D_MODEL // 4
D_FF = (3 * D_MODEL) // 2
MLA_HEADS = GROUP_WIDTH // 128
MLA_Q_LORA = D_MODEL // 4
MLA_KV_LORA = D_MODEL // 8
MLA_NOPE = 128
MLA_ROPE = 64
MLA_V = GROUP_WIDTH // MLA_HEADS
MLSTM_HEADS = 4
MLSTM_V = GROUP_WIDTH // MLSTM_HEADS
MLSTM_QK = MLSTM_V // 2
MLSTM_CONV = 5
DIFF_HEADS = 8
DIFF_HEAD_DIM = GROUP_WIDTH // (2 * DIFF_HEADS)
SWA_HEADS = 16
SWA_KV_HEADS = 2
SWA_HEAD_DIM = GROUP_WIDTH // SWA_HEADS
SWA_WINDOW = 128
SWA_REP = SWA_HEADS // SWA_KV_HEADS

MLA_Q_SCALE = (MLA_NOPE + MLA_ROPE) ** -0.5 * LOG2E
DIFF_Q_SCALE = DIFF_HEAD_DIM ** -0.5 * LOG2E

MLA_COLS = MLA_Q_LORA + MLA_KV_LORA + MLA_ROPE
MLSTM_COLS = 2 * MLSTM_HEADS * MLSTM_QK + 2 * MLSTM_HEADS * MLSTM_V + 4 * MLSTM_HEADS
DIFF_COLS = 3 * DIFF_HEADS * 2 * DIFF_HEAD_DIM
SWA_COLS = (SWA_HEADS + 2 * SWA_KV_HEADS) * SWA_HEAD_DIM

ZF_CQ = 0
ZF_MQK = 1024
ZF_MO = 2048
ZF_CKV = 3072
ZF_KPE = 3584
ZF_GATES = 3712
ZF_COLS = 3840
ZB_MV = 0
ZB_DQ = 1024
ZB_DK = 2048
ZB_DV = 3072
ZB_SQ = 4096
ZB_SK = 5120
ZB_SV = 5376
ZB_COLS = 5632


def _params(*sem):
    return pltpu.CompilerParams(dimension_semantics=sem, vmem_limit_bytes=VMEM_LIMIT_BYTES)


def _rms(x, g):
    return x * lax.rsqrt(jnp.mean(x * x, axis=-1, keepdims=True) + NORM_EPS) * g


def _sigmoid(x):
    return 1.0 / (1.0 + jnp.exp(-x))


_NT = (((1,), (1,)), ((), ()))


def _prenorm_kernel(x_ref, g_ref, o_ref):
    o_ref[...] = _rms(x_ref[...], g_ref[...]).astype(o_ref.dtype)


def prenorm(x, g, tm=512):
    t, d = x.shape
    return pl.pallas_call(
        _prenorm_kernel,
        grid=(t // tm,),
        in_specs=[pl.BlockSpec((tm, d), lambda i: (i, 0)), pl.BlockSpec((1, d), lambda i: (0, 0))],
        out_specs=pl.BlockSpec((tm, d), lambda i: (i, 0)),
        out_shape=jax.ShapeDtypeStruct((t, d), BF16),
        compiler_params=_params("parallel"),
        name="prenorm",
    )(x, g.reshape(1, d))


def _matmul_kernel(a_ref, w_ref, o_ref):
    o_ref[...] = jnp.dot(a_ref[...], w_ref[...], preferred_element_type=F32).astype(o_ref.dtype)


def matmul(a, w, out_dtype, tm, tn, name):
    m, k = a.shape
    n = w.shape[1]
    return pl.pallas_call(
        _matmul_kernel,
        grid=(m // tm, n // tn),
        in_specs=[pl.BlockSpec((tm, k), lambda i, j: (i, 0)), pl.BlockSpec((k, tn), lambda i, j: (0, j))],
        out_specs=pl.BlockSpec((tm, tn), lambda i, j: (i, j)),
        out_shape=jax.ShapeDtypeStruct((m, n), out_dtype),
        compiler_params=_params("parallel", "arbitrary"),
        name=name,
    )(a, w)


def _swiglu_kernel(a_ref, wg_ref, wu_ref, o_ref, wgb_ref, wub_ref):
    @pl.when(pl.program_id(1) == 0)
    def _():
        wgb_ref[...] = wg_ref[...].astype(BF16)
        wub_ref[...] = wu_ref[...].astype(BF16)

    a = a_ref[...]
    g = jnp.dot(a, wgb_ref[...], preferred_element_type=F32)
    u = jnp.dot(a, wub_ref[...], preferred_element_type=F32)
    o_ref[...] = (g * _sigmoid(g) * u).astype(o_ref.dtype)


def swiglu_up(a, w_gu, l, tm=2048, tn=256):
    m, k = a.shape
    f = w_gu.shape[2] // 2
    nj = f // tn
    return pl.pallas_call(
        _swiglu_kernel,
        grid=(nj, m // tm),
        in_specs=[pl.BlockSpec((tm, k), lambda j, i: (i, 0)),
                  pl.BlockSpec((None, k, tn), lambda j, i: (l, 0, j)),
                  pl.BlockSpec((None, k, tn), lambda j, i: (l, 0, j + nj))],
        out_specs=pl.BlockSpec((tm, tn), lambda j, i: (i, j)),
        out_shape=jax.ShapeDtypeStruct((m, f), BF16),
        scratch_shapes=[pltpu.VMEM((k, tn), BF16), pltpu.VMEM((k, tn), BF16)],
        compiler_params=_params("parallel", "arbitrary"),
        name="swiglu_up",
    )(a, w_gu, w_gu)


def _cast_kernel(x_ref, o_ref):
    o_ref[...] = x_ref[...].astype(o_ref.dtype)


def cast_bf16(w, tr=512):
    nl, r, c = w.shape
    spec = pl.BlockSpec((None, tr, c), lambda l, i: (l, i, 0))
    return pl.pallas_call(
        _cast_kernel,
        grid=(nl, r // tr),
        in_specs=[spec],
        out_specs=spec,
        out_shape=jax.ShapeDtypeStruct(w.shape, BF16),
        compiler_params=_params("parallel", "parallel"),
        name="cast_bf16",
    )(w)


def _w_in_pieces():
    a0, b0 = 0, MLA_COLS
    c0 = b0 + MLSTM_COLS
    d0 = c0 + DIFF_COLS
    qk_w = 2 * MLSTM_HEADS * MLSTM_QK
    v_w = MLSTM_HEADS * MLSTM_V
    dq_w = DIFF_HEADS * 2 * DIFF_HEAD_DIM
    sq_w = SWA_HEADS * SWA_HEAD_DIM
    hd = SWA_HEAD_DIM
    kpe = a0 + MLA_Q_LORA + MLA_KV_LORA
    half = MLA_ROPE // 2
    f = [(ZF_CQ, a0, MLA_Q_LORA, 1.0), (ZF_MQK, b0, qk_w, 1.0), (ZF_MO, b0 + qk_w + v_w, v_w, 1.0),
         (ZF_CKV, a0 + MLA_Q_LORA, MLA_KV_LORA, 1.0), (ZF_KPE, kpe, MLA_ROPE, 1.0),
         (ZF_KPE + MLA_ROPE, kpe + half, half, -1.0), (ZF_KPE + MLA_ROPE + half, kpe, half, 1.0),
         (ZF_GATES, b0 + qk_w + 2 * v_w, 4 * MLSTM_HEADS, 1.0)]
    b = [(ZB_MV, b0 + qk_w, v_w, 1.0), (ZB_DQ, c0, dq_w, DIFF_Q_SCALE), (ZB_DK, c0 + dq_w, 2 * dq_w, 1.0),
         (ZB_SQ, d0, sq_w, SWA_HEAD_DIM ** -0.5)]
    for g in range(SWA_KV_HEADS):
        for rep in range(2):
            b.append((ZB_SK + (2 * g + rep) * hd, d0 + sq_w + g * hd, hd, 1.0))
            b.append((ZB_SV + (2 * g + rep) * hd, d0 + sq_w + SWA_KV_HEADS * hd + g * hd, hd, 1.0))
    return [(0,) + p for p in f] + [(1,) + p for p in b]


def _w_in_kernel(w_ref, wf_ref, wb_ref):
    pad0 = ZF_GATES + 4 * MLSTM_HEADS
    wf_ref[:, pad0:] = jnp.zeros((wf_ref.shape[0], ZF_COLS - pad0), wf_ref.dtype)
    for grp, dst, src, width, scale in _w_in_pieces():
        piece = w_ref[:, src:src + width]
        if scale != 1.0:
            piece = piece * scale
        (wb_ref if grp else wf_ref)[:, dst:dst + width] = piece.astype(BF16)


def w_in_layout(w_in, l, tr=256):
    r, c = w_in.shape[1:]
    return pl.pallas_call(
        _w_in_kernel,
        grid=(r // tr,),
        in_specs=[pl.BlockSpec((None, tr, c), lambda i: (l, i, 0))],
        out_specs=[pl.BlockSpec((tr, ZF_COLS), lambda i: (i, 0)), pl.BlockSpec((tr, ZB_COLS), lambda i: (i, 0))],
        out_shape=[jax.ShapeDtypeStruct((r, ZF_COLS), BF16), jax.ShapeDtypeStruct((r, ZB_COLS), BF16)],
        compiler_params=_params("parallel"),
        name="w_in_layout",
    )(w_in)


RESID_STEPS = 8


def _resid_kernel(a_ref, w_ref, x_ref, gp_ref, gn_ref, xo_ref, xn_ref, acc0_ref, acc1_ref, *, coef, n_tiles):
    i = pl.program_id(0)
    k = pl.program_id(1)
    rs = x_ref.shape[0]

    def epilogue(prev_ref):
        rows = pl.ds(pl.multiple_of(k * rs, rs), rs)
        xnew = x_ref[...] + coef * _rms(prev_ref[rows, :], gp_ref[...])
        xo_ref[...] = xnew
        xn_ref[...] = _rms(xnew, gn_ref[...]).astype(xn_ref.dtype)
        prev_ref[rows, :] = jnp.zeros((rs, prev_ref.shape[1]), F32)

    @pl.when((i == 0) & (k == 0))
    def _():
        acc0_ref[...] = jnp.zeros_like(acc0_ref)
        acc1_ref[...] = jnp.zeros_like(acc1_ref)

    for parity, (cur_ref, prev_ref) in enumerate(((acc0_ref, acc1_ref), (acc1_ref, acc0_ref))):
        @pl.when((i % 2 == parity) & (i < n_tiles))
        def _(cur_ref=cur_ref, prev_ref=prev_ref):
            epilogue(prev_ref)
            cur_ref[...] += jnp.dot(a_ref[...], w_ref[...], preferred_element_type=F32)

        @pl.when((i % 2 == parity) & (i == n_tiles))
        def _(prev_ref=prev_ref):
            epilogue(prev_ref)


def resid_update(a, w, l, x, g_post, g_next, coef, tm=1024):
    m, kk = a.shape
    d = w.shape[2]
    n_tiles = m // tm
    tk = kk // RESID_STEPS
    rs = tm // RESID_STEPS
    row_map = lambda i, k: (jnp.where(i == 0, 0, (i - 1) * RESID_STEPS + k), 0)
    return pl.pallas_call(
        functools.partial(_resid_kernel, coef=coef, n_tiles=n_tiles),
        grid=(n_tiles + 1, RESID_STEPS),
        in_specs=[pl.BlockSpec((tm, tk), lambda i, k: (jnp.minimum(i, n_tiles - 1), k)),
                  pl.BlockSpec((None, tk, d), lambda i, k: (l, k, 0)),
                  pl.BlockSpec((rs, d), row_map),
                  pl.BlockSpec((1, d), lambda i, k: (0, 0)),
                  pl.BlockSpec((1, d), lambda i, k: (0, 0))],
        out_specs=[pl.BlockSpec((rs, d), row_map), pl.BlockSpec((rs, d), row_map)],
        out_shape=[jax.ShapeDtypeStruct((m, d), F32), jax.ShapeDtypeStruct((m, d), BF16)],
        scratch_shapes=[pltpu.VMEM((tm, d), F32), pltpu.VMEM((tm, d), F32)],
        compiler_params=_params("arbitrary", "arbitrary"),
        name="resid_update",
    )(a, w, x, g_post.reshape(1, d), g_next.reshape(1, d))


def _group_norm_kernel(a_ref, b_ref, c_ref, d_ref, g_ref, o_ref):
    w = a_ref.shape[1]
    for n, r in enumerate((a_ref, b_ref, c_ref, d_ref)):
        o_ref[:, n * w:(n + 1) * w] = _rms(r[...], g_ref[:, n * w:(n + 1) * w]).astype(o_ref.dtype)


def group_rmsnorm(ya, yb, yc, yd, g, tm=512):
    t, w = ya.shape
    spec = pl.BlockSpec((tm, w), lambda i: (i, 0))
    return pl.pallas_call(
        _group_norm_kernel,
        grid=(t // tm,),
        in_specs=[spec, spec, spec, spec, pl.BlockSpec((1, 4 * w), lambda i: (0, 0))],
        out_specs=pl.BlockSpec((tm, 4 * w), lambda i: (i, 0)),
        out_shape=jax.ShapeDtypeStruct((t, 4 * w), BF16),
        compiler_params=_params("parallel"),
        name="group_norm",
    )(ya, yb, yc, yd, g.reshape(1, 4 * w))


def _mla_proj_kernel(cq_ref, ckv_ref, kpe_ref, qg_ref, kvg_ref, wq_ref, wkv_ref, cos_ref, sin_ref,
                     q_ref, k_ref, v_ref):
    cos = cos_ref[...]
    sin = sin_ref[...]
    kp = kpe_ref[...]
    k_rope = (kp[:, :MLA_ROPE] * cos + kp[:, MLA_ROPE:] * sin).astype(k_ref.dtype)
    qq = jnp.dot(_rms(cq_ref[...], qg_ref[...]).astype(BF16), wq_ref[...],
                 preferred_element_type=F32) * MLA_Q_SCALE
    kv = jnp.dot(_rms(ckv_ref[...], kvg_ref[...]).astype(BF16), wkv_ref[...], preferred_element_type=F32)
    hw = 2 * LANES
    for h in range(MLA_HEADS):
        qh = qq[:, h * hw:(h + 1) * hw]
        q_ref[h, :, :MLA_NOPE] = qh[:, :MLA_NOPE].astype(q_ref.dtype)
        q_ref[h, :, MLA_NOPE:] = (qh[:, MLA_NOPE:MLA_NOPE + MLA_ROPE] * cos
                                  + qh[:, MLA_NOPE + MLA_ROPE:] * sin).astype(q_ref.dtype)
        k_ref[h, :, :MLA_NOPE] = kv[:, h * hw:h * hw + MLA_NOPE].astype(k_ref.dtype)
        k_ref[h, :, MLA_NOPE:] = k_rope
        v_ref[h] = kv[:, h * hw + MLA_NOPE:(h + 1) * hw].astype(v_ref.dtype)


def mla_proj(zf, q_gain, kv_gain, wq, wkv, cos, sin, b, s, tm=512):
    ns = s // tm
    nh = MLA_HEADS
    dqk = MLA_NOPE + MLA_ROPE
    out_map = lambda i: (i // ns, 0, i % ns, 0)
    return pl.pallas_call(
        _mla_proj_kernel,
        grid=(b * ns,),
        in_specs=[pl.BlockSpec((tm, MLA_Q_LORA), lambda i: (i, ZF_CQ // MLA_Q_LORA)),
                  pl.BlockSpec((tm, MLA_KV_LORA), lambda i: (i, ZF_CKV // MLA_KV_LORA)),
                  pl.BlockSpec((tm, LANES), lambda i: (i, ZF_KPE // LANES)),
                  pl.BlockSpec((1, MLA_Q_LORA), lambda i: (0, 0)),
                  pl.BlockSpec((1, MLA_KV_LORA), lambda i: (0, 0)),
                  pl.BlockSpec(wq.shape, lambda i: (0, 0)),
                  pl.BlockSpec(wkv.shape, lambda i: (0, 0)),
                  pl.BlockSpec((tm, MLA_ROPE), lambda i: (i % ns, 0)),
                  pl.BlockSpec((tm, MLA_ROPE), lambda i: (i % ns, 0))],
        out_specs=[pl.BlockSpec((None, nh, tm, dqk), out_map),
                   pl.BlockSpec((None, nh, tm, dqk), out_map),
                   pl.BlockSpec((None, nh, tm, MLA_V), out_map)],
        out_shape=[jax.ShapeDtypeStruct((b, nh, s, dqk), BF16),
                   jax.ShapeDtypeStruct((b, nh, s, dqk), BF16),
                   jax.ShapeDtypeStruct((b, nh, s, MLA_V), BF16)],
        compiler_params=_params("parallel"),
        name="mla_proj",
    )(zf, zf, zf, q_gain.reshape(1, -1), kv_gain.reshape(1, -1), wq, wkv, cos, sin)


def _key_chunk(s):
    return min(256, s)


def _fill_v_ones(vaug_ref, v_ref):
    dv = v_ref.shape[1]
    vaug_ref[:, :dv] = v_ref[...]
    vaug_ref[:, dv:] = jnp.ones((v_ref.shape[0], vaug_ref.shape[1] - dv), vaug_ref.dtype)


def _softmax_pv(lhs, k_ref, vaug_ref, dv, bias_fn=None):
    chunk = _key_chunk(k_ref.shape[0])
    m = acc = None
    for c in range(k_ref.shape[0] // chunk):
        ks = slice(c * chunk, (c + 1) * chunk)
        sc = lax.dot_general(lhs, k_ref[ks, :], _NT, preferred_element_type=F32)
        if bias_fn is not None:
            sc = sc + bias_fn(c)
        top = jnp.max(sc, axis=-1, keepdims=True)
        m_new = top if m is None else jnp.maximum(m, top)
        pv = jnp.dot(jnp.exp2(sc - m_new).astype(BF16), vaug_ref[ks, :], preferred_element_type=F32)
        acc = pv if acc is None else acc * jnp.exp2(m - m_new) + pv
        m = m_new
    return acc[:, :dv] / acc[:, dv:]


def _mla_attn_kernel(q_ref, k_ref, v_ref, o_ref, vaug_ref):
    @pl.when(pl.program_id(2) == 0)
    def _():
        _fill_v_ones(vaug_ref, v_ref)

    o_ref[...] = _softmax_pv(q_ref[...], k_ref, vaug_ref, MLA_V).astype(o_ref.dtype)


def mla_attn(q, k, v, tq=2048):
    b, h, s, dqk = q.shape
    tq = min(tq, s)
    nq = s // tq
    return pl.pallas_call(
        _mla_attn_kernel,
        grid=(b, h, nq),
        in_specs=[pl.BlockSpec((None, None, tq, dqk), lambda b_, h_, i: (b_, h_, i, 0)),
                  pl.BlockSpec((None, None, s, dqk), lambda b_, h_, i: (b_, h_, 0, 0)),
                  pl.BlockSpec((None, None, s, MLA_V), lambda b_, h_, i: (b_, h_, 0, 0))],
        out_specs=pl.BlockSpec((tq, MLA_V), lambda b_, h_, i: (b_ * nq + i, h_)),
        out_shape=jax.ShapeDtypeStruct((b * s, h * MLA_V), F32),
        scratch_shapes=[pltpu.VMEM((s, 2 * MLA_V), BF16)],
        compiler_params=_params("parallel", "parallel", "arbitrary"),
        name="mla_attn",
    )(q, k, v)


def _diff_attn_kernel(q_ref, k_ref, v_ref, bias_ref, lv_ref, sub_ref, o_ref, vaug_ref, *, lam_init, rows):
    i = pl.program_id(2)
    tq = q_ref.shape[0]
    s = k_ref.shape[0]
    d = DIFF_HEAD_DIM

    @pl.when(i == 0)
    def _():
        _fill_v_ones(vaug_ref, v_ref)

    lv = lv_ref[...]
    lam = (jnp.exp(jnp.sum(lv[0:1] * lv[1:2], axis=-1, keepdims=True))
           - jnp.exp(jnp.sum(lv[2:3] * lv[3:4], axis=-1, keepdims=True)) + lam_init)
    q = q_ref[...]
    lane = lax.broadcasted_iota(jnp.int32, q.shape, 1)
    zero = jnp.zeros_like(q)
    lhs = jnp.concatenate([jnp.where(lane < d, q, zero), jnp.where(lane >= d, q, zero)], axis=0)

    def bias(c):
        chunk = _key_chunk(s)
        groups = [bias_ref[:, pl.ds(pl.multiple_of(s - rows - (i * tq + g * rows), rows) + c * chunk, chunk)]
                  for g in range(tq // rows)]
        return jnp.concatenate(groups + groups, axis=0)

    o = _softmax_pv(lhs, k_ref, vaug_ref, 2 * d, bias)
    o = o[:tq] - lam * o[tq:]
    o_ref[...] = (_rms(o, sub_ref[...]) * (1.0 - lam_init)).astype(o_ref.dtype)


def _alibi_table(n_heads, s, rows):
    slopes = 2.0 ** (-8.0 * jnp.arange(1, n_heads + 1, dtype=F32) / n_heads)
    r = jnp.arange(rows, dtype=jnp.int32)[:, None]
    c = jnp.arange(2 * s - rows, dtype=jnp.int32)[None, :]
    dist = jnp.abs(r - c + (s - rows)).astype(F32)
    return (-LOG2E * slopes)[:, None, None] * dist[None]


def diff_attn(zb, lam_vecs, subln, lam_init, bias_tbl, b, s, tq=1024):
    tq = min(tq, s)
    nq = s // tq
    hd = 2 * DIFF_HEAD_DIM
    rows, width = bias_tbl.shape[1:]
    return pl.pallas_call(
        functools.partial(_diff_attn_kernel, lam_init=lam_init, rows=rows),
        grid=(b, DIFF_HEADS, nq),
        in_specs=[pl.BlockSpec((tq, hd), lambda b_, h, i: (b_ * nq + i, ZB_DQ // hd + h)),
                  pl.BlockSpec((s, hd), lambda b_, h, i: (b_, ZB_DK // hd + h)),
                  pl.BlockSpec((s, hd), lambda b_, h, i: (b_, ZB_DV // hd + h)),
                  pl.BlockSpec((None, rows, width), lambda b_, h, i: (h, 0, 0)),
                  pl.BlockSpec((4, DIFF_HEAD_DIM), lambda b_, h, i: (0, 0)),
                  pl.BlockSpec((1, hd), lambda b_, h, i: (0, 0))],
        out_specs=pl.BlockSpec((tq, hd), lambda b_, h, i: (b_ * nq + i, h)),
        out_shape=jax.ShapeDtypeStruct((b * s, DIFF_HEADS * hd), F32),
        scratch_shapes=[pltpu.VMEM((s, 2 * hd), BF16)],
        compiler_params=_params("parallel", "parallel", "arbitrary"),
        name="diff_attn",
    )(zb, zb, zb, bias_tbl, lam_vecs, subln.reshape(1, hd))


def _swa_kernel(sink_ref, q_ref, k_ref, v_ref, o_ref, *, seq):
    w = SWA_WINDOW
    d = SWA_HEAD_DIM
    qb = q_ref.shape[0]
    band = qb + 2 * w
    n = pl.program_id(1)
    start = pl.multiple_of(jnp.clip(n * qb - w, 0, seq - band), w)
    kband = k_ref[pl.ds(start, band), :]
    vband = v_ref[pl.ds(start, band), :]
    qpos = n * qb + lax.broadcasted_iota(jnp.int32, (2 * qb, band), 0) % qb
    kpos = start + lax.broadcasted_iota(jnp.int32, (2 * qb, band), 1)
    rel = jnp.abs(qpos - kpos)
    neg_rel = jnp.where(rel <= w, -rel.astype(F32), -jnp.inf)
    lane = lax.broadcasted_iota(jnp.int32, (qb, 2 * d), 1)
    row = lax.broadcasted_iota(jnp.int32, (2 * qb, 1), 0)
    for pair in range(SWA_HEADS // 2):
        g = (2 * pair) // SWA_REP
        qp = q_ref[:, pair * 2 * d:(pair + 1) * 2 * d]
        zero = jnp.zeros_like(qp)
        qs = jnp.concatenate([jnp.where(lane < d, qp, zero), jnp.where(lane >= d, qp, zero)], axis=0)
        sc = lax.dot_general(qs, kband[:, g * 2 * d:(g + 1) * 2 * d], _NT, preferred_element_type=F32)
        h0 = 2 * pair
        slope = jnp.where(row < qb, 2.0 ** (-8.0 * (h0 + 1) / SWA_HEADS), 2.0 ** (-8.0 * (h0 + 2) / SWA_HEADS))
        sink = jnp.where(row < qb, sink_ref[h0], sink_ref[h0 + 1])
        sc = sc + slope * neg_rel
        m = jnp.maximum(jnp.max(sc, axis=-1, keepdims=True), sink)
        e = jnp.exp(sc - m)
        l = jnp.sum(e, axis=-1, keepdims=True) + jnp.exp(sink - m)
        p = (e * (1.0 / l)).astype(BF16)
        o = jnp.dot(p, vband[:, g * 2 * d:(g + 1) * 2 * d], preferred_element_type=F32)
        o_ref[:, pair * 2 * d:(pair + 1) * 2 * d] = jnp.where(lane < d, o[:qb], o[qb:]).astype(o_ref.dtype)


def swa_attn(zb, sink, b, s, qb=256):
    nb = s // qb
    kvw = SWA_KV_HEADS * 2 * SWA_HEAD_DIM
    qw = SWA_HEADS * SWA_HEAD_DIM
    return pl.pallas_call(
        functools.partial(_swa_kernel, seq=s),
        grid=(b, nb),
        in_specs=[pl.BlockSpec(memory_space=pltpu.SMEM),
                  pl.BlockSpec((qb, qw), lambda b_, n: (b_ * nb + n, ZB_SQ // qw)),
                  pl.BlockSpec((s, kvw), lambda b_, n: (b_, ZB_SK // kvw)),
                  pl.BlockSpec((s, kvw), lambda b_, n: (b_, ZB_SV // kvw))],
        out_specs=pl.BlockSpec((qb, qw), lambda b_, n: (b_ * nb + n, 0)),
        out_shape=jax.ShapeDtypeStruct((b * s, qw), F32),
        compiler_params=_params("parallel", "arbitrary"),
        name="swa_attn",
    )(sink, zb, zb, zb)


def _conv_kernel(x_ref, w_ref, b_ref, o_ref, *, kscale):
    x = x_ref[...]
    s = x.shape[0]
    row = lax.broadcasted_iota(jnp.int32, x.shape, 0)
    pad = MLSTM_CONV // 2
    y = x * w_ref[pad:pad + 1, :] + b_ref[...]
    for j in range(MLSTM_CONV):
        off = j - pad
        if off == 0:
            continue
        shifted = pltpu.roll(x, (-off) % s, 0)
        ok = (row >= -off) if off < 0 else (row < s - off)
        y = y + jnp.where(ok, shifted, 0.0) * w_ref[j:j + 1, :]
    y = y * _sigmoid(y)
    y = y * jnp.where(pl.program_id(1) >= pl.num_programs(1) // 2, kscale, 1.0)
    o_ref[...] = y.astype(o_ref.dtype)


def mlstm_conv(zf, conv_w, conv_b, b, s, tc=256):
    c = conv_w.shape[1]
    nc = c // tc
    return pl.pallas_call(
        functools.partial(_conv_kernel, kscale=MLSTM_QK ** -0.5),
        grid=(b, nc),
        in_specs=[pl.BlockSpec((s, tc), lambda b_, j: (b_, ZF_MQK // tc + j)),
                  pl.BlockSpec((MLSTM_CONV, tc), lambda b_, j: (0, j)),
                  pl.BlockSpec((1, tc), lambda b_, j: (0, j))],
        out_specs=pl.BlockSpec((s, tc), lambda b_, j: (b_, j)),
        out_shape=jax.ShapeDtypeStruct((b * s, c), BF16),
        compiler_params=_params("parallel", "arbitrary"),
        name="mlstm_conv",
    )(zf, conv_w, conv_b.reshape(1, c))


def _scan(x, row, op, fill, reverse):
    s = x.shape[0]
    d = 1
    while d < s:
        if reverse:
            shifted = jnp.where(row < s - d, pltpu.roll(x, s - d, 0), fill)
        else:
            shifted = jnp.where(row >= d, pltpu.roll(x, d, 0), fill)
        x = op(x, shifted)
        d *= 2
    return x


def _gate_kernel(g_ref, gb_ref, col_ref, row_ref):
    nh = MLSTM_HEADS
    g = g_ref[...] + gb_ref[...]
    ls = jnp.minimum(g, 0.0) - jnp.log1p(jnp.exp(-jnp.abs(g)))
    row = lax.broadcasted_iota(jnp.int32, g.shape, 0)
    lane = lax.broadcasted_iota(jnp.int32, g.shape, 1)
    cum_f = pltpu.roll(_scan(ls, row, jnp.add, 0.0, False), LANES - nh, 1)
    cum_b = pltpu.roll(_scan(ls, row, jnp.add, 0.0, True), LANES - nh, 1)
    a_f = g - cum_f
    a_b = g - cum_b
    top_f = _scan(a_f, row, jnp.maximum, -jnp.inf, False)
    top_b = _scan(a_b, row, jnp.maximum, -jnp.inf, True)
    m_f = cum_f + top_f
    m_b = cum_b + top_b
    src_t = (jnp.where(lane < 2 * nh, a_f, a_b) * LOG2E).T
    sub = lax.broadcasted_iota(jnp.int32, row_ref.shape[1:], 0)
    for h in range(nh):
        hb = 2 * nh + h
        col_ref[h] = jnp.where(lane == 0, top_f[:, h:h + 1] * LOG2E,
                     jnp.where(lane == 1, m_f[:, h:h + 1],
                     jnp.where(lane == 2, top_b[:, hb:hb + 1] * LOG2E,
                     jnp.where(lane == 3, m_b[:, hb:hb + 1],
                     jnp.where(lane == 4, a_f[:, h:h + 1] * LOG2E, a_b[:, hb:hb + 1] * LOG2E)))))
        row_ref[h] = jnp.where(sub == 0, src_t[h:h + 1, :],
                     jnp.where(sub == 1, src_t[2 * nh + h:2 * nh + h + 1, :], 0.0))


def mlstm_gates(zf, gate_b, b, s):
    gb = jnp.zeros((1, LANES), F32).at[0, :4 * MLSTM_HEADS].set(gate_b.reshape(-1))
    nh = MLSTM_HEADS
    return pl.pallas_call(
        _gate_kernel,
        grid=(b,),
        in_specs=[pl.BlockSpec((s, LANES), lambda b_: (b_, ZF_GATES // LANES)),
                  pl.BlockSpec((1, LANES), lambda b_: (0, 0))],
        out_specs=[pl.BlockSpec((None, nh, s, LANES), lambda b_: (b_, 0, 0, 0)),
                   pl.BlockSpec((None, nh, 8, s), lambda b_: (b_, 0, 0, 0))],
        out_shape=[jax.ShapeDtypeStruct((b, nh, s, LANES), F32),
                   jax.ShapeDtypeStruct((b, nh, 8, s), F32)],
        compiler_params=_params("parallel"),
        name="mlstm_gates",
    )(zf, gb)


MLSTM_SCAN_CHUNK = 512


def _mlstm_scan_kernel(q_ref, k_ref, v_ref, col_ref, row_ref, og_ref, o_ref, hf_ref):
    s = q_ref.shape[0]
    ch = min(MLSTM_SCAN_CHUNK, s)
    nc = s // ch
    tpos = lax.broadcasted_iota(jnp.int32, (ch, ch), 0)
    spos = lax.broadcasted_iota(jnp.int32, (ch, ch), 1)

    def sweep(order, lane0, src_row, mask, edge_row, emit):
        state = zsum = edge = None
        for n, j in enumerate(order):
            rows = slice(j * ch, (j + 1) * ch)
            q, k, v = q_ref[rows, :], k_ref[rows, :], v_ref[rows, :]
            col = col_ref[rows, :]
            top = col[:, lane0:lane0 + 1]
            sc = lax.dot_general(q, k, _NT, preferred_element_type=F32)
            wgt = jnp.where(mask, jnp.exp2(row_ref[src_row:src_row + 1, rows] - top), 0.0) * sc
            num = jnp.dot(wgt.astype(BF16), v, preferred_element_type=F32)
            den = jnp.sum(wgt, axis=-1, keepdims=True)
            if state is not None:
                carry = jnp.exp2(edge - top)
                num = num + carry * jnp.dot(q, state.astype(BF16), preferred_element_type=F32)
                den = den + carry * jnp.sum(q.astype(F32) * zsum, axis=-1, keepdims=True)
            emit(rows, num / jnp.maximum(jnp.abs(den), jnp.exp(-col[:, lane0 + 1:lane0 + 2])))
            if n + 1 < nc:
                new_edge = col_ref[edge_row(j):edge_row(j) + 1, lane0:lane0 + 1]
                a_col = col[:, lane0 // 2 + 4:lane0 // 2 + 5]
                kw = k.astype(F32) * jnp.exp2(a_col - new_edge)
                upd = jnp.dot(kw.T.astype(BF16), v, preferred_element_type=F32)
                zupd = jnp.sum(kw.astype(BF16).astype(F32), axis=0, keepdims=True)
                if state is None:
                    state, zsum = upd, zupd
                else:
                    decay = jnp.exp2(edge - new_edge)
                    state, zsum = decay * state + upd, decay * zsum + zupd
                edge = new_edge

    def emit_forward(rows, h):
        hf_ref[rows, :] = h

    def emit_backward(rows, h):
        o_ref[rows, :] = (_sigmoid(og_ref[rows, :]) * (hf_ref[rows, :] + h)).astype(o_ref.dtype)

    sweep(range(nc), 0, 0, tpos >= spos, lambda j: (j + 1) * ch - 1, emit_forward)
    sweep(range(nc - 1, -1, -1), 2, 1, tpos <= spos, lambda j: j * ch, emit_backward)


def mlstm_attn(qk, zb, zf, col, row, b, s):
    nh, dk, dv = MLSTM_HEADS, MLSTM_QK, MLSTM_V
    return pl.pallas_call(
        _mlstm_scan_kernel,
        grid=(b, nh),
        in_specs=[pl.BlockSpec((s, dk), lambda b_, h: (b_, h)),
                  pl.BlockSpec((s, dk), lambda b_, h: (b_, nh + h)),
                  pl.BlockSpec((s, dv), lambda b_, h: (b_, ZB_MV // dv + h)),
                  pl.BlockSpec((None, None, s, LANES), lambda b_, h: (b_, h, 0, 0)),
                  pl.BlockSpec((None, None, 8, s), lambda b_, h: (b_, h, 0, 0)),
                  pl.BlockSpec((s, dv), lambda b_, h: (b_, ZF_MO // dv + h))],
        out_specs=pl.BlockSpec((s, dv), lambda b_, h: (b_, h)),
        out_shape=jax.ShapeDtypeStruct((b * s, nh * dv), F32),
        scratch_shapes=[pltpu.VMEM((s, dv), F32)],
        compiler_params=_params("parallel", "arbitrary"),
        name="mlstm_attn",
    )(qk, qk, zb, col, row, zf)


def _rot_half_cols(w):
    half = w.shape[-1] // 2
    return jnp.concatenate([-w[..., half:], w[..., :half]], axis=-1)


def _mla_weights(w_uq, w_ukv):
    wq = w_uq.reshape(MLA_Q_LORA, MLA_HEADS, MLA_NOPE + MLA_ROPE)
    pe = wq[..., MLA_NOPE:]
    wq = jnp.concatenate([wq[..., :MLA_NOPE], pe, _rot_half_cols(pe)], axis=-1)
    return wq.reshape(MLA_Q_LORA, -1).astype(BF16), w_ukv.astype(BF16)


def _rope_tables(s):
    half = MLA_ROPE // 2
    inv = ROPE_THETA ** (-jnp.arange(half, dtype=F32) / half)
    ang = jnp.arange(s).astype(F32)[:, None] * inv[None, :]
    cos, sin = jnp.cos(ang), jnp.sin(ang)
    return jnp.concatenate([cos, cos], axis=-1), jnp.concatenate([sin, sin], axis=-1)


def mixers(zf, zb, l, b, s, cos, sin, bias_tbl, mla_q_norm, mla_kv_norm, mla_w_uq, mla_w_ukv, mlstm_conv_w,
           mlstm_conv_b, mlstm_gate_b, diff_lambda, diff_subln, swa_sink):
    wq, wkv = _mla_weights(mla_w_uq, mla_w_ukv)
    q, k, v = mla_proj(zf, mla_q_norm, mla_kv_norm, wq, wkv, cos, sin, b, s)
    y_a = mla_attn(q, k, v)
    qk = mlstm_conv(zf, mlstm_conv_w, mlstm_conv_b, b, s)
    col, row = mlstm_gates(zf, mlstm_gate_b, b, s)
    y_b = mlstm_attn(qk, zb, zf, col, row, b, s)
    lam_init = 0.8 - 0.6 * math.exp(-0.3 * l)
    y_c = diff_attn(zb, diff_lambda, diff_subln, lam_init, bias_tbl, b, s)
    y_d = swa_attn(zb, swa_sink, b, s)
    return y_a, y_b, y_c, y_d


def kernel(x, norm_gains, w_in, mla_q_norm, mla_kv_norm, mla_w_uq, mla_w_ukv, mlstm_conv_w,
           mlstm_conv_b, mlstm_gate_b, diff_lambda, diff_subln, swa_sink, group_norm, w_out,
           ffn1_w_gu, ffn1_w_down, ffn2_w_gu, ffn2_w_down):
    b, s, d = x.shape
    depth = w_in.shape[0]
    cos, sin = _rope_tables(s)
    bias_tbl = _alibi_table(DIFF_HEADS, s, ATTN_ROWS)
    w_down1, w_down2, w_o = cast_bf16(ffn1_w_down), cast_bf16(ffn2_w_down), cast_bf16(w_out)
    x = x.reshape(b * s, d)
    xn = prenorm(x, norm_gains[0, 0])
    for l in range(depth):
        g = norm_gains[l]
        h = swiglu_up(xn, ffn1_w_gu, l)
        x, xn = resid_update(h, w_down1, l, x, g[1], g[2], 0.5)
        wf, wb = w_in_layout(w_in, l)
        zf = matmul(xn, wf, F32, 1024, 768, "proj_f32")
        zb = matmul(xn, wb, BF16, 1024, 512, "proj_bf16")
        ys = mixers(zf, zb, l, b, s, cos, sin, bias_tbl, mla_q_norm[l], mla_kv_norm[l], mla_w_uq[l], mla_w_ukv[l],
                    mlstm_conv_w[l], mlstm_conv_b[l], mlstm_gate_b[l], diff_lambda[l], diff_subln[l],
                    swa_sink[l])
        yn = group_rmsnorm(*ys, group_norm[l])
        x, xn = resid_update(yn, w_o, l, x, g[3], g[4], 1.0)
        h = swiglu_up(xn, ffn2_w_gu, l)
        g_next = norm_gains[l + 1, 0] if l + 1 < depth else g[5]
        x, xn = resid_update(h, w_down2, l, x, g[5], g_next, 0.5)
    return x.reshape(b, s, d)
```

```python
import functools
import math

import jax
import jax.numpy as jnp
from jax import lax
from jax.experimental import pallas as pl
from jax.experimental.pallas import tpu as pltpu

F32 = jnp.float32
BF16 = jnp.bfloat16
NORM_EPS = 1e-6
ROPE_THETA = 10000.0
LOG2E = math.log2(math.e)
LANES = 128
ATTN_ROWS = 128
VMEM_LIMIT_BYTES = 60 * 1024 * 1024

D_MODEL = 4096
GROUP_WIDTH = D_MODEL // 4
D_FF = (3 * D_MODEL) // 2
MLA_HEADS = GROUP_WIDTH // 128
MLA_Q_LORA = D_MODEL // 4
MLA_KV_LORA = D_MODEL // 8
MLA_NOPE = 128
MLA_ROPE = 64
MLA_V = GROUP_WIDTH // MLA_HEADS
MLSTM_HEADS = 4
MLSTM_V = GROUP_WIDTH // MLSTM_HEADS
MLSTM_QK = MLSTM_V // 2
MLSTM_CONV = 5
DIFF_HEADS = 8
DIFF_HEAD_DIM = GROUP_WIDTH // (2 * DIFF_HEADS)
SWA_HEADS = 16
SWA_KV_HEADS = 2
SWA_HEAD_DIM = GROUP_WIDTH // SWA_HEADS
SWA_WINDOW = 128
SWA_REP = SWA_HEADS // SWA_KV_HEADS

MLA_Q_SCALE = (MLA_NOPE + MLA_ROPE) ** -0.5 * LOG2E
DIFF_Q_SCALE = DIFF_HEAD_DIM ** -0.5 * LOG2E

MLA_COLS = MLA_Q_LORA + MLA_KV_LORA + MLA_ROPE
MLSTM_COLS = 2 * MLSTM_HEADS * MLSTM_QK + 2 * MLSTM_HEADS * MLSTM_V + 4 * MLSTM_HEADS
DIFF_COLS = 3 * DIFF_HEADS * 2 * DIFF_HEAD_DIM
SWA_COLS = (SWA_HEADS + 2 * SWA_KV_HEADS) * SWA_HEAD_DIM

ZF_CQ = 0
ZF_MQK = 1024
ZF_MO = 2048
ZF_CKV = 3072
ZF_KPE = 3584
ZF_GATES = 3712
ZF_COLS = 3840
ZB_MV = 0
ZB_DQ = 1024
ZB_DK = 2048
ZB_DV = 3072
ZB_SQ = 4096
ZB_SK = 5120
ZB_SV = 5376
ZB_COLS = 5632


def _params(*sem):
    return pltpu.CompilerParams(dimension_semantics=sem, vmem_limit_bytes=VMEM_LIMIT_BYTES)


def _rms(x, g):
    return x * lax.rsqrt(jnp.mean(x * x, axis=-1, keepdims=True) + NORM_EPS) * g


def _sigmoid(x):
    return 1.0 / (1.0 + jnp.exp(-x))


_NT = (((1,), (1,)), ((), ()))


def _prenorm_kernel(x_ref, g_ref, o_ref):
    o_ref[...] = _rms(x_ref[...], g_ref[...]).astype(o_ref.dtype)


def prenorm(x, g, tm=512):
    t, d = x.shape
    return pl.pallas_call(
        _prenorm_kernel,
        grid=(t // tm,),
        in_specs=[pl.BlockSpec((tm, d), lambda i: (i, 0)), pl.BlockSpec((1, d), lambda i: (0, 0))],
        out_specs=pl.BlockSpec((tm, d), lambda i: (i, 0)),
        out_shape=jax.ShapeDtypeStruct((t, d), BF16),
        compiler_params=_params("parallel"),
        name="prenorm",
    )(x, g.reshape(1, d))


def _matmul_kernel(a_ref, w_ref, o_ref):
    o_ref[...] = lax.dot_general(a_ref[...], w_ref[...], _NT, preferred_element_type=F32).astype(o_ref.dtype)


def matmul_nt(a, w_t, out_dtype, tm, tn, name):
    m, k = a.shape
    n = w_t.shape[0]
    return pl.pallas_call(
        _matmul_kernel,
        grid=(m // tm, n // tn),
        in_specs=[pl.BlockSpec((tm, k), lambda i, j: (i, 0)), pl.BlockSpec((tn, k), lambda i, j: (j, 0))],
        out_specs=pl.BlockSpec((tm, tn), lambda i, j: (i, j)),
        out_shape=jax.ShapeDtypeStruct((m, n), out_dtype),
        compiler_params=_params("parallel", "arbitrary"),
        name=name,
    )(a, w_t)


def _swiglu_kernel(a_ref, wg_ref, wu_ref, o_ref, wgb_ref, wub_ref):
    @pl.when(pl.program_id(1) == 0)
    def _():
        wgb_ref[...] = wg_ref[...].astype(BF16)
        wub_ref[...] = wu_ref[...].astype(BF16)

    a = a_ref[...]
    g = jnp.dot(a, wgb_ref[...], preferred_element_type=F32)
    u = jnp.dot(a, wub_ref[...], preferred_element_type=F32)
    o_ref[...] = (g * _sigmoid(g) * u).astype(o_ref.dtype)


def swiglu_up(a, w_gu, l, tm=2048, tn=256):
    m, k = a.shape
    f = w_gu.shape[2] // 2
    nj = f // tn
    return pl.pallas_call(
        _swiglu_kernel,
        grid=(nj, m // tm),
        in_specs=[pl.BlockSpec((tm, k), lambda j, i: (i, 0)),
                  pl.BlockSpec((None, k, tn), lambda j, i: (l, 0, j)),
                  pl.BlockSpec((None, k, tn), lambda j, i: (l, 0, j + nj))],
        out_specs=pl.BlockSpec((tm, tn), lambda j, i: (i, j)),
        out_shape=jax.ShapeDtypeStruct((m, f), BF16),
        scratch_shapes=[pltpu.VMEM((k, tn), BF16), pltpu.VMEM((k, tn), BF16)],
        compiler_params=_params("parallel", "arbitrary"),
        name="swiglu_up",
    )(a, w_gu, w_gu)


def _cast_kernel(x_ref, o_ref):
    o_ref[...] = x_ref[...].astype(o_ref.dtype)


def cast_bf16(w, tr=512):
    nl, r, c = w.shape
    spec = pl.BlockSpec((None, tr, c), lambda l, i: (l, i, 0))
    return pl.pallas_call(
        _cast_kernel,
        grid=(nl, r // tr),
        in_specs=[spec],
        out_specs=spec,
        out_shape=jax.ShapeDtypeStruct(w.shape, BF16),
        compiler_params=_params("parallel", "parallel"),
        name="cast_bf16",
    )(w)


def _w_in_pieces():
    a0, b0 = 0, MLA_COLS
    c0 = b0 + MLSTM_COLS
    d0 = c0 + DIFF_COLS
    qk_w = 2 * MLSTM_HEADS * MLSTM_QK
    v_w = MLSTM_HEADS * MLSTM_V
    dq_w = DIFF_HEADS * 2 * DIFF_HEAD_DIM
    sq_w = SWA_HEADS * SWA_HEAD_DIM
    hd = SWA_HEAD_DIM
    kpe = a0 + MLA_Q_LORA + MLA_KV_LORA
    half = MLA_ROPE // 2
    f = [(ZF_CQ, a0, MLA_Q_LORA, 1.0), (ZF_MQK, b0, qk_w, 1.0), (ZF_MO, b0 + qk_w + v_w, v_w, 1.0),
         (ZF_CKV, a0 + MLA_Q_LORA, MLA_KV_LORA, 1.0), (ZF_KPE, kpe, MLA_ROPE, 1.0),
         (ZF_KPE + MLA_ROPE, kpe + half, half, -1.0), (ZF_KPE + MLA_ROPE + half, kpe, half, 1.0),
         (ZF_GATES, b0 + qk_w + 2 * v_w, 4 * MLSTM_HEADS, 1.0)]
    b = [(ZB_MV, b0 + qk_w, v_w, 1.0), (ZB_DQ, c0, dq_w, DIFF_Q_SCALE), (ZB_DK, c0 + dq_w, 2 * dq_w, 1.0),
         (ZB_SQ, d0, sq_w, SWA_HEAD_DIM ** -0.5)]
    for g in range(SWA_KV_HEADS):
        for rep in range(2):
            b.append((ZB_SK + (2 * g + rep) * hd, d0 + sq_w + g * hd, hd, 1.0))
            b.append((ZB_SV + (2 * g + rep) * hd, d0 + sq_w + SWA_KV_HEADS * hd + g * hd, hd, 1.0))
    return [(0,) + p for p in f] + [(1,) + p for p in b]


def _w_in_kernel(w_ref, wf_ref, wb_ref):
    pad0 = ZF_GATES + 4 * MLSTM_HEADS
    wf_ref[pad0:, :] = jnp.zeros((ZF_COLS - pad0, wf_ref.shape[1]), wf_ref.dtype)
    for grp, dst, src, width, scale in _w_in_pieces():
        piece = w_ref[src:src + width, :]
        if scale != 1.0:
            piece = piece * scale
        (wb_ref if grp else wf_ref)[dst:dst + width, :] = piece.astype(BF16)


def w_in_layout(w_in_t, l, tk=256):
    c, k = w_in_t.shape[1:]
    return pl.pallas_call(
        _w_in_kernel,
        grid=(k // tk,),
        in_specs=[pl.BlockSpec((None, c, tk), lambda i: (l, 0, i))],
        out_specs=[pl.BlockSpec((ZF_COLS, tk), lambda i: (0, i)), pl.BlockSpec((ZB_COLS, tk), lambda i: (0, i))],
        out_shape=[jax.ShapeDtypeStruct((ZF_COLS, k), BF16), jax.ShapeDtypeStruct((ZB_COLS, k), BF16)],
        compiler_params=_params("parallel"),
        name="w_in_layout",
    )(w_in_t)


RESID_STEPS = 8


def _resid_kernel(a_ref, w_ref, x_ref, gp_ref, gn_ref, xo_ref, xn_ref, acc0_ref, acc1_ref, *, coef, n_tiles):
    i = pl.program_id(0)
    k = pl.program_id(1)
    rs = x_ref.shape[0]

    def epilogue(prev_ref):
        rows = pl.ds(pl.multiple_of(k * rs, rs), rs)
        xnew = x_ref[...] + coef * _rms(prev_ref[rows, :], gp_ref[...])
        xo_ref[...] = xnew
        xn_ref[...] = _rms(xnew, gn_ref[...]).astype(xn_ref.dtype)
        prev_ref[rows, :] = jnp.zeros((rs, prev_ref.shape[1]), F32)

    @pl.when((i == 0) & (k == 0))
    def _():
        acc0_ref[...] = jnp.zeros_like(acc0_ref)
        acc1_ref[...] = jnp.zeros_like(acc1_ref)

    for parity, (cur_ref, prev_ref) in enumerate(((acc0_ref, acc1_ref), (acc1_ref, acc0_ref))):
        @pl.when((i % 2 == parity) & (i < n_tiles))
        def _(cur_ref=cur_ref, prev_ref=prev_ref):
            epilogue(prev_ref)
            cur_ref[...] += jnp.dot(a_ref[...], w_ref[...], preferred_element_type=F32)

        @pl.when((i % 2 == parity) & (i == n_tiles))
        def _(prev_ref=prev_ref):
            epilogue(prev_ref)


def resid_update(a, w, l, x, g_post, g_next, coef, tm=1024):
    m, kk = a.shape
    d = w.shape[2]
    n_tiles = m // tm
    tk = kk // RESID_STEPS
    rs = tm // RESID_STEPS
    row_map = lambda i, k: (jnp.where(i == 0, 0, (i - 1) * RESID_STEPS + k), 0)
    return pl.pallas_call(
        functools.partial(_resid_kernel, coef=coef, n_tiles=n_tiles),
        grid=(n_tiles + 1, RESID_STEPS),
        in_specs=[pl.BlockSpec((tm, tk), lambda i, k: (jnp.minimum(i, n_tiles - 1), k)),
                  pl.BlockSpec((None, tk, d), lambda i, k: (l, k, 0)),
                  pl.BlockSpec((rs, d), row_map),
                  pl.BlockSpec((1, d), lambda i, k: (0, 0)),
                  pl.BlockSpec((1, d), lambda i, k: (0, 0))],
        out_specs=[pl.BlockSpec((rs, d), row_map), pl.BlockSpec((rs, d), row_map)],
        out_shape=[jax.ShapeDtypeStruct((m, d), F32), jax.ShapeDtypeStruct((m, d), BF16)],
        scratch_shapes=[pltpu.VMEM((tm, d), F32), pltpu.VMEM((tm, d), F32)],
        compiler_params=_params("arbitrary", "arbitrary"),
        name="resid_update",
    )(a, w, x, g_post.reshape(1, d), g_next.reshape(1, d))


def _group_norm_kernel(a_ref, b_ref, c_ref, d_ref, g_ref, o_ref):
    w = a_ref.shape[1]
    for n, r in enumerate((a_ref, b_ref, c_ref, d_ref)):
        o_ref[:, n * w:(n + 1) * w] = _rms(r[...], g_ref[:, n * w:(n + 1) * w]).astype(o_ref.dtype)


def group_rmsnorm(ya, yb, yc, yd, g, tm=512):
    t, w = ya.shape
    spec = pl.BlockSpec((tm, w), lambda i: (i, 0))
    return pl.pallas_call(
        _group_norm_kernel,
        grid=(t // tm,),
        in_specs=[spec, spec, spec, spec, pl.BlockSpec((1, 4 * w), lambda i: (0, 0))],
        out_specs=pl.BlockSpec((tm, 4 * w), lambda i: (i, 0)),
        out_shape=jax.ShapeDtypeStruct((t, 4 * w), BF16),
        compiler_params=_params("parallel"),
        name="group_norm",
    )(ya, yb, yc, yd, g.reshape(1, 4 * w))


def _mla_proj_kernel(cq_ref, ckv_ref, kpe_ref, qg_ref, kvg_ref, wq_ref, wkv_ref, cos_ref, sin_ref,
                     q_ref, k_ref, v_ref):
    cos = cos_ref[...]
    sin = sin_ref[...]
    kp = kpe_ref[...]
    k_rope = (kp[:, :MLA_ROPE] * cos + kp[:, MLA_ROPE:] * sin).astype(k_ref.dtype)
    qq = jnp.dot(_rms(cq_ref[...], qg_ref[...]).astype(BF16), wq_ref[...],
                 preferred_element_type=F32) * MLA_Q_SCALE
    kv = jnp.dot(_rms(ckv_ref[...], kvg_ref[...]).astype(BF16), wkv_ref[...], preferred_element_type=F32)
    hw = 2 * LANES
    for h in range(MLA_HEADS):
        qh = qq[:, h * hw:(h + 1) * hw]
        q_ref[h, :, :MLA_NOPE] = qh[:, :MLA_NOPE].astype(q_ref.dtype)
        q_ref[h, :, MLA_NOPE:] = (qh[:, MLA_NOPE:MLA_NOPE + MLA_ROPE] * cos
                                  + qh[:, MLA_NOPE + MLA_ROPE:] * sin).astype(q_ref.dtype)
        k_ref[h, :, :MLA_NOPE] = kv[:, h * hw:h * hw + MLA_NOPE].astype(k_ref.dtype)
        k_ref[h, :, MLA_NOPE:] = k_rope
        v_ref[h] = kv[:, h * hw + MLA_NOPE:(h + 1) * hw].astype(v_ref.dtype)


def mla_proj(zf, q_gain, kv_gain, wq, wkv, cos, sin, b, s, tm=512):
    ns = s // tm
    nh = MLA_HEADS
    dqk = MLA_NOPE + MLA_ROPE
    out_map = lambda i: (i // ns, 0, i % ns, 0)
    return pl.pallas_call(
        _mla_proj_kernel,
        grid=(b * ns,),
        in_specs=[pl.BlockSpec((tm, MLA_Q_LORA), lambda i: (i, ZF_CQ // MLA_Q_LORA)),
                  pl.BlockSpec((tm, MLA_KV_LORA), lambda i: (i, ZF_CKV // MLA_KV_LORA)),
                  pl.BlockSpec((tm, LANES), lambda i: (i, ZF_KPE // LANES)),
                  pl.BlockSpec((1, MLA_Q_LORA), lambda i: (0, 0)),
                  pl.BlockSpec((1, MLA_KV_LORA), lambda i: (0, 0)),
                  pl.BlockSpec(wq.shape, lambda i: (0, 0)),
                  pl.BlockSpec(wkv.shape, lambda i: (0, 0)),
                  pl.BlockSpec((tm, MLA_ROPE), lambda i: (i % ns, 0)),
                  pl.BlockSpec((tm, MLA_ROPE), lambda i: (i % ns, 0))],
        out_specs=[pl.BlockSpec((None, nh, tm, dqk), out_map),
                   pl.BlockSpec((None, nh, tm, dqk), out_map),
                   pl.BlockSpec((None, nh, tm, MLA_V), out_map)],
        out_shape=[jax.ShapeDtypeStruct((b, nh, s, dqk), BF16),
                   jax.ShapeDtypeStruct((b, nh, s, dqk), BF16),
                   jax.ShapeDtypeStruct((b, nh, s, MLA_V), BF16)],
        compiler_params=_params("parallel"),
        name="mla_proj",
    )(zf, zf, zf, q_gain.reshape(1, -1), kv_gain.reshape(1, -1), wq, wkv, cos, sin)


def _key_chunk(s):
    return min(256, s)


def _fill_v_ones(vaug_ref, v_ref):
    dv = v_ref.shape[1]
    vaug_ref[:, :dv] = v_ref[...]
    vaug_ref[:, dv:] = jnp.ones((v_ref.shape[0], vaug_ref.shape[1] - dv), vaug_ref.dtype)


def _softmax_pv(lhs, k_ref, vaug_ref, dv, bias_fn=None):
    chunk = _key_chunk(k_ref.shape[0])
    m = acc = None
    for c in range(k_ref.shape[0] // chunk):
        ks = slice(c * chunk, (c + 1) * chunk)
        sc = lax.dot_general(lhs, k_ref[ks, :], _NT, preferred_element_type=F32)
        if bias_fn is not None:
            sc = sc + bias_fn(c)
        top = jnp.max(sc, axis=-1, keepdims=True)
        m_new = top if m is None else jnp.maximum(m, top)
        pv = jnp.dot(jnp.exp2(sc - m_new).astype(BF16), vaug_ref[ks, :], preferred_element_type=F32)
        acc = pv if acc is None else acc * jnp.exp2(m - m_new) + pv
        m = m_new
    return acc[:, :dv] / acc[:, dv:]


def _mla_attn_kernel(q_ref, k_ref, v_ref, o_ref, vaug_ref):
    @pl.when(pl.program_id(2) == 0)
    def _():
        _fill_v_ones(vaug_ref, v_ref)

    o_ref[...] = _softmax_pv(q_ref[...], k_ref, vaug_ref, MLA_V).astype(o_ref.dtype)


def mla_attn(q, k, v, tq=2048):
    b, h, s, dqk = q.shape
    tq = min(tq, s)
    nq = s // tq
    return pl.pallas_call(
        _mla_attn_kernel,
        grid=(b, h, nq),
        in_specs=[pl.BlockSpec((None, None, tq, dqk), lambda b_, h_, i: (b_, h_, i, 0)),
                  pl.BlockSpec((None, None, s, dqk), lambda b_, h_, i: (b_, h_, 0, 0)),
                  pl.BlockSpec((None, None, s, MLA_V), lambda b_, h_, i: (b_, h_, 0, 0))],
        out_specs=pl.BlockSpec((tq, MLA_V), lambda b_, h_, i: (b_ * nq + i, h_)),
        out_shape=jax.ShapeDtypeStruct((b * s, h * MLA_V), F32),
        scratch_shapes=[pltpu.VMEM((s, 2 * MLA_V), BF16)],
        compiler_params=_params("parallel", "parallel", "arbitrary"),
        name="mla_attn",
    )(q, k, v)


def _diff_attn_kernel(q_ref, k_ref, v_ref, bias_ref, lv_ref, sub_ref, o_ref, vaug_ref, *, lam_init, rows):
    i = pl.program_id(2)
    tq = q_ref.shape[0]
    s = k_ref.shape[0]
    d = DIFF_HEAD_DIM

    @pl.when(i == 0)
    def _():
        _fill_v_ones(vaug_ref, v_ref)

    lv = lv_ref[...]
    lam = (jnp.exp(jnp.sum(lv[0:1] * lv[1:2], axis=-1, keepdims=True))
           - jnp.exp(jnp.sum(lv[2:3] * lv[3:4], axis=-1, keepdims=True)) + lam_init)
    q = q_ref[...]
    lane = lax.broadcasted_iota(jnp.int32, q.shape, 1)
    zero = jnp.zeros_like(q)
    lhs = jnp.concatenate([jnp.where(lane < d, q, zero), jnp.where(lane >= d, q, zero)], axis=0)

    def bias(c):
        chunk = _key_chunk(s)
        groups = [bias_ref[:, pl.ds(pl.multiple_of(s - rows - (i * tq + g * rows), rows) + c * chunk, chunk)]
                  for g in range(tq // rows)]
        return jnp.concatenate(groups + groups, axis=0)

    o = _softmax_pv(lhs, k_ref, vaug_ref, 2 * d, bias)
    o = o[:tq] - lam * o[tq:]
    o_ref[...] = (_rms(o, sub_ref[...]) * (1.0 - lam_init)).astype(o_ref.dtype)


def _alibi_table(n_heads, s, rows):
    slopes = 2.0 ** (-8.0 * jnp.arange(1, n_heads + 1, dtype=F32) / n_heads)
    r = jnp.arange(rows, dtype=jnp.int32)[:, None]
    c = jnp.arange(2 * s - rows, dtype=jnp.int32)[None, :]
    dist = jnp.abs(r - c + (s - rows)).astype(F32)
    return (-LOG2E * slopes)[:, None, None] * dist[None]


def diff_attn(zb, lam_vecs, subln, lam_init, bias_tbl, b, s, tq=1024):
    tq = min(tq, s)
    nq = s // tq
    hd = 2 * DIFF_HEAD_DIM
    rows, width = bias_tbl.shape[1:]
    return pl.pallas_call(
        functools.partial(_diff_attn_kernel, lam_init=lam_init, rows=rows),
        grid=(b, DIFF_HEADS, nq),
        in_specs=[pl.BlockSpec((tq, hd), lambda b_, h, i: (b_ * nq + i, ZB_DQ // hd + h)),
                  pl.BlockSpec((s, hd), lambda b_, h, i: (b_, ZB_DK // hd + h)),
                  pl.BlockSpec((s, hd), lambda b_, h, i: (b_, ZB_DV // hd + h)),
                  pl.BlockSpec((None, rows, width), lambda b_, h, i: (h, 0, 0)),
                  pl.BlockSpec((4, DIFF_HEAD_DIM), lambda b_, h, i: (0, 0)),
                  pl.BlockSpec((1, hd), lambda b_, h, i: (0, 0))],
        out_specs=pl.BlockSpec((tq, hd), lambda b_, h, i: (b_ * nq + i, h)),
        out_shape=jax.ShapeDtypeStruct((b * s, DIFF_HEADS * hd), F32),
        scratch_shapes=[pltpu.VMEM((s, 2 * hd), BF16)],
        compiler_params=_params("parallel", "parallel", "arbitrary"),
        name="diff_attn",
    )(zb, zb, zb, bias_tbl, lam_vecs, subln.reshape(1, hd))


def _swa_kernel(sink_ref, q_ref, k_ref, v_ref, o_ref, *, seq):
    w = SWA_WINDOW
    d = SWA_HEAD_DIM
    qb = q_ref.shape[0]
    band = qb + 2 * w
    n = pl.program_id(1)
    start = pl.multiple_of(jnp.clip(n * qb - w, 0, seq - band), w)
    kband = k_ref[pl.ds(start, band), :]
    vband = v_ref[pl.ds(start, band), :]
    qpos = n * qb + lax.broadcasted_iota(jnp.int32, (2 * qb, band), 0) % qb
    kpos = start + lax.broadcasted_iota(jnp.int32, (2 * qb, band), 1)
    rel = jnp.abs(qpos - kpos)
    neg_rel = jnp.where(rel <= w, -rel.astype(F32), -jnp.inf)
    lane = lax.broadcasted_iota(jnp.int32, (qb, 2 * d), 1)
    row = lax.broadcasted_iota(jnp.int32, (2 * qb, 1), 0)
    for pair in range(SWA_HEADS // 2):
        g = (2 * pair) // SWA_REP
        qp = q_ref[:, pair * 2 * d:(pair + 1) * 2 * d]
        zero = jnp.zeros_like(qp)
        qs = jnp.concatenate([jnp.where(lane < d, qp, zero), jnp.where(lane >= d, qp, zero)], axis=0)
        sc = lax.dot_general(qs, kband[:, g * 2 * d:(g + 1) * 2 * d], _NT, preferred_element_type=F32)
        h0 = 2 * pair
        slope = jnp.where(row < qb, 2.0 ** (-8.0 * (h0 + 1) / SWA_HEADS), 2.0 ** (-8.0 * (h0 + 2) / SWA_HEADS))
        sink = jnp.where(row < qb, sink_ref[h0], sink_ref[h0 + 1])
        sc = sc + slope * neg_rel
        m = jnp.maximum(jnp.max(sc, axis=-1, keepdims=True), sink)
        e = jnp.exp(sc - m)
        l = jnp.sum(e, axis=-1, keepdims=True) + jnp.exp(sink - m)
        p = (e * (1.0 / l)).astype(BF16)
        o = jnp.dot(p, vband[:, g * 2 * d:(g + 1) * 2 * d], preferred_element_type=F32)
        o_ref[:, pair * 2 * d:(pair + 1) * 2 * d] = jnp.where(lane < d, o[:qb], o[qb:]).astype(o_ref.dtype)


def swa_attn(zb, sink, b, s, qb=256):
    nb = s // qb
    kvw = SWA_KV_HEADS * 2 * SWA_HEAD_DIM
    qw = SWA_HEADS * SWA_HEAD_DIM
    return pl.pallas_call(
        functools.partial(_swa_kernel, seq=s),
        grid=(b, nb),
        in_specs=[pl.BlockSpec(memory_space=pltpu.SMEM),
                  pl.BlockSpec((qb, qw), lambda b_, n: (b_ * nb + n, ZB_SQ // qw)),
                  pl.BlockSpec((s, kvw), lambda b_, n: (b_, ZB_SK // kvw)),
                  pl.BlockSpec((s, kvw), lambda b_, n: (b_, ZB_SV // kvw))],
        out_specs=pl.BlockSpec((qb, qw), lambda b_, n: (b_ * nb + n, 0)),
        out_shape=jax.ShapeDtypeStruct((b * s, qw), F32),
        compiler_params=_params("parallel", "arbitrary"),
        name="swa_attn",
    )(sink, zb, zb, zb)


def _conv_kernel(x_ref, w_ref, b_ref, o_ref, *, kscale):
    x = x_ref[...]
    s = x.shape[0]
    row = lax.broadcasted_iota(jnp.int32, x.shape, 0)
    pad = MLSTM_CONV // 2
    y = x * w_ref[pad:pad + 1, :] + b_ref[...]
    for j in range(MLSTM_CONV):
        off = j - pad
        if off == 0:
            continue
        shifted = pltpu.roll(x, (-off) % s, 0)
        ok = (row >= -off) if off < 0 else (row < s - off)
        y = y + jnp.where(ok, shifted, 0.0) * w_ref[j:j + 1, :]
    y = y * _sigmoid(y)
    y = y * jnp.where(pl.program_id(1) >= pl.num_programs(1) // 2, kscale, 1.0)
    o_ref[...] = y.astype(o_ref.dtype)


def mlstm_conv(zf, conv_w, conv_b, b, s, tc=256):
    c = conv_w.shape[1]
    nc = c // tc
    return pl.pallas_call(
        functools.partial(_conv_kernel, kscale=MLSTM_QK ** -0.5),
        grid=(b, nc),
        in_specs=[pl.BlockSpec((s, tc), lambda b_, j: (b_, ZF_MQK // tc + j)),
                  pl.BlockSpec((MLSTM_CONV, tc), lambda b_, j: (0, j)),
                  pl.BlockSpec((1, tc), lambda b_, j: (0, j))],
        out_specs=pl.BlockSpec((s, tc), lambda b_, j: (b_, j)),
        out_shape=jax.ShapeDtypeStruct((b * s, c), BF16),
        compiler_params=_params("parallel", "arbitrary"),
        name="mlstm_conv",
    )(zf, conv_w, conv_b.reshape(1, c))


def _scan(x, row, op, fill, reverse):
    s = x.shape[0]
    d = 1
    while d < s:
        if reverse:
            shifted = jnp.where(row < s - d, pltpu.roll(x, s - d, 0), fill)
        else:
            shifted = jnp.where(row >= d, pltpu.roll(x, d, 0), fill)
        x = op(x, shifted)
        d *= 2
    return x


def _gate_kernel(g_ref, gb_ref, col_ref, row_ref):
    nh = MLSTM_HEADS
    g = g_ref[...] + gb_ref[...]
    ls = jnp.minimum(g, 0.0) - jnp.log1p(jnp.exp(-jnp.abs(g)))
    row = lax.broadcasted_iota(jnp.int32, g.shape, 0)
    lane = lax.broadcasted_iota(jnp.int32, g.shape, 1)
    cum_f = pltpu.roll(_scan(ls, row, jnp.add, 0.0, False), LANES - nh, 1)
    cum_b = pltpu.roll(_scan(ls, row, jnp.add, 0.0, True), LANES - nh, 1)
    a_f = g - cum_f
    a_b = g - cum_b
    top_f = _scan(a_f, row, jnp.maximum, -jnp.inf, False)
    top_b = _scan(a_b, row, jnp.maximum, -jnp.inf, True)
    m_f = cum_f + top_f
    m_b = cum_b + top_b
    src_t = (jnp.where(lane < 2 * nh, a_f, a_b) * LOG2E).T
    sub = lax.broadcasted_iota(jnp.int32, row_ref.shape[1:], 0)
    for h in range(nh):
        hb = 2 * nh + h
        col_ref[h] = jnp.where(lane == 0, top_f[:, h:h + 1] * LOG2E,
                     jnp.where(lane == 1, m_f[:, h:h + 1],
                     jnp.where(lane == 2, top_b[:, hb:hb + 1] * LOG2E,
                     jnp.where(lane == 3, m_b[:, hb:hb + 1],
                     jnp.where(lane == 4, a_f[:, h:h + 1] * LOG2E, a_b[:, hb:hb + 1] * LOG2E)))))
        row_ref[h] = jnp.where(sub == 0, src_t[h:h + 1, :],
                     jnp.where(sub == 1, src_t[2 * nh + h:2 * nh + h + 1, :], 0.0))


def mlstm_gates(zf, gate_b, b, s):
    gb = jnp.zeros((1, LANES), F32).at[0, :4 * MLSTM_HEADS].set(gate_b.reshape(-1))
    nh = MLSTM_HEADS
    return pl.pallas_call(
        _gate_kernel,
        grid=(b,),
        in_specs=[pl.BlockSpec((s, LANES), lambda b_: (b_, ZF_GATES // LANES)),
                  pl.BlockSpec((1, LANES), lambda b_: (0, 0))],
        out_specs=[pl.BlockSpec((None, nh, s, LANES), lambda b_: (b_, 0, 0, 0)),
                   pl.BlockSpec((None, nh, 8, s), lambda b_: (b_, 0, 0, 0))],
        out_shape=[jax.ShapeDtypeStruct((b, nh, s, LANES), F32),
                   jax.ShapeDtypeStruct((b, nh, 8, s), F32)],
        compiler_params=_params("parallel"),
        name="mlstm_gates",
    )(zf, gb)


MLSTM_SCAN_CHUNK = 512


def _mlstm_scan_kernel(q_ref, k_ref, v_ref, col_ref, row_ref, og_ref, o_ref, hf_ref):
    s = q_ref.shape[0]
    ch = min(MLSTM_SCAN_CHUNK, s)
    nc = s // ch
    tpos = lax.broadcasted_iota(jnp.int32, (ch, ch), 0)
    spos = lax.broadcasted_iota(jnp.int32, (ch, ch), 1)

    def sweep(order, lane0, src_row, mask, edge_row, emit):
        state = zsum = edge = None
        for n, j in enumerate(order):
            rows = slice(j * ch, (j + 1) * ch)
            q, k, v = q_ref[rows, :], k_ref[rows, :], v_ref[rows, :]
            col = col_ref[rows, :]
            top = col[:, lane0:lane0 + 1]
            sc = lax.dot_general(q, k, _NT, preferred_element_type=F32)
            wgt = jnp.where(mask, jnp.exp2(row_ref[src_row:src_row + 1, rows] - top), 0.0) * sc
            num = jnp.dot(wgt.astype(BF16), v, preferred_element_type=F32)
            den = jnp.sum(wgt, axis=-1, keepdims=True)
            if state is not None:
                carry = jnp.exp2(edge - top)
                num = num + carry * jnp.dot(q, state.astype(BF16), preferred_element_type=F32)
                den = den + carry * jnp.sum(q.astype(F32) * zsum, axis=-1, keepdims=True)
            emit(rows, num / jnp.maximum(jnp.abs(den), jnp.exp(-col[:, lane0 + 1:lane0 + 2])))
            if n + 1 < nc:
                new_edge = col_ref[edge_row(j):edge_row(j) + 1, lane0:lane0 + 1]
                a_col = col[:, lane0 // 2 + 4:lane0 // 2 + 5]
                kw = k.astype(F32) * jnp.exp2(a_col - new_edge)
                upd = jnp.dot(kw.T.astype(BF16), v, preferred_element_type=F32)
                zupd = jnp.sum(kw.astype(BF16).astype(F32), axis=0, keepdims=True)
                if state is None:
                    state, zsum = upd, zupd
                else:
                    decay = jnp.exp2(edge - new_edge)
                    state, zsum = decay * state + upd, decay * zsum + zupd
                edge = new_edge

    def emit_forward(rows, h):
        hf_ref[rows, :] = h

    def emit_backward(rows, h):
        o_ref[rows, :] = (_sigmoid(og_ref[rows, :]) * (hf_ref[rows, :] + h)).astype(o_ref.dtype)

    sweep(range(nc), 0, 0, tpos >= spos, lambda j: (j + 1) * ch - 1, emit_forward)
    sweep(range(nc - 1, -1, -1), 2, 1, tpos <= spos, lambda j: j * ch, emit_backward)


def mlstm_attn(qk, zb, zf, col, row, b, s):
    nh, dk, dv = MLSTM_HEADS, MLSTM_QK, MLSTM_V
    return pl.pallas_call(
        _mlstm_scan_kernel,
        grid=(b, nh),
        in_specs=[pl.BlockSpec((s, dk), lambda b_, h: (b_, h)),
                  pl.BlockSpec((s, dk), lambda b_, h: (b_, nh + h)),
                  pl.BlockSpec((s, dv), lambda b_, h: (b_, ZB_MV // dv + h)),
                  pl.BlockSpec((None, None, s, LANES), lambda b_, h: (b_, h, 0, 0)),
                  pl.BlockSpec((None, None, 8, s), lambda b_, h: (b_, h, 0, 0)),
                  pl.BlockSpec((s, dv), lambda b_, h: (b_, ZF_MO // dv + h))],
        out_specs=pl.BlockSpec((s, dv), lambda b_, h: (b_, h)),
        out_shape=jax.ShapeDtypeStruct((b * s, nh * dv), F32),
        scratch_shapes=[pltpu.VMEM((s, dv), F32)],
        compiler_params=_params("parallel", "arbitrary"),
        name="mlstm_attn",
    )(qk, qk, zb, col, row, zf)


def _rot_half_cols(w):
    half = w.shape[-1] // 2
    return jnp.concatenate([-w[..., half:], w[..., :half]], axis=-1)


def _mla_weights(w_uq, w_ukv):
    wq = w_uq.reshape(MLA_Q_LORA, MLA_HEADS, MLA_NOPE + MLA_ROPE)
    pe = wq[..., MLA_NOPE:]
    wq = jnp.concatenate([wq[..., :MLA_NOPE], pe, _rot_half_cols(pe)], axis=-1)
    return wq.reshape(MLA_Q_LORA, -1).astype(BF16), w_ukv.astype(BF16)


def _rope_tables(s):
    half = MLA_ROPE // 2
    inv = ROPE_THETA ** (-jnp.arange(half, dtype=F32) / half)
    ang = jnp.arange(s).astype(F32)[:, None] * inv[None, :]
    cos, sin = jnp.cos(ang), jnp.sin(ang)
    return jnp.concatenate([cos, cos], axis=-1), jnp.concatenate([sin, sin], axis=-1)


def mixers(zf, zb, l, b, s, cos, sin, bias_tbl, mla_q_norm, mla_kv_norm, mla_w_uq, mla_w_ukv, mlstm_conv_w,
           mlstm_conv_b, mlstm_gate_b, diff_lambda, diff_subln, swa_sink):
    wq, wkv = _mla_weights(mla_w_uq, mla_w_ukv)
    q, k, v = mla_proj(zf, mla_q_norm, mla_kv_norm, wq, wkv, cos, sin, b, s)
    y_a = mla_attn(q, k, v)
    qk = mlstm_conv(zf, mlstm_conv_w, mlstm_conv_b, b, s)
    col, row = mlstm_gates(zf, mlstm_gate_b, b, s)
    y_b = mlstm_attn(qk, zb, zf, col, row, b, s)
    lam_init = 0.8 - 0.6 * math.exp(-0.3 * l)
    y_c = diff_attn(zb, diff_lambda, diff_subln, lam_init, bias_tbl, b, s)
    y_d = swa_attn(zb, swa_sink, b, s)
    return y_a, y_b, y_c, y_d


def kernel(x, norm_gains, w_in, mla_q_norm, mla_kv_norm, mla_w_uq, mla_w_ukv, mlstm_conv_w,
           mlstm_conv_b, mlstm_gate_b, diff_lambda, diff_subln, swa_sink, group_norm, w_out,
           ffn1_w_gu, ffn1_w_down, ffn2_w_gu, ffn2_w_down):
    b, s, d = x.shape
    depth = w_in.shape[0]
    cos, sin = _rope_tables(s)
    bias_tbl = _alibi_table(DIFF_HEADS, s, ATTN_ROWS)
    w_down1, w_down2, w_o = cast_bf16(ffn1_w_down), cast_bf16(ffn2_w_down), cast_bf16(w_out)
    w_in_t = jnp.swapaxes(w_in, 1, 2)
    x = x.reshape(b * s, d)
    xn = prenorm(x, norm_gains[0, 0])
    for l in range(depth):
        g = norm_gains[l]
        h = swiglu_up(xn, ffn1_w_gu, l)
        x, xn = resid_update(h, w_down1, l, x, g[1], g[2], 0.5)
        wf, wb = w_in_layout(w_in_t, l)
        zf = matmul_nt(xn, wf, F32, 1024, 768, "proj_f32")
        zb = matmul_nt(xn, wb, BF16, 1024, 512, "proj_bf16")
        ys = mixers(zf, zb, l, b, s, cos, sin, bias_tbl, mla_q_norm[l], mla_kv_norm[l], mla_w_uq[l], mla_w_ukv[l],
                    mlstm_conv_w[l], mlstm_conv_b[l], mlstm_gate_b[l], diff_lambda[l], diff_subln[l],
                    swa_sink[l])
        yn = group_rmsnorm(*ys, group_norm[l])
        x, xn = resid_update(yn, w_o, l, x, g[3], g[4], 1.0)
        h = swiglu_up(xn, ffn2_w_gu, l)
        g_next = norm_gains[l + 1, 0] if l + 1 < depth else g[5]
        x, xn = resid_update(h, w_down2, l, x, g[5], g_next, 0.5)
    return x.reshape(b, s, d)
```

```python
import functools
import math

import jax
import jax.numpy as jnp
from jax import lax
from jax.experimental import pallas as pl
from jax.experimental.pallas import tpu as pltpu

F32 = jnp.float32
BF16 = jnp.bfloat16
NORM_EPS = 1e-6
ROPE_THETA = 10000.0
LOG2E = math.log2(math.e)
LANES = 128
ATTN_ROWS = 128
VMEM_LIMIT_BYTES = 60 * 1024 * 1024

D_MODEL = 4096
GROUP_WIDTH = D_MODEL // 4
D_FF = (3 * D_MODEL) // 2
MLA_HEADS = GROUP_WIDTH // 128
MLA_Q_LORA = D_MODEL // 4
MLA_KV_LORA = D_MODEL // 8
MLA_NOPE = 128
MLA_ROPE = 64
MLA_V = GROUP_WIDTH // MLA_HEADS
MLSTM_HEADS = 4
MLSTM_V = GROUP_WIDTH // MLSTM_HEADS
MLSTM_QK = MLSTM_V // 2
MLSTM_CONV = 5
DIFF_HEADS = 8
DIFF_HEAD_DIM = GROUP_WIDTH // (2 * DIFF_HEADS)
SWA_HEADS = 16
SWA_KV_HEADS = 2
SWA_HEAD_DIM = GROUP_WIDTH // SWA_HEADS
SWA_WINDOW = 128
SWA_REP = SWA_HEADS // SWA_KV_HEADS

MLA_Q_SCALE = (MLA_NOPE + MLA_ROPE) ** -0.5 * LOG2E
DIFF_Q_SCALE = DIFF_HEAD_DIM ** -0.5 * LOG2E

MLA_COLS = MLA_Q_LORA + MLA_KV_LORA + MLA_ROPE
MLSTM_COLS = 2 * MLSTM_HEADS * MLSTM_QK + 2 * MLSTM_HEADS * MLSTM_V + 4 * MLSTM_HEADS
DIFF_COLS = 3 * DIFF_HEADS * 2 * DIFF_HEAD_DIM
SWA_COLS = (SWA_HEADS + 2 * SWA_KV_HEADS) * SWA_HEAD_DIM

ZF_CQ = 0
ZF_MQK = 1024
ZF_MO = 2048
ZF_CKV = 3072
ZF_KPE = 3584
ZF_GATES = 3712
ZF_COLS = 3840
ZB_MV = 0
ZB_DQ = 1024
ZB_DK = 2048
ZB_DV = 3072
ZB_SQ = 4096
ZB_SK = 5120
ZB_SV = 5376
ZB_COLS = 5632


def _params(*sem):
    return pltpu.CompilerParams(dimension_semantics=sem, vmem_limit_bytes=VMEM_LIMIT_BYTES)


def _rms(x, g):
    return x * lax.rsqrt(jnp.mean(x * x, axis=-1, keepdims=True) + NORM_EPS) * g


def _sigmoid(x):
    return 1.0 / (1.0 + jnp.exp(-x))


_NT = (((1,), (1,)), ((), ()))


def _prenorm_kernel(x_ref, g_ref, o_ref):
    o_ref[...] = _rms(x_ref[...], g_ref[...]).astype(o_ref.dtype)


def prenorm(x, g, tm=512):
    t, d = x.shape
    return pl.pallas_call(
        _prenorm_kernel,
        grid=(t // tm,),
        in_specs=[pl.BlockSpec((tm, d), lambda i: (i, 0)), pl.BlockSpec((1, d), lambda i: (0, 0))],
        out_specs=pl.BlockSpec((tm, d), lambda i: (i, 0)),
        out_shape=jax.ShapeDtypeStruct((t, d), BF16),
        compiler_params=_params("parallel"),
        name="prenorm",
    )(x, g.reshape(1, d))


def _matmul_kernel(a_ref, w_ref, o_ref):
    o_ref[...] = lax.dot_general(a_ref[...], w_ref[...], _NT, preferred_element_type=F32).astype(o_ref.dtype)


def matmul_nt(a, w_t, out_dtype, tm, tn, name):
    m, k = a.shape
    n = w_t.shape[0]
    return pl.pallas_call(
        _matmul_kernel,
        grid=(m // tm, n // tn),
        in_specs=[pl.BlockSpec((tm, k), lambda i, j: (i, 0)), pl.BlockSpec((tn, k), lambda i, j: (j, 0))],
        out_specs=pl.BlockSpec((tm, tn), lambda i, j: (i, j)),
        out_shape=jax.ShapeDtypeStruct((m, n), out_dtype),
        compiler_params=_params("parallel", "arbitrary"),
        name=name,
    )(a, w_t)


def _swiglu_kernel(a_ref, wg_ref, wu_ref, o_ref, wgb_ref, wub_ref):
    @pl.when(pl.program_id(1) == 0)
    def _():
        wgb_ref[...] = wg_ref[...].astype(BF16)
        wub_ref[...] = wu_ref[...].astype(BF16)

    half = a_ref.shape[0] // 2
    for r in range(2):
        rows = slice(r * half, (r + 1) * half)
        a = a_ref[rows, :]
        g = jnp.dot(a, wgb_ref[...], preferred_element_type=F32)
        u = jnp.dot(a, wub_ref[...], preferred_element_type=F32)
        o_ref[rows, :] = (g * _sigmoid(g) * u).astype(o_ref.dtype)


def swiglu_up(a, w_gu, l, tm=2048, tn=256):
    m, k = a.shape
    f = w_gu.shape[2] // 2
    nj = f // tn
    return pl.pallas_call(
        _swiglu_kernel,
        grid=(nj, m // tm),
        in_specs=[pl.BlockSpec((tm, k), lambda j, i: (i, 0)),
                  pl.BlockSpec((None, k, tn), lambda j, i: (l, 0, j)),
                  pl.BlockSpec((None, k, tn), lambda j, i: (l, 0, j + nj))],
        out_specs=pl.BlockSpec((tm, tn), lambda j, i: (i, j)),
        out_shape=jax.ShapeDtypeStruct((m, f), BF16),
        scratch_shapes=[pltpu.VMEM((k, tn), BF16), pltpu.VMEM((k, tn), BF16)],
        compiler_params=_params("parallel", "arbitrary"),
        name="swiglu_up",
    )(a, w_gu, w_gu)


def _cast_kernel(x_ref, o_ref):
    o_ref[...] = x_ref[...].astype(o_ref.dtype)


def cast_bf16(w, tr=512):
    nl, r, c = w.shape
    spec = pl.BlockSpec((None, tr, c), lambda l, i: (l, i, 0))
    return pl.pallas_call(
        _cast_kernel,
        grid=(nl, r // tr),
        in_specs=[spec],
        out_specs=spec,
        out_shape=jax.ShapeDtypeStruct(w.shape, BF16),
        compiler_params=_params("parallel", "parallel"),
        name="cast_bf16",
    )(w)


def _w_in_pieces():
    a0, b0 = 0, MLA_COLS
    c0 = b0 + MLSTM_COLS
    d0 = c0 + DIFF_COLS
    qk_w = 2 * MLSTM_HEADS * MLSTM_QK
    v_w = MLSTM_HEADS * MLSTM_V
    dq_w = DIFF_HEADS * 2 * DIFF_HEAD_DIM
    sq_w = SWA_HEADS * SWA_HEAD_DIM
    hd = SWA_HEAD_DIM
    kpe = a0 + MLA_Q_LORA + MLA_KV_LORA
    half = MLA_ROPE // 2
    f = [(ZF_CQ, a0, MLA_Q_LORA, 1.0), (ZF_MQK, b0, qk_w, 1.0), (ZF_MO, b0 + qk_w + v_w, v_w, 1.0),
         (ZF_CKV, a0 + MLA_Q_LORA, MLA_KV_LORA, 1.0), (ZF_KPE, kpe, MLA_ROPE, 1.0),
         (ZF_KPE + MLA_ROPE, kpe + half, half, -1.0), (ZF_KPE + MLA_ROPE + half, kpe, half, 1.0),
         (ZF_GATES, b0 + qk_w + 2 * v_w, 4 * MLSTM_HEADS, 1.0)]
    b = [(ZB_MV, b0 + qk_w, v_w, 1.0), (ZB_DQ, c0, dq_w, DIFF_Q_SCALE), (ZB_DK, c0 + dq_w, 2 * dq_w, 1.0),
         (ZB_SQ, d0, sq_w, SWA_HEAD_DIM ** -0.5)]
    for g in range(SWA_KV_HEADS):
        for rep in range(2):
            b.append((ZB_SK + (2 * g + rep) * hd, d0 + sq_w + g * hd, hd, 1.0))
            b.append((ZB_SV + (2 * g + rep) * hd, d0 + sq_w + SWA_KV_HEADS * hd + g * hd, hd, 1.0))
    return [(0,) + p for p in f] + [(1,) + p for p in b]


def _w_in_kernel(w_ref, wf_ref, wb_ref):
    pad0 = ZF_GATES + 4 * MLSTM_HEADS
    wf_ref[pad0:, :] = jnp.zeros((ZF_COLS - pad0, wf_ref.shape[1]), wf_ref.dtype)
    for grp, dst, src, width, scale in _w_in_pieces():
        piece = w_ref[src:src + width, :]
        if scale != 1.0:
            piece = piece * scale
        (wb_ref if grp else wf_ref)[dst:dst + width, :] = piece.astype(BF16)


def w_in_layout(w_in_t, l, tk=256):
    c, k = w_in_t.shape[1:]
    return pl.pallas_call(
        _w_in_kernel,
        grid=(k // tk,),
        in_specs=[pl.BlockSpec((None, c, tk), lambda i: (l, 0, i))],
        out_specs=[pl.BlockSpec((ZF_COLS, tk), lambda i: (0, i)), pl.BlockSpec((ZB_COLS, tk), lambda i: (0, i))],
        out_shape=[jax.ShapeDtypeStruct((ZF_COLS, k), BF16), jax.ShapeDtypeStruct((ZB_COLS, k), BF16)],
        compiler_params=_params("parallel"),
        name="w_in_layout",
    )(w_in_t)


RESID_STEPS = 8


def _resid_kernel(a_ref, w_ref, x_ref, gp_ref, gn_ref, xo_ref, xn_ref, acc0_ref, acc1_ref, *, coef, n_tiles):
    i = pl.program_id(0)
    k = pl.program_id(1)
    rs = x_ref.shape[0]

    def epilogue(prev_ref):
        rows = pl.ds(pl.multiple_of(k * rs, rs), rs)
        xnew = x_ref[...] + coef * _rms(prev_ref[rows, :], gp_ref[...])
        xo_ref[...] = xnew
        xn_ref[...] = _rms(xnew, gn_ref[...]).astype(xn_ref.dtype)
        prev_ref[rows, :] = jnp.zeros((rs, prev_ref.shape[1]), F32)

    @pl.when((i == 0) & (k == 0))
    def _():
        acc0_ref[...] = jnp.zeros_like(acc0_ref)
        acc1_ref[...] = jnp.zeros_like(acc1_ref)

    for parity, (cur_ref, prev_ref) in enumerate(((acc0_ref, acc1_ref), (acc1_ref, acc0_ref))):
        @pl.when((i % 2 == parity) & (i < n_tiles))
        def _(cur_ref=cur_ref, prev_ref=prev_ref):
            epilogue(prev_ref)
            cur_ref[...] += jnp.dot(a_ref[...], w_ref[...], preferred_element_type=F32)

        @pl.when((i % 2 == parity) & (i == n_tiles))
        def _(prev_ref=prev_ref):
            epilogue(prev_ref)


def resid_update(a, w, l, x, g_post, g_next, coef, tm=1024):
    m, kk = a.shape
    d = w.shape[2]
    n_tiles = m // tm
    tk = kk // RESID_STEPS
    rs = tm // RESID_STEPS
    row_map = lambda i, k: (jnp.where(i == 0, 0, (i - 1) * RESID_STEPS + k), 0)
    return pl.pallas_call(
        functools.partial(_resid_kernel, coef=coef, n_tiles=n_tiles),
        grid=(n_tiles + 1, RESID_STEPS),
        in_specs=[pl.BlockSpec((tm, tk), lambda i, k: (jnp.minimum(i, n_tiles - 1), k)),
                  pl.BlockSpec((None, tk, d), lambda i, k: (l, k, 0)),
                  pl.BlockSpec((rs, d), row_map),
                  pl.BlockSpec((1, d), lambda i, k: (0, 0)),
                  pl.BlockSpec((1, d), lambda i, k: (0, 0))],
        out_specs=[pl.BlockSpec((rs, d), row_map), pl.BlockSpec((rs, d), row_map)],
        out_shape=[jax.ShapeDtypeStruct((m, d), F32), jax.ShapeDtypeStruct((m, d), BF16)],
        scratch_shapes=[pltpu.VMEM((tm, d), F32), pltpu.VMEM((tm, d), F32)],
        compiler_params=_params("arbitrary", "arbitrary"),
        name="resid_update",
    )(a, w, x, g_post.reshape(1, d), g_next.reshape(1, d))


def _group_norm_kernel(a_ref, b_ref, c_ref, d_ref, g_ref, o_ref):
    w = a_ref.shape[1]
    for n, r in enumerate((a_ref, b_ref, c_ref, d_ref)):
        o_ref[:, n * w:(n + 1) * w] = _rms(r[...], g_ref[:, n * w:(n + 1) * w]).astype(o_ref.dtype)


def group_rmsnorm(ya, yb, yc, yd, g, tm=512):
    t, w = ya.shape
    spec = pl.BlockSpec((tm, w), lambda i: (i, 0))
    return pl.pallas_call(
        _group_norm_kernel,
        grid=(t // tm,),
        in_specs=[spec, spec, spec, spec, pl.BlockSpec((1, 4 * w), lambda i: (0, 0))],
        out_specs=pl.BlockSpec((tm, 4 * w), lambda i: (i, 0)),
        out_shape=jax.ShapeDtypeStruct((t, 4 * w), BF16),
        compiler_params=_params("parallel"),
        name="group_norm",
    )(ya, yb, yc, yd, g.reshape(1, 4 * w))


def _mla_proj_kernel(cq_ref, ckv_ref, kpe_ref, qg_ref, kvg_ref, wq_ref, wkv_ref, cos_ref, sin_ref,
                     q_ref, k_ref, v_ref):
    cos = cos_ref[...]
    sin = sin_ref[...]
    kp = kpe_ref[...]
    k_rope = (kp[:, :MLA_ROPE] * cos + kp[:, MLA_ROPE:] * sin).astype(k_ref.dtype)
    qq = jnp.dot(_rms(cq_ref[...], qg_ref[...]).astype(BF16), wq_ref[...],
                 preferred_element_type=F32) * MLA_Q_SCALE
    kv = jnp.dot(_rms(ckv_ref[...], kvg_ref[...]).astype(BF16), wkv_ref[...], preferred_element_type=F32)
    hw = 2 * LANES
    for h in range(MLA_HEADS):
        qh = qq[:, h * hw:(h + 1) * hw]
        q_ref[h, :, :MLA_NOPE] = qh[:, :MLA_NOPE].astype(q_ref.dtype)
        q_ref[h, :, MLA_NOPE:] = (qh[:, MLA_NOPE:MLA_NOPE + MLA_ROPE] * cos
                                  + qh[:, MLA_NOPE + MLA_ROPE:] * sin).astype(q_ref.dtype)
        k_ref[h, :, :MLA_NOPE] = kv[:, h * hw:h * hw + MLA_NOPE].astype(k_ref.dtype)
        k_ref[h, :, MLA_NOPE:] = k_rope
        v_ref[h] = kv[:, h * hw + MLA_NOPE:(h + 1) * hw].astype(v_ref.dtype)


def mla_proj(zf, q_gain, kv_gain, wq, wkv, cos, sin, b, s, tm=512):
    ns = s // tm
    nh = MLA_HEADS
    dqk = MLA_NOPE + MLA_ROPE
    out_map = lambda i: (i // ns, 0, i % ns, 0)
    return pl.pallas_call(
        _mla_proj_kernel,
        grid=(b * ns,),
        in_specs=[pl.BlockSpec((tm, MLA_Q_LORA), lambda i: (i, ZF_CQ // MLA_Q_LORA)),
                  pl.BlockSpec((tm, MLA_KV_LORA), lambda i: (i, ZF_CKV // MLA_KV_LORA)),
                  pl.BlockSpec((tm, LANES), lambda i: (i, ZF_KPE // LANES)),
                  pl.BlockSpec((1, MLA_Q_LORA), lambda i: (0, 0)),
                  pl.BlockSpec((1, MLA_KV_LORA), lambda i: (0, 0)),
                  pl.BlockSpec(wq.shape, lambda i: (0, 0)),
                  pl.BlockSpec(wkv.shape, lambda i: (0, 0)),
                  pl.BlockSpec((tm, MLA_ROPE), lambda i: (i % ns, 0)),
                  pl.BlockSpec((tm, MLA_ROPE), lambda i: (i % ns, 0))],
        out_specs=[pl.BlockSpec((None, nh, tm, dqk), out_map),
                   pl.BlockSpec((None, nh, tm, dqk), out_map),
                   pl.BlockSpec((None, nh, tm, MLA_V), out_map)],
        out_shape=[jax.ShapeDtypeStruct((b, nh, s, dqk), BF16),
                   jax.ShapeDtypeStruct((b, nh, s, dqk), BF16),
                   jax.ShapeDtypeStruct((b, nh, s, MLA_V), BF16)],
        compiler_params=_params("parallel"),
        name="mla_proj",
    )(zf, zf, zf, q_gain.reshape(1, -1), kv_gain.reshape(1, -1), wq, wkv, cos, sin)


def _key_chunk(s):
    return min(256, s)


def _fill_v_ones(vaug_ref, v_ref):
    dv = v_ref.shape[1]
    vaug_ref[:, :dv] = v_ref[...]
    vaug_ref[:, dv:] = jnp.ones((v_ref.shape[0], vaug_ref.shape[1] - dv), vaug_ref.dtype)


def _softmax_pv(lhs, k_ref, vaug_ref, dv, bias_fn=None):
    chunk = _key_chunk(k_ref.shape[0])
    m = acc = None
    for c in range(k_ref.shape[0] // chunk):
        ks = slice(c * chunk, (c + 1) * chunk)
        sc = lax.dot_general(lhs, k_ref[ks, :], _NT, preferred_element_type=F32)
        if bias_fn is not None:
            sc = sc + bias_fn(c)
        top = jnp.max(sc, axis=-1, keepdims=True)
        m_new = top if m is None else jnp.maximum(m, top)
        pv = jnp.dot(jnp.exp2(sc - m_new).astype(BF16), vaug_ref[ks, :], preferred_element_type=F32)
        acc = pv if acc is None else acc * jnp.exp2(m - m_new) + pv
        m = m_new
    return acc[:, :dv] / acc[:, dv:]


def _mla_attn_kernel(q_ref, k_ref, v_ref, o_ref, vaug_ref):
    @pl.when(pl.program_id(2) == 0)
    def _():
        _fill_v_ones(vaug_ref, v_ref)

    o_ref[...] = _softmax_pv(q_ref[...], k_ref, vaug_ref, MLA_V).astype(o_ref.dtype)


def mla_attn(q, k, v, tq=2048):
    b, h, s, dqk = q.shape
    tq = min(tq, s)
    nq = s // tq
    return pl.pallas_call(
        _mla_attn_kernel,
        grid=(b, h, nq),
        in_specs=[pl.BlockSpec((None, None, tq, dqk), lambda b_, h_, i: (b_, h_, i, 0)),
                  pl.BlockSpec((None, None, s, dqk), lambda b_, h_, i: (b_, h_, 0, 0)),
                  pl.BlockSpec((None, None, s, MLA_V), lambda b_, h_, i: (b_, h_, 0, 0))],
        out_specs=pl.BlockSpec((tq, MLA_V), lambda b_, h_, i: (b_ * nq + i, h_)),
        out_shape=jax.ShapeDtypeStruct((b * s, h * MLA_V), F32),
        scratch_shapes=[pltpu.VMEM((s, 2 * MLA_V), BF16)],
        compiler_params=_params("parallel", "parallel", "arbitrary"),
        name="mla_attn",
    )(q, k, v)


def _diff_attn_kernel(q_ref, k_ref, v_ref, bias_ref, lv_ref, sub_ref, o_ref, vaug_ref, *, lam_init, rows):
    i = pl.program_id(2)
    tq = q_ref.shape[0]
    s = k_ref.shape[0]
    d = DIFF_HEAD_DIM

    @pl.when(i == 0)
    def _():
        _fill_v_ones(vaug_ref, v_ref)

    lv = lv_ref[...]
    lam = (jnp.exp(jnp.sum(lv[0:1] * lv[1:2], axis=-1, keepdims=True))
           - jnp.exp(jnp.sum(lv[2:3] * lv[3:4], axis=-1, keepdims=True)) + lam_init)
    q = q_ref[...]
    lane = lax.broadcasted_iota(jnp.int32, q.shape, 1)
    zero = jnp.zeros_like(q)
    lhs = jnp.concatenate([jnp.where(lane < d, q, zero), jnp.where(lane >= d, q, zero)], axis=0)

    def bias(c):
        chunk = _key_chunk(s)
        groups = [bias_ref[:, pl.ds(pl.multiple_of(s - rows - (i * tq + g * rows), rows) + c * chunk, chunk)]
                  for g in range(tq // rows)]
        return jnp.concatenate(groups + groups, axis=0)

    o = _softmax_pv(lhs, k_ref, vaug_ref, 2 * d, bias)
    o = o[:tq] - lam * o[tq:]
    o_ref[...] = (_rms(o, sub_ref[...]) * (1.0 - lam_init)).astype(o_ref.dtype)


def _alibi_table(n_heads, s, rows):
    slopes = 2.0 ** (-8.0 * jnp.arange(1, n_heads + 1, dtype=F32) / n_heads)
    r = jnp.arange(rows, dtype=jnp.int32)[:, None]
    c = jnp.arange(2 * s - rows, dtype=jnp.int32)[None, :]
    dist = jnp.abs(r - c + (s - rows)).astype(F32)
    return (-LOG2E * slopes)[:, None, None] * dist[None]


def diff_attn(zb, lam_vecs, subln, lam_init, bias_tbl, b, s, tq=1024):
    tq = min(tq, s)
    nq = s // tq
    hd = 2 * DIFF_HEAD_DIM
    rows, width = bias_tbl.shape[1:]
    return pl.pallas_call(
        functools.partial(_diff_attn_kernel, lam_init=lam_init, rows=rows),
        grid=(b, DIFF_HEADS, nq),
        in_specs=[pl.BlockSpec((tq, hd), lambda b_, h, i: (b_ * nq + i, ZB_DQ // hd + h)),
                  pl.BlockSpec((s, hd), lambda b_, h, i: (b_, ZB_DK // hd + h)),
                  pl.BlockSpec((s, hd), lambda b_, h, i: (b_, ZB_DV // hd + h)),
                  pl.BlockSpec((None, rows, width), lambda b_, h, i: (h, 0, 0)),
                  pl.BlockSpec((4, DIFF_HEAD_DIM), lambda b_, h, i: (0, 0)),
                  pl.BlockSpec((1, hd), lambda b_, h, i: (0, 0))],
        out_specs=pl.BlockSpec((tq, hd), lambda b_, h, i: (b_ * nq + i, h)),
        out_shape=jax.ShapeDtypeStruct((b * s, DIFF_HEADS * hd), F32),
        scratch_shapes=[pltpu.VMEM((s, 2 * hd), BF16)],
        compiler_params=_params("parallel", "parallel", "arbitrary"),
        name="diff_attn",
    )(zb, zb, zb, bias_tbl, lam_vecs, subln.reshape(1, hd))


def _swa_kernel(sink_ref, q_ref, k_ref, v_ref, o_ref, *, seq):
    w = SWA_WINDOW
    d = SWA_HEAD_DIM
    qb = q_ref.shape[0]
    band = qb + 2 * w
    n = pl.program_id(1)
    start = pl.multiple_of(jnp.clip(n * qb - w, 0, seq - band), w)
    kband = k_ref[pl.ds(start, band), :]
    vband = v_ref[pl.ds(start, band), :]
    qpos = n * qb + lax.broadcasted_iota(jnp.int32, (2 * qb, band), 0) % qb
    kpos = start + lax.broadcasted_iota(jnp.int32, (2 * qb, band), 1)
    rel = jnp.abs(qpos - kpos)
    neg_rel = jnp.where(rel <= w, -rel.astype(F32), -jnp.inf)
    lane = lax.broadcasted_iota(jnp.int32, (qb, 2 * d), 1)
    row = lax.broadcasted_iota(jnp.int32, (2 * qb, 1), 0)
    for pair in range(SWA_HEADS // 2):
        g = (2 * pair) // SWA_REP
        qp = q_ref[:, pair * 2 * d:(pair + 1) * 2 * d]
        zero = jnp.zeros_like(qp)
        qs = jnp.concatenate([jnp.where(lane < d, qp, zero), jnp.where(lane >= d, qp, zero)], axis=0)
        sc = lax.dot_general(qs, kband[:, g * 2 * d:(g + 1) * 2 * d], _NT, preferred_element_type=F32)
        h0 = 2 * pair
        slope = jnp.where(row < qb, 2.0 ** (-8.0 * (h0 + 1) / SWA_HEADS), 2.0 ** (-8.0 * (h0 + 2) / SWA_HEADS))
        sink = jnp.where(row < qb, sink_ref[h0], sink_ref[h0 + 1])
        sc = sc + slope * neg_rel
        m = jnp.maximum(jnp.max(sc, axis=-1, keepdims=True), sink)
        e = jnp.exp(sc - m)
        l = jnp.sum(e, axis=-1, keepdims=True) + jnp.exp(sink - m)
        p = (e * (1.0 / l)).astype(BF16)
        o = jnp.dot(p, vband[:, g * 2 * d:(g + 1) * 2 * d], preferred_element_type=F32)
        o_ref[:, pair * 2 * d:(pair + 1) * 2 * d] = jnp.where(lane < d, o[:qb], o[qb:]).astype(o_ref.dtype)


def swa_attn(zb, sink, b, s, qb=256):
    qb = min(qb, s // 2)
    nb = s // qb
    kvw = SWA_KV_HEADS * 2 * SWA_HEAD_DIM
    qw = SWA_HEADS * SWA_HEAD_DIM
    return pl.pallas_call(
        functools.partial(_swa_kernel, seq=s),
        grid=(b, nb),
        in_specs=[pl.BlockSpec(memory_space=pltpu.SMEM),
                  pl.BlockSpec((qb, qw), lambda b_, n: (b_ * nb + n, ZB_SQ // qw)),
                  pl.BlockSpec((s, kvw), lambda b_, n: (b_, ZB_SK // kvw)),
                  pl.BlockSpec((s, kvw), lambda b_, n: (b_, ZB_SV // kvw))],
        out_specs=pl.BlockSpec((qb, qw), lambda b_, n: (b_ * nb + n, 0)),
        out_shape=jax.ShapeDtypeStruct((b * s, qw), F32),
        compiler_params=_params("parallel", "arbitrary"),
        name="swa_attn",
    )(sink, zb, zb, zb)


def _conv_kernel(x_ref, w_ref, b_ref, o_ref, *, kscale):
    x = x_ref[...]
    s = x.shape[0]
    row = lax.broadcasted_iota(jnp.int32, x.shape, 0)
    pad = MLSTM_CONV // 2
    y = x * w_ref[pad:pad + 1, :] + b_ref[...]
    for j in range(MLSTM_CONV):
        off = j - pad
        if off == 0:
            continue
        shifted = pltpu.roll(x, (-off) % s, 0)
        ok = (row >= -off) if off < 0 else (row < s - off)
        y = y + jnp.where(ok, shifted, 0.0) * w_ref[j:j + 1, :]
    y = y * _sigmoid(y)
    y = y * jnp.where(pl.program_id(1) >= pl.num_programs(1) // 2, kscale, 1.0)
    o_ref[...] = y.astype(o_ref.dtype)


def mlstm_conv(zf, conv_w, conv_b, b, s, tc=256):
    c = conv_w.shape[1]
    nc = c // tc
    return pl.pallas_call(
        functools.partial(_conv_kernel, kscale=MLSTM_QK ** -0.5),
        grid=(b, nc),
        in_specs=[pl.BlockSpec((s, tc), lambda b_, j: (b_, ZF_MQK // tc + j)),
                  pl.BlockSpec((MLSTM_CONV, tc), lambda b_, j: (0, j)),
                  pl.BlockSpec((1, tc), lambda b_, j: (0, j))],
        out_specs=pl.BlockSpec((s, tc), lambda b_, j: (b_, j)),
        out_shape=jax.ShapeDtypeStruct((b * s, c), BF16),
        compiler_params=_params("parallel", "arbitrary"),
        name="mlstm_conv",
    )(zf, conv_w, conv_b.reshape(1, c))


def _scan(x, row, op, fill, reverse):
    s = x.shape[0]
    d = 1
    while d < s:
        if reverse:
            shifted = jnp.where(row < s - d, pltpu.roll(x, s - d, 0), fill)
        else:
            shifted = jnp.where(row >= d, pltpu.roll(x, d, 0), fill)
        x = op(x, shifted)
        d *= 2
    return x


def _gate_kernel(g_ref, gb_ref, col_ref, row_ref):
    nh = MLSTM_HEADS
    g = g_ref[...] + gb_ref[...]
    ls = jnp.minimum(g, 0.0) - jnp.log1p(jnp.exp(-jnp.abs(g)))
    row = lax.broadcasted_iota(jnp.int32, g.shape, 0)
    lane = lax.broadcasted_iota(jnp.int32, g.shape, 1)
    cum_f = pltpu.roll(_scan(ls, row, jnp.add, 0.0, False), LANES - nh, 1)
    cum_b = pltpu.roll(_scan(ls, row, jnp.add, 0.0, True), LANES - nh, 1)
    a_f = g - cum_f
    a_b = g - cum_b
    top_f = _scan(a_f, row, jnp.maximum, -jnp.inf, False)
    top_b = _scan(a_b, row, jnp.maximum, -jnp.inf, True)
    m_f = cum_f + top_f
    m_b = cum_b + top_b
    src_t = (jnp.where(lane < 2 * nh, a_f, a_b) * LOG2E).T
    sub = lax.broadcasted_iota(jnp.int32, row_ref.shape[1:], 0)
    for h in range(nh):
        hb = 2 * nh + h
        col_ref[h] = jnp.where(lane == 0, top_f[:, h:h + 1] * LOG2E,
                     jnp.where(lane == 1, m_f[:, h:h + 1],
                     jnp.where(lane == 2, top_b[:, hb:hb + 1] * LOG2E,
                     jnp.where(lane == 3, m_b[:, hb:hb + 1],
                     jnp.where(lane == 4, a_f[:, h:h + 1] * LOG2E, a_b[:, hb:hb + 1] * LOG2E)))))
        row_ref[h] = jnp.where(sub == 0, src_t[h:h + 1, :],
                     jnp.where(sub == 1, src_t[2 * nh + h:2 * nh + h + 1, :], 0.0))


def mlstm_gates(zf, gate_b, b, s):
    gb = jnp.zeros((1, LANES), F32).at[0, :4 * MLSTM_HEADS].set(gate_b.reshape(-1))
    nh = MLSTM_HEADS
    return pl.pallas_call(
        _gate_kernel,
        grid=(b,),
        in_specs=[pl.BlockSpec((s, LANES), lambda b_: (b_, ZF_GATES // LANES)),
                  pl.BlockSpec((1, LANES), lambda b_: (0, 0))],
        out_specs=[pl.BlockSpec((None, nh, s, LANES), lambda b_: (b_, 0, 0, 0)),
                   pl.BlockSpec((None, nh, 8, s), lambda b_: (b_, 0, 0, 0))],
        out_shape=[jax.ShapeDtypeStruct((b, nh, s, LANES), F32),
                   jax.ShapeDtypeStruct((b, nh, 8, s), F32)],
        compiler_params=_params("parallel"),
        name="mlstm_gates",
    )(zf, gb)


MLSTM_SCAN_CHUNK = 512


def _mlstm_scan_kernel(q_ref, k_ref, v_ref, col_ref, row_ref, og_ref, o_ref, hf_ref):
    s = q_ref.shape[0]
    ch = min(MLSTM_SCAN_CHUNK, s)
    nc = s // ch
    tpos = lax.broadcasted_iota(jnp.int32, (ch, ch), 0)
    spos = lax.broadcasted_iota(jnp.int32, (ch, ch), 1)

    def sweep(order, lane0, src_row, mask, edge_row, emit):
        state = zsum = edge = None
        for n, j in enumerate(order):
            rows = slice(j * ch, (j + 1) * ch)
            q, k, v = q_ref[rows, :], k_ref[rows, :], v_ref[rows, :]
            col = col_ref[rows, :]
            top = col[:, lane0:lane0 + 1]
            sc = lax.dot_general(q, k, _NT, preferred_element_type=F32)
            wgt = jnp.where(mask, jnp.exp2(row_ref[src_row:src_row + 1, rows] - top), 0.0) * sc
            num = jnp.dot(wgt.astype(BF16), v, preferred_element_type=F32)
            den = jnp.sum(wgt, axis=-1, keepdims=True)
            if state is not None:
                carry = jnp.exp2(edge - top)
                num = num + carry * jnp.dot(q, state.astype(BF16), preferred_element_type=F32)
                den = den + carry * jnp.sum(q.astype(F32) * zsum, axis=-1, keepdims=True)
            emit(rows, num / jnp.maximum(jnp.abs(den), jnp.exp(-col[:, lane0 + 1:lane0 + 2])))
            if n + 1 < nc:
                new_edge = col_ref[edge_row(j):edge_row(j) + 1, lane0:lane0 + 1]
                a_col = col[:, lane0 // 2 + 4:lane0 // 2 + 5]
                kw = k.astype(F32) * jnp.exp2(a_col - new_edge)
                upd = jnp.dot(kw.T.astype(BF16), v, preferred_element_type=F32)
                zupd = jnp.sum(kw.astype(BF16).astype(F32), axis=0, keepdims=True)
                if state is None:
                    state, zsum = upd, zupd
                else:
                    decay = jnp.exp2(edge - new_edge)
                    state, zsum = decay * state + upd, decay * zsum + zupd
                edge = new_edge

    def emit_forward(rows, h):
        hf_ref[rows, :] = h

    def emit_backward(rows, h):
        o_ref[rows, :] = (_sigmoid(og_ref[rows, :]) * (hf_ref[rows, :] + h)).astype(o_ref.dtype)

    sweep(range(nc), 0, 0, tpos >= spos, lambda j: (j + 1) * ch - 1, emit_forward)
    sweep(range(nc - 1, -1, -1), 2, 1, tpos <= spos, lambda j: j * ch, emit_backward)


def mlstm_attn(qk, zb, zf, col, row, b, s):
    nh, dk, dv = MLSTM_HEADS, MLSTM_QK, MLSTM_V
    return pl.pallas_call(
        _mlstm_scan_kernel,
        grid=(b, nh),
        in_specs=[pl.BlockSpec((s, dk), lambda b_, h: (b_, h)),
                  pl.BlockSpec((s, dk), lambda b_, h: (b_, nh + h)),
                  pl.BlockSpec((s, dv), lambda b_, h: (b_, ZB_MV // dv + h)),
                  pl.BlockSpec((None, None, s, LANES), lambda b_, h: (b_, h, 0, 0)),
                  pl.BlockSpec((None, None, 8, s), lambda b_, h: (b_, h, 0, 0)),
                  pl.BlockSpec((s, dv), lambda b_, h: (b_, ZF_MO // dv + h))],
        out_specs=pl.BlockSpec((s, dv), lambda b_, h: (b_, h)),
        out_shape=jax.ShapeDtypeStruct((b * s, nh * dv), F32),
        scratch_shapes=[pltpu.VMEM((s, dv), F32)],
        compiler_params=_params("parallel", "arbitrary"),
        name="mlstm_attn",
    )(qk, qk, zb, col, row, zf)


def _rot_half_cols(w):
    half = w.shape[-1] // 2
    return jnp.concatenate([-w[..., half:], w[..., :half]], axis=-1)


def _mla_weights(w_uq, w_ukv):
    wq = w_uq.reshape(MLA_Q_LORA, MLA_HEADS, MLA_NOPE + MLA_ROPE)
    pe = wq[..., MLA_NOPE:]
    wq = jnp.concatenate([wq[..., :MLA_NOPE], pe, _rot_half_cols(pe)], axis=-1)
    return wq.reshape(MLA_Q_LORA, -1).astype(BF16), w_ukv.astype(BF16)


def _rope_tables(s):
    half = MLA_ROPE // 2
    inv = ROPE_THETA ** (-jnp.arange(half, dtype=F32) / half)
    ang = jnp.arange(s).astype(F32)[:, None] * inv[None, :]
    cos, sin = jnp.cos(ang), jnp.sin(ang)
    return jnp.concatenate([cos, cos], axis=-1), jnp.concatenate([sin, sin], axis=-1)


def mixers(zf, zb, l, b, s, cos, sin, bias_tbl, mla_q_norm, mla_kv_norm, mla_w_uq, mla_w_ukv, mlstm_conv_w,
           mlstm_conv_b, mlstm_gate_b, diff_lambda, diff_subln, swa_sink):
    wq, wkv = _mla_weights(mla_w_uq, mla_w_ukv)
    q, k, v = mla_proj(zf, mla_q_norm, mla_kv_norm, wq, wkv, cos, sin, b, s)
    y_a = mla_attn(q, k, v)
    qk = mlstm_conv(zf, mlstm_conv_w, mlstm_conv_b, b, s)
    col, row = mlstm_gates(zf, mlstm_gate_b, b, s)
    y_b = mlstm_attn(qk, zb, zf, col, row, b, s)
    lam_init = 0.8 - 0.6 * math.exp(-0.3 * l)
    y_c = diff_attn(zb, diff_lambda, diff_subln, lam_init, bias_tbl, b, s)
    y_d = swa_attn(zb, swa_sink, b, s)
    return y_a, y_b, y_c, y_d


def kernel(x, norm_gains, w_in, mla_q_norm, mla_kv_norm, mla_w_uq, mla_w_ukv, mlstm_conv_w,
           mlstm_conv_b, mlstm_gate_b, diff_lambda, diff_subln, swa_sink, group_norm, w_out,
           ffn1_w_gu, ffn1_w_down, ffn2_w_gu, ffn2_w_down):
    b, s, d = x.shape
    depth = w_in.shape[0]
    cos, sin = _rope_tables(s)
    bias_tbl = _alibi_table(DIFF_HEADS, s, ATTN_ROWS)
    w_down1, w_down2, w_o = cast_bf16(ffn1_w_down), cast_bf16(ffn2_w_down), cast_bf16(w_out)
    w_in_t = jnp.swapaxes(w_in, 1, 2)
    x = x.reshape(b * s, d)
    xn = prenorm(x, norm_gains[0, 0])
    for l in range(depth):
        g = norm_gains[l]
        h = swiglu_up(xn, ffn1_w_gu, l)
        x, xn = resid_update(h, w_down1, l, x, g[1], g[2], 0.5)
        wf, wb = w_in_layout(w_in_t, l)
        zf = matmul_nt(xn, wf, F32, 1024, 768, "proj_f32")
        zb = matmul_nt(xn, wb, BF16, 1024, 512, "proj_bf16")
        ys = mixers(zf, zb, l, b, s, cos, sin, bias_tbl, mla_q_norm[l], mla_kv_norm[l], mla_w_uq[l], mla_w_ukv[l],
                    mlstm_conv_w[l], mlstm_conv_b[l], mlstm_gate_b[l], diff_lambda[l], diff_subln[l],
                    swa_sink[l])
        yn = group_rmsnorm(*ys, group_norm[l])
        x, xn = resid_update(yn, w_o, l, x, g[3], g[4], 1.0)
        h = swiglu_up(xn, ffn2_w_gu, l)
        g_next = norm_gains[l + 1, 0] if l + 1 < depth else g[5]
        x, xn = resid_update(h, w_down2, l, x, g[5], g_next, 0.5)
    return x.reshape(b, s, d)
```

```python
import functools
import math

import jax
import jax.numpy as jnp
from jax import lax
from jax.experimental import pallas as pl
from jax.experimental.pallas import tpu as pltpu

F32 = jnp.float32
BF16 = jnp.bfloat16
NORM_EPS = 1e-6
ROPE_THETA = 10000.0
LOG2E = math.log2(math.e)
LANES = 128
ATTN_ROWS = 128
VMEM_LIMIT_BYTES = 60 * 1024 * 1024

D_MODEL = 4096
GROUP_WIDTH = D_MODEL // 4
D_FF = (3 * D_MODEL) // 2
MLA_HEADS = GROUP_WIDTH // 128
MLA_Q_LORA = D_MODEL // 4
MLA_KV_LORA = D_MODEL // 8
MLA_NOPE = 128
MLA_ROPE = 64
MLA_V = GROUP_WIDTH // MLA_HEADS
MLSTM_HEADS = 4
MLSTM_V = GROUP_WIDTH // MLSTM_HEADS
MLSTM_QK = MLSTM_V // 2
MLSTM_CONV = 5
DIFF_HEADS = 8
DIFF_HEAD_DIM = GROUP_WIDTH // (2 * DIFF_HEADS)
SWA_HEADS = 16
SWA_KV_HEADS = 2
SWA_HEAD_DIM = GROUP_WIDTH // SWA_HEADS
SWA_WINDOW = 128
SWA_REP = SWA_HEADS // SWA_KV_HEADS

MLA_Q_SCALE = (MLA_NOPE + MLA_ROPE) ** -0.5 * LOG2E
DIFF_Q_SCALE = DIFF_HEAD_DIM ** -0.5 * LOG2E

MLA_COLS = MLA_Q_LORA + MLA_KV_LORA + MLA_ROPE
MLSTM_COLS = 2 * MLSTM_HEADS * MLSTM_QK + 2 * MLSTM_HEADS * MLSTM_V + 4 * MLSTM_HEADS
DIFF_COLS = 3 * DIFF_HEADS * 2 * DIFF_HEAD_DIM
SWA_COLS = (SWA_HEADS + 2 * SWA_KV_HEADS) * SWA_HEAD_DIM

ZF_CQ = 0
ZF_MQK = 1024
ZF_MO = 2048
ZF_CKV = 3072
ZF_KPE = 3584
ZF_GATES = 3712
ZF_COLS = 3840
ZB_MV = 0
ZB_DQ = 1024
ZB_DK = 2048
ZB_DV = 3072
ZB_SQ = 4096
ZB_SK = 5120
ZB_SV = 5376
ZB_COLS = 5632


def _params(*sem):
    return pltpu.CompilerParams(dimension_semantics=sem, vmem_limit_bytes=VMEM_LIMIT_BYTES)


def _rms(x, g):
    return x * lax.rsqrt(jnp.mean(x * x, axis=-1, keepdims=True) + NORM_EPS) * g


def _sigmoid(x):
    return 1.0 / (1.0 + jnp.exp(-x))


_NT = (((1,), (1,)), ((), ()))


def _prenorm_kernel(x_ref, g_ref, o_ref):
    o_ref[...] = _rms(x_ref[...], g_ref[...]).astype(o_ref.dtype)


def prenorm(x, g, tm=512):
    t, d = x.shape
    return pl.pallas_call(
        _prenorm_kernel,
        grid=(t // tm,),
        in_specs=[pl.BlockSpec((tm, d), lambda i: (i, 0)), pl.BlockSpec((1, d), lambda i: (0, 0))],
        out_specs=pl.BlockSpec((tm, d), lambda i: (i, 0)),
        out_shape=jax.ShapeDtypeStruct((t, d), BF16),
        compiler_params=_params("parallel"),
        name="prenorm",
    )(x, g.reshape(1, d))


def _matmul_kernel(a_ref, w_ref, o_ref):
    o_ref[...] = lax.dot_general(a_ref[...], w_ref[...], _NT, preferred_element_type=F32).astype(o_ref.dtype)


def matmul_nt(a, w_t, out_dtype, tm, tn, name):
    m, k = a.shape
    n = w_t.shape[0]
    return pl.pallas_call(
        _matmul_kernel,
        grid=(m // tm, n // tn),
        in_specs=[pl.BlockSpec((tm, k), lambda i, j: (i, 0)), pl.BlockSpec((tn, k), lambda i, j: (j, 0))],
        out_specs=pl.BlockSpec((tm, tn), lambda i, j: (i, j)),
        out_shape=jax.ShapeDtypeStruct((m, n), out_dtype),
        compiler_params=_params("parallel", "arbitrary"),
        name=name,
    )(a, w_t)


def _swiglu_kernel(a_ref, wg_ref, wu_ref, o_ref, wgb_ref, wub_ref):
    @pl.when(pl.program_id(1) == 0)
    def _():
        wgb_ref[...] = wg_ref[...].astype(BF16)
        wub_ref[...] = wu_ref[...].astype(BF16)

    half = a_ref.shape[0] // 2
    for r in range(2):
        rows = slice(r * half, (r + 1) * half)
        a = a_ref[rows, :]
        g = jnp.dot(a, wgb_ref[...], preferred_element_type=F32)
        u = jnp.dot(a, wub_ref[...], preferred_element_type=F32)
        o_ref[rows, :] = (g * _sigmoid(g) * u).astype(o_ref.dtype)


def swiglu_up(a, w_gu, l, tm=2048, tn=256):
    m, k = a.shape
    f = w_gu.shape[2] // 2
    nj = f // tn
    return pl.pallas_call(
        _swiglu_kernel,
        grid=(nj, m // tm),
        in_specs=[pl.BlockSpec((tm, k), lambda j, i: (i, 0)),
                  pl.BlockSpec((None, k, tn), lambda j, i: (l, 0, j)),
                  pl.BlockSpec((None, k, tn), lambda j, i: (l, 0, j + nj))],
        out_specs=pl.BlockSpec((tm, tn), lambda j, i: (i, j)),
        out_shape=jax.ShapeDtypeStruct((m, f), BF16),
        scratch_shapes=[pltpu.VMEM((k, tn), BF16), pltpu.VMEM((k, tn), BF16)],
        compiler_params=_params("parallel", "arbitrary"),
        name="swiglu_up",
    )(a, w_gu, w_gu)


def _cast_kernel(x_ref, o_ref):
    o_ref[...] = x_ref[...].astype(o_ref.dtype)


def cast_bf16(w, tr=512):
    nl, r, c = w.shape
    spec = pl.BlockSpec((None, tr, c), lambda l, i: (l, i, 0))
    return pl.pallas_call(
        _cast_kernel,
        grid=(nl, r // tr),
        in_specs=[spec],
        out_specs=spec,
        out_shape=jax.ShapeDtypeStruct(w.shape, BF16),
        compiler_params=_params("parallel", "parallel"),
        name="cast_bf16",
    )(w)


def _w_in_pieces():
    a0, b0 = 0, MLA_COLS
    c0 = b0 + MLSTM_COLS
    d0 = c0 + DIFF_COLS
    qk_w = 2 * MLSTM_HEADS * MLSTM_QK
    v_w = MLSTM_HEADS * MLSTM_V
    dq_w = DIFF_HEADS * 2 * DIFF_HEAD_DIM
    sq_w = SWA_HEADS * SWA_HEAD_DIM
    hd = SWA_HEAD_DIM
    kpe = a0 + MLA_Q_LORA + MLA_KV_LORA
    half = MLA_ROPE // 2
    f = [(ZF_CQ, a0, MLA_Q_LORA, 1.0), (ZF_MQK, b0, qk_w, 1.0), (ZF_MO, b0 + qk_w + v_w, v_w, 1.0),
         (ZF_CKV, a0 + MLA_Q_LORA, MLA_KV_LORA, 1.0), (ZF_KPE, kpe, MLA_ROPE, 1.0),
         (ZF_KPE + MLA_ROPE, kpe + half, half, -1.0), (ZF_KPE + MLA_ROPE + half, kpe, half, 1.0),
         (ZF_GATES, b0 + qk_w + 2 * v_w, 4 * MLSTM_HEADS, 1.0)]
    b = [(ZB_MV, b0 + qk_w, v_w, 1.0), (ZB_DQ, c0, dq_w, DIFF_Q_SCALE), (ZB_DK, c0 + dq_w, 2 * dq_w, 1.0),
         (ZB_SQ, d0, sq_w, SWA_HEAD_DIM ** -0.5)]
    for g in range(SWA_KV_HEADS):
        for rep in range(2):
            b.append((ZB_SK + (2 * g + rep) * hd, d0 + sq_w + g * hd, hd, 1.0))
            b.append((ZB_SV + (2 * g + rep) * hd, d0 + sq_w + SWA_KV_HEADS * hd + g * hd, hd, 1.0))
    return [(0,) + p for p in f] + [(1,) + p for p in b]


def _w_in_kernel(w_ref, wf_ref, wb_ref):
    pad0 = ZF_GATES + 4 * MLSTM_HEADS
    wf_ref[pad0:, :] = jnp.zeros((ZF_COLS - pad0, wf_ref.shape[1]), wf_ref.dtype)
    for grp, dst, src, width, scale in _w_in_pieces():
        piece = w_ref[src:src + width, :]
        if scale != 1.0:
            piece = piece * scale
        (wb_ref if grp else wf_ref)[dst:dst + width, :] = piece.astype(BF16)


def w_in_layout(w_in_t, l, tk=256):
    c, k = w_in_t.shape[1:]
    return pl.pallas_call(
        _w_in_kernel,
        grid=(k // tk,),
        in_specs=[pl.BlockSpec((None, c, tk), lambda i: (l, 0, i))],
        out_specs=[pl.BlockSpec((ZF_COLS, tk), lambda i: (0, i)), pl.BlockSpec((ZB_COLS, tk), lambda i: (0, i))],
        out_shape=[jax.ShapeDtypeStruct((ZF_COLS, k), BF16), jax.ShapeDtypeStruct((ZB_COLS, k), BF16)],
        compiler_params=_params("parallel"),
        name="w_in_layout",
    )(w_in_t)


RESID_STEPS = 8


def _resid_kernel(a_ref, w_ref, x_ref, gp_ref, gn_ref, xo_ref, xn_ref, acc0_ref, acc1_ref, *, coef, n_tiles):
    i = pl.program_id(0)
    k = pl.program_id(1)
    rs = x_ref.shape[0]

    def epilogue(prev_ref):
        rows = pl.ds(pl.multiple_of(k * rs, rs), rs)
        xnew = x_ref[...] + coef * _rms(prev_ref[rows, :], gp_ref[...])
        xo_ref[...] = xnew
        xn_ref[...] = _rms(xnew, gn_ref[...]).astype(xn_ref.dtype)
        prev_ref[rows, :] = jnp.zeros((rs, prev_ref.shape[1]), F32)

    @pl.when((i == 0) & (k == 0))
    def _():
        acc0_ref[...] = jnp.zeros_like(acc0_ref)
        acc1_ref[...] = jnp.zeros_like(acc1_ref)

    for parity, (cur_ref, prev_ref) in enumerate(((acc0_ref, acc1_ref), (acc1_ref, acc0_ref))):
        @pl.when((i % 2 == parity) & (i < n_tiles))
        def _(cur_ref=cur_ref, prev_ref=prev_ref):
            epilogue(prev_ref)
            cur_ref[...] += jnp.dot(a_ref[...], w_ref[...], preferred_element_type=F32)

        @pl.when((i % 2 == parity) & (i == n_tiles))
        def _(prev_ref=prev_ref):
            epilogue(prev_ref)


def resid_update(a, w, l, x, g_post, g_next, coef, tm=1024):
    m, kk = a.shape
    d = w.shape[2]
    n_tiles = m // tm
    tk = kk // RESID_STEPS
    rs = tm // RESID_STEPS
    row_map = lambda i, k: (jnp.where(i == 0, 0, (i - 1) * RESID_STEPS + k), 0)
    return pl.pallas_call(
        functools.partial(_resid_kernel, coef=coef, n_tiles=n_tiles),
        grid=(n_tiles + 1, RESID_STEPS),
        in_specs=[pl.BlockSpec((tm, tk), lambda i, k: (jnp.minimum(i, n_tiles - 1), k)),
                  pl.BlockSpec((None, tk, d), lambda i, k: (l, k, 0)),
                  pl.BlockSpec((rs, d), row_map),
                  pl.BlockSpec((1, d), lambda i, k: (0, 0)),
                  pl.BlockSpec((1, d), lambda i, k: (0, 0))],
        out_specs=[pl.BlockSpec((rs, d), row_map), pl.BlockSpec((rs, d), row_map)],
        out_shape=[jax.ShapeDtypeStruct((m, d), F32), jax.ShapeDtypeStruct((m, d), BF16)],
        scratch_shapes=[pltpu.VMEM((tm, d), F32), pltpu.VMEM((tm, d), F32)],
        compiler_params=_params("arbitrary", "arbitrary"),
        name="resid_update",
    )(a, w, x, g_post.reshape(1, d), g_next.reshape(1, d))


def _group_norm_kernel(a_ref, b_ref, c_ref, d_ref, g_ref, o_ref):
    w = a_ref.shape[1]
    for n, r in enumerate((a_ref, b_ref, c_ref, d_ref)):
        o_ref[:, n * w:(n + 1) * w] = _rms(r[...], g_ref[:, n * w:(n + 1) * w]).astype(o_ref.dtype)


def group_rmsnorm(ya, yb, yc, yd, g, tm=512):
    t, w = ya.shape
    spec = pl.BlockSpec((tm, w), lambda i: (i, 0))
    return pl.pallas_call(
        _group_norm_kernel,
        grid=(t // tm,),
        in_specs=[spec, spec, spec, spec, pl.BlockSpec((1, 4 * w), lambda i: (0, 0))],
        out_specs=pl.BlockSpec((tm, 4 * w), lambda i: (i, 0)),
        out_shape=jax.ShapeDtypeStruct((t, 4 * w), BF16),
        compiler_params=_params("parallel"),
        name="group_norm",
    )(ya, yb, yc, yd, g.reshape(1, 4 * w))


def _mla_proj_kernel(cq_ref, ckv_ref, kpe_ref, qg_ref, kvg_ref, wq_ref, wkv_ref, cos_ref, sin_ref,
                     q_ref, k_ref, v_ref):
    cos = cos_ref[...]
    sin = sin_ref[...]
    kp = kpe_ref[...]
    k_rope = (kp[:, :MLA_ROPE] * cos + kp[:, MLA_ROPE:] * sin).astype(k_ref.dtype)
    qq = jnp.dot(_rms(cq_ref[...], qg_ref[...]).astype(BF16), wq_ref[...],
                 preferred_element_type=F32) * MLA_Q_SCALE
    kv = jnp.dot(_rms(ckv_ref[...], kvg_ref[...]).astype(BF16), wkv_ref[...], preferred_element_type=F32)
    hw = 2 * LANES
    for h in range(MLA_HEADS):
        qh = qq[:, h * hw:(h + 1) * hw]
        q_ref[h, :, :MLA_NOPE] = qh[:, :MLA_NOPE].astype(q_ref.dtype)
        q_ref[h, :, MLA_NOPE:] = (qh[:, MLA_NOPE:MLA_NOPE + MLA_ROPE] * cos
                                  + qh[:, MLA_NOPE + MLA_ROPE:] * sin).astype(q_ref.dtype)
        k_ref[h, :, :MLA_NOPE] = kv[:, h * hw:h * hw + MLA_NOPE].astype(k_ref.dtype)
        k_ref[h, :, MLA_NOPE:] = k_rope
        v_ref[h] = kv[:, h * hw + MLA_NOPE:(h + 1) * hw].astype(v_ref.dtype)


def mla_proj(zf, q_gain, kv_gain, wq, wkv, cos, sin, b, s, tm=512):
    ns = s // tm
    nh = MLA_HEADS
    dqk = MLA_NOPE + MLA_ROPE
    out_map = lambda i: (i // ns, 0, i % ns, 0)
    return pl.pallas_call(
        _mla_proj_kernel,
        grid=(b * ns,),
        in_specs=[pl.BlockSpec((tm, MLA_Q_LORA), lambda i: (i, ZF_CQ // MLA_Q_LORA)),
                  pl.BlockSpec((tm, MLA_KV_LORA), lambda i: (i, ZF_CKV // MLA_KV_LORA)),
                  pl.BlockSpec((tm, LANES), lambda i: (i, ZF_KPE // LANES)),
                  pl.BlockSpec((1, MLA_Q_LORA), lambda i: (0, 0)),
                  pl.BlockSpec((1, MLA_KV_LORA), lambda i: (0, 0)),
                  pl.BlockSpec(wq.shape, lambda i: (0, 0)),
                  pl.BlockSpec(wkv.shape, lambda i: (0, 0)),
                  pl.BlockSpec((tm, MLA_ROPE), lambda i: (i % ns, 0)),
                  pl.BlockSpec((tm, MLA_ROPE), lambda i: (i % ns, 0))],
        out_specs=[pl.BlockSpec((None, nh, tm, dqk), out_map),
                   pl.BlockSpec((None, nh, tm, dqk), out_map),
                   pl.BlockSpec((None, nh, tm, MLA_V), out_map)],
        out_shape=[jax.ShapeDtypeStruct((b, nh, s, dqk), BF16),
                   jax.ShapeDtypeStruct((b, nh, s, dqk), BF16),
                   jax.ShapeDtypeStruct((b, nh, s, MLA_V), BF16)],
        compiler_params=_params("parallel"),
        name="mla_proj",
    )(zf, zf, zf, q_gain.reshape(1, -1), kv_gain.reshape(1, -1), wq, wkv, cos, sin)


def _key_chunk(s):
    return min(256, s)


def _fill_v_ones(vaug_ref, v_ref):
    dv = v_ref.shape[1]
    vaug_ref[:, :dv] = v_ref[...]
    vaug_ref[:, dv:] = jnp.ones((v_ref.shape[0], vaug_ref.shape[1] - dv), vaug_ref.dtype)


def _softmax_pv(lhs, k_ref, vaug_ref, dv, bias_fn=None):
    chunk = _key_chunk(k_ref.shape[0])
    m = acc = None
    for c in range(k_ref.shape[0] // chunk):
        ks = slice(c * chunk, (c + 1) * chunk)
        sc = lax.dot_general(lhs, k_ref[ks, :], _NT, preferred_element_type=F32)
        if bias_fn is not None:
            sc = sc + bias_fn(c)
        top = jnp.max(sc, axis=-1, keepdims=True)
        m_new = top if m is None else jnp.maximum(m, top)
        pv = jnp.dot(jnp.exp2(sc - m_new).astype(BF16), vaug_ref[ks, :], preferred_element_type=F32)
        acc = pv if acc is None else acc * jnp.exp2(m - m_new) + pv
        m = m_new
    return acc[:, :dv] / acc[:, dv:]


def _mla_attn_kernel(q_ref, k_ref, v_ref, o_ref, vaug_ref):
    @pl.when(pl.program_id(2) == 0)
    def _():
        _fill_v_ones(vaug_ref, v_ref)

    o_ref[...] = _softmax_pv(q_ref[...], k_ref, vaug_ref, MLA_V).astype(o_ref.dtype)


def mla_attn(q, k, v, tq=4096):
    b, h, s, dqk = q.shape
    tq = min(tq, s)
    nq = s // tq
    return pl.pallas_call(
        _mla_attn_kernel,
        grid=(b, h, nq),
        in_specs=[pl.BlockSpec((None, None, tq, dqk), lambda b_, h_, i: (b_, h_, i, 0)),
                  pl.BlockSpec((None, None, s, dqk), lambda b_, h_, i: (b_, h_, 0, 0)),
                  pl.BlockSpec((None, None, s, MLA_V), lambda b_, h_, i: (b_, h_, 0, 0))],
        out_specs=pl.BlockSpec((tq, MLA_V), lambda b_, h_, i: (b_ * nq + i, h_)),
        out_shape=jax.ShapeDtypeStruct((b * s, h * MLA_V), F32),
        scratch_shapes=[pltpu.VMEM((s, 2 * MLA_V), BF16)],
        compiler_params=_params("parallel", "parallel", "arbitrary"),
        name="mla_attn",
    )(q, k, v)


def _diff_attn_kernel(q_ref, k_ref, v_ref, bias_ref, lv_ref, sub_ref, o_ref, vaug_ref, *, lam_init, rows):
    i = pl.program_id(2)
    tq = q_ref.shape[0]
    s = k_ref.shape[0]
    d = DIFF_HEAD_DIM

    @pl.when(i == 0)
    def _():
        _fill_v_ones(vaug_ref, v_ref)

    lv = lv_ref[...]
    lam = (jnp.exp(jnp.sum(lv[0:1] * lv[1:2], axis=-1, keepdims=True))
           - jnp.exp(jnp.sum(lv[2:3] * lv[3:4], axis=-1, keepdims=True)) + lam_init)
    q = q_ref[...]
    lane = lax.broadcasted_iota(jnp.int32, q.shape, 1)
    zero = jnp.zeros_like(q)
    lhs = jnp.concatenate([jnp.where(lane < d, q, zero), jnp.where(lane >= d, q, zero)], axis=0)

    def bias(c):
        chunk = _key_chunk(s)
        groups = [bias_ref[:, pl.ds(pl.multiple_of(s - rows - (i * tq + g * rows), rows) + c * chunk, chunk)]
                  for g in range(tq // rows)]
        return jnp.concatenate(groups + groups, axis=0)

    o = _softmax_pv(lhs, k_ref, vaug_ref, 2 * d, bias)
    o = o[:tq] - lam * o[tq:]
    o_ref[...] = (_rms(o, sub_ref[...]) * (1.0 - lam_init)).astype(o_ref.dtype)


def _alibi_table(n_heads, s, rows):
    slopes = 2.0 ** (-8.0 * jnp.arange(1, n_heads + 1, dtype=F32) / n_heads)
    r = jnp.arange(rows, dtype=jnp.int32)[:, None]
    c = jnp.arange(2 * s - rows, dtype=jnp.int32)[None, :]
    dist = jnp.abs(r - c + (s - rows)).astype(F32)
    return (-LOG2E * slopes)[:, None, None] * dist[None]


def diff_attn(zb, lam_vecs, subln, lam_init, bias_tbl, b, s, tq=2048):
    tq = min(tq, s)
    nq = s // tq
    hd = 2 * DIFF_HEAD_DIM
    rows, width = bias_tbl.shape[1:]
    return pl.pallas_call(
        functools.partial(_diff_attn_kernel, lam_init=lam_init, rows=rows),
        grid=(b, DIFF_HEADS, nq),
        in_specs=[pl.BlockSpec((tq, hd), lambda b_, h, i: (b_ * nq + i, ZB_DQ // hd + h)),
                  pl.BlockSpec((s, hd), lambda b_, h, i: (b_, ZB_DK // hd + h)),
                  pl.BlockSpec((s, hd), lambda b_, h, i: (b_, ZB_DV // hd + h)),
                  pl.BlockSpec((None, rows, width), lambda b_, h, i: (h, 0, 0)),
                  pl.BlockSpec((4, DIFF_HEAD_DIM), lambda b_, h, i: (0, 0)),
                  pl.BlockSpec((1, hd), lambda b_, h, i: (0, 0))],
        out_specs=pl.BlockSpec((tq, hd), lambda b_, h, i: (b_ * nq + i, h)),
        out_shape=jax.ShapeDtypeStruct((b * s, DIFF_HEADS * hd), F32),
        scratch_shapes=[pltpu.VMEM((s, 2 * hd), BF16)],
        compiler_params=_params("parallel", "parallel", "arbitrary"),
        name="diff_attn",
    )(zb, zb, zb, bias_tbl, lam_vecs, subln.reshape(1, hd))


def _swa_kernel(sink_ref, q_ref, k_ref, v_ref, o_ref, *, seq):
    w = SWA_WINDOW
    d = SWA_HEAD_DIM
    qb = q_ref.shape[0]
    band = qb + 2 * w
    n = pl.program_id(1)
    start = pl.multiple_of(jnp.clip(n * qb - w, 0, seq - band), w)
    kband = k_ref[pl.ds(start, band), :]
    vband = v_ref[pl.ds(start, band), :]
    qpos = n * qb + lax.broadcasted_iota(jnp.int32, (2 * qb, band), 0) % qb
    kpos = start + lax.broadcasted_iota(jnp.int32, (2 * qb, band), 1)
    rel = jnp.abs(qpos - kpos)
    neg_rel = jnp.where(rel <= w, -rel.astype(F32), -jnp.inf)
    lane = lax.broadcasted_iota(jnp.int32, (qb, 2 * d), 1)
    row = lax.broadcasted_iota(jnp.int32, (2 * qb, 1), 0)
    for pair in range(SWA_HEADS // 2):
        g = (2 * pair) // SWA_REP
        qp = q_ref[:, pair * 2 * d:(pair + 1) * 2 * d]
        zero = jnp.zeros_like(qp)
        qs = jnp.concatenate([jnp.where(lane < d, qp, zero), jnp.where(lane >= d, qp, zero)], axis=0)
        sc = lax.dot_general(qs, kband[:, g * 2 * d:(g + 1) * 2 * d], _NT, preferred_element_type=F32)
        h0 = 2 * pair
        slope = jnp.where(row < qb, 2.0 ** (-8.0 * (h0 + 1) / SWA_HEADS), 2.0 ** (-8.0 * (h0 + 2) / SWA_HEADS))
        sink = jnp.where(row < qb, sink_ref[h0], sink_ref[h0 + 1])
        sc = sc + slope * neg_rel
        m = jnp.maximum(jnp.max(sc, axis=-1, keepdims=True), sink)
        e = jnp.exp(sc - m)
        l = jnp.sum(e, axis=-1, keepdims=True) + jnp.exp(sink - m)
        p = (e * (1.0 / l)).astype(BF16)
        o = jnp.dot(p, vband[:, g * 2 * d:(g + 1) * 2 * d], preferred_element_type=F32)
        o_ref[:, pair * 2 * d:(pair + 1) * 2 * d] = jnp.where(lane < d, o[:qb], o[qb:]).astype(o_ref.dtype)


def swa_attn(zb, sink, b, s, qb=256):
    qb = min(qb, s // 2)
    nb = s // qb
    kvw = SWA_KV_HEADS * 2 * SWA_HEAD_DIM
    qw = SWA_HEADS * SWA_HEAD_DIM
    return pl.pallas_call(
        functools.partial(_swa_kernel, seq=s),
        grid=(b, nb),
        in_specs=[pl.BlockSpec(memory_space=pltpu.SMEM),
                  pl.BlockSpec((qb, qw), lambda b_, n: (b_ * nb + n, ZB_SQ // qw)),
                  pl.BlockSpec((s, kvw), lambda b_, n: (b_, ZB_SK // kvw)),
                  pl.BlockSpec((s, kvw), lambda b_, n: (b_, ZB_SV // kvw))],
        out_specs=pl.BlockSpec((qb, qw), lambda b_, n: (b_ * nb + n, 0)),
        out_shape=jax.ShapeDtypeStruct((b * s, qw), F32),
        compiler_params=_params("parallel", "arbitrary"),
        name="swa_attn",
    )(sink, zb, zb, zb)


def _conv_kernel(x_ref, w_ref, b_ref, o_ref, *, kscale):
    x = x_ref[...]
    s = x.shape[0]
    row = lax.broadcasted_iota(jnp.int32, x.shape, 0)
    pad = MLSTM_CONV // 2
    y = x * w_ref[pad:pad + 1, :] + b_ref[...]
    for j in range(MLSTM_CONV):
        off = j - pad
        if off == 0:
            continue
        shifted = pltpu.roll(x, (-off) % s, 0)
        ok = (row >= -off) if off < 0 else (row < s - off)
        y = y + jnp.where(ok, shifted, 0.0) * w_ref[j:j + 1, :]
    y = y * _sigmoid(y)
    y = y * jnp.where(pl.program_id(1) >= pl.num_programs(1) // 2, kscale, 1.0)
    o_ref[...] = y.astype(o_ref.dtype)


def mlstm_conv(zf, conv_w, conv_b, b, s, tc=256):
    c = conv_w.shape[1]
    nc = c // tc
    return pl.pallas_call(
        functools.partial(_conv_kernel, kscale=MLSTM_QK ** -0.5),
        grid=(b, nc),
        in_specs=[pl.BlockSpec((s, tc), lambda b_, j: (b_, ZF_MQK // tc + j)),
                  pl.BlockSpec((MLSTM_CONV, tc), lambda b_, j: (0, j)),
                  pl.BlockSpec((1, tc), lambda b_, j: (0, j))],
        out_specs=pl.BlockSpec((s, tc), lambda b_, j: (b_, j)),
        out_shape=jax.ShapeDtypeStruct((b * s, c), BF16),
        compiler_params=_params("parallel", "arbitrary"),
        name="mlstm_conv",
    )(zf, conv_w, conv_b.reshape(1, c))


def _scan(x, row, op, fill, reverse):
    s = x.shape[0]
    d = 1
    while d < s:
        if reverse:
            shifted = jnp.where(row < s - d, pltpu.roll(x, s - d, 0), fill)
        else:
            shifted = jnp.where(row >= d, pltpu.roll(x, d, 0), fill)
        x = op(x, shifted)
        d *= 2
    return x


def _gate_kernel(g_ref, gb_ref, col_ref, row_ref):
    nh = MLSTM_HEADS
    g = g_ref[...] + gb_ref[...]
    ls = jnp.minimum(g, 0.0) - jnp.log1p(jnp.exp(-jnp.abs(g)))
    row = lax.broadcasted_iota(jnp.int32, g.shape, 0)
    lane = lax.broadcasted_iota(jnp.int32, g.shape, 1)
    cum_f = pltpu.roll(_scan(ls, row, jnp.add, 0.0, False), LANES - nh, 1)
    cum_b = pltpu.roll(_scan(ls, row, jnp.add, 0.0, True), LANES - nh, 1)
    a_f = g - cum_f
    a_b = g - cum_b
    top_f = _scan(a_f, row, jnp.maximum, -jnp.inf, False)
    top_b = _scan(a_b, row, jnp.maximum, -jnp.inf, True)
    m_f = cum_f + top_f
    m_b = cum_b + top_b
    src_t = (jnp.where(lane < 2 * nh, a_f, a_b) * LOG2E).T
    sub = lax.broadcasted_iota(jnp.int32, row_ref.shape[1:], 0)
    for h in range(nh):
        hb = 2 * nh + h
        col_ref[h] = jnp.where(lane == 0, top_f[:, h:h + 1] * LOG2E,
                     jnp.where(lane == 1, m_f[:, h:h + 1],
                     jnp.where(lane == 2, top_b[:, hb:hb + 1] * LOG2E,
                     jnp.where(lane == 3, m_b[:, hb:hb + 1],
                     jnp.where(lane == 4, a_f[:, h:h + 1] * LOG2E, a_b[:, hb:hb + 1] * LOG2E)))))
        row_ref[h] = jnp.where(sub == 0, src_t[h:h + 1, :],
                     jnp.where(sub == 1, src_t[2 * nh + h:2 * nh + h + 1, :], 0.0))


def mlstm_gates(zf, gate_b, b, s):
    gb = jnp.zeros((1, LANES), F32).at[0, :4 * MLSTM_HEADS].set(gate_b.reshape(-1))
    nh = MLSTM_HEADS
    return pl.pallas_call(
        _gate_kernel,
        grid=(b,),
        in_specs=[pl.BlockSpec((s, LANES), lambda b_: (b_, ZF_GATES // LANES)),
                  pl.BlockSpec((1, LANES), lambda b_: (0, 0))],
        out_specs=[pl.BlockSpec((None, nh, s, LANES), lambda b_: (b_, 0, 0, 0)),
                   pl.BlockSpec((None, nh, 8, s), lambda b_: (b_, 0, 0, 0))],
        out_shape=[jax.ShapeDtypeStruct((b, nh, s, LANES), F32),
                   jax.ShapeDtypeStruct((b, nh, 8, s), F32)],
        compiler_params=_params("parallel"),
        name="mlstm_gates",
    )(zf, gb)


MLSTM_SCAN_CHUNK = 512


def _mlstm_scan_kernel(q_ref, k_ref, v_ref, col_ref, row_ref, og_ref, o_ref, hf_ref):
    s = q_ref.shape[0]
    ch = min(MLSTM_SCAN_CHUNK, s)
    nc = s // ch
    tpos = lax.broadcasted_iota(jnp.int32, (ch, ch), 0)
    spos = lax.broadcasted_iota(jnp.int32, (ch, ch), 1)

    def sweep(order, lane0, src_row, mask, edge_row, emit):
        state = zsum = edge = None
        for n, j in enumerate(order):
            rows = slice(j * ch, (j + 1) * ch)
            q, k, v = q_ref[rows, :], k_ref[rows, :], v_ref[rows, :]
            col = col_ref[rows, :]
            top = col[:, lane0:lane0 + 1]
            sc = lax.dot_general(q, k, _NT, preferred_element_type=F32)
            wgt = jnp.where(mask, jnp.exp2(row_ref[src_row:src_row + 1, rows] - top), 0.0) * sc
            num = jnp.dot(wgt.astype(BF16), v, preferred_element_type=F32)
            den = jnp.sum(wgt, axis=-1, keepdims=True)
            if state is not None:
                carry = jnp.exp2(edge - top)
                num = num + carry * jnp.dot(q, state.astype(BF16), preferred_element_type=F32)
                den = den + carry * jnp.sum(q.astype(F32) * zsum, axis=-1, keepdims=True)
            emit(rows, num / jnp.maximum(jnp.abs(den), jnp.exp(-col[:, lane0 + 1:lane0 + 2])))
            if n + 1 < nc:
                new_edge = col_ref[edge_row(j):edge_row(j) + 1, lane0:lane0 + 1]
                a_col = col[:, lane0 // 2 + 4:lane0 // 2 + 5]
                kw = k.astype(F32) * jnp.exp2(a_col - new_edge)
                upd = jnp.dot(kw.T.astype(BF16), v, preferred_element_type=F32)
                zupd = jnp.sum(kw.astype(BF16).astype(F32), axis=0, keepdims=True)
                if state is None:
                    state, zsum = upd, zupd
                else:
                    decay = jnp.exp2(edge - new_edge)
                    state, zsum = decay * state + upd, decay * zsum + zupd
                edge = new_edge

    def emit_forward(rows, h):
        hf_ref[rows, :] = h

    def emit_backward(rows, h):
        o_ref[rows, :] = (_sigmoid(og_ref[rows, :]) * (hf_ref[rows, :] + h)).astype(o_ref.dtype)

    sweep(range(nc), 0, 0, tpos >= spos, lambda j: (j + 1) * ch - 1, emit_forward)
    sweep(range(nc - 1, -1, -1), 2, 1, tpos <= spos, lambda j: j * ch, emit_backward)


def mlstm_attn(qk, zb, zf, col, row, b, s):
    nh, dk, dv = MLSTM_HEADS, MLSTM_QK, MLSTM_V
    return pl.pallas_call(
        _mlstm_scan_kernel,
        grid=(b, nh),
        in_specs=[pl.BlockSpec((s, dk), lambda b_, h: (b_, h)),
                  pl.BlockSpec((s, dk), lambda b_, h: (b_, nh + h)),
                  pl.BlockSpec((s, dv), lambda b_, h: (b_, ZB_MV // dv + h)),
                  pl.BlockSpec((None, None, s, LANES), lambda b_, h: (b_, h, 0, 0)),
                  pl.BlockSpec((None, None, 8, s), lambda b_, h: (b_, h, 0, 0)),
                  pl.BlockSpec((s, dv), lambda b_, h: (b_, ZF_MO // dv + h))],
        out_specs=pl.BlockSpec((s, dv), lambda b_, h: (b_, h)),
        out_shape=jax.ShapeDtypeStruct((b * s, nh * dv), F32),
        scratch_shapes=[pltpu.VMEM((s, dv), F32)],
        compiler_params=_params("parallel", "arbitrary"),
        name="mlstm_attn",
    )(qk, qk, zb, col, row, zf)


def _rot_half_cols(w):
    half = w.shape[-1] // 2
    return jnp.concatenate([-w[..., half:], w[..., :half]], axis=-1)


def _mla_weights(w_uq, w_ukv):
    wq = w_uq.reshape(MLA_Q_LORA, MLA_HEADS, MLA_NOPE + MLA_ROPE)
    pe = wq[..., MLA_NOPE:]
    wq = jnp.concatenate([wq[..., :MLA_NOPE], pe, _rot_half_cols(pe)], axis=-1)
    return wq.reshape(MLA_Q_LORA, -1).astype(BF16), w_ukv.astype(BF16)


def _rope_tables(s):
    half = MLA_ROPE // 2
    inv = ROPE_THETA ** (-jnp.arange(half, dtype=F32) / half)
    ang = jnp.arange(s).astype(F32)[:, None] * inv[None, :]
    cos, sin = jnp.cos(ang), jnp.sin(ang)
    return jnp.concatenate([cos, cos], axis=-1), jnp.concatenate([sin, sin], axis=-1)


def mixers(zf, zb, l, b, s, cos, sin, bias_tbl, mla_q_norm, mla_kv_norm, mla_w_uq, mla_w_ukv, mlstm_conv_w,
           mlstm_conv_b, mlstm_gate_b, diff_lambda, diff_subln, swa_sink):
    wq, wkv = _mla_weights(mla_w_uq, mla_w_ukv)
    q, k, v = mla_proj(zf, mla_q_norm, mla_kv_norm, wq, wkv, cos, sin, b, s)
    y_a = mla_attn(q, k, v)
    qk = mlstm_conv(zf, mlstm_conv_w, mlstm_conv_b, b, s)
    col, row = mlstm_gates(zf, mlstm_gate_b, b, s)
    y_b = mlstm_attn(qk, zb, zf, col, row, b, s)
    lam_init = 0.8 - 0.6 * math.exp(-0.3 * l)
    y_c = diff_attn(zb, diff_lambda, diff_subln, lam_init, bias_tbl, b, s)
    y_d = swa_attn(zb, swa_sink, b, s)
    return y_a, y_b, y_c, y_d


def kernel(x, norm_gains, w_in, mla_q_norm, mla_kv_norm, mla_w_uq, mla_w_ukv, mlstm_conv_w,
           mlstm_conv_b, mlstm_gate_b, diff_lambda, diff_subln, swa_sink, group_norm, w_out,
           ffn1_w_gu, ffn1_w_down, ffn2_w_gu, ffn2_w_down):
    b, s, d = x.shape
    depth = w_in.shape[0]
    cos, sin = _rope_tables(s)
    bias_tbl = _alibi_table(DIFF_HEADS, s, ATTN_ROWS)
    w_down1, w_down2, w_o = cast_bf16(ffn1_w_down), cast_bf16(ffn2_w_down), cast_bf16(w_out)
    w_in_t = jnp.swapaxes(w_in, 1, 2)
    x = x.reshape(b * s, d)
    xn = prenorm(x, norm_gains[0, 0])
    for l in range(depth):
        g = norm_gains[l]
        h = swiglu_up(xn, ffn1_w_gu, l)
        x, xn = resid_update(h, w_down1, l, x, g[1], g[2], 0.5)
        wf, wb = w_in_layout(w_in_t, l)
        zf = matmul_nt(xn, wf, F32, 1024, 768, "proj_f32")
        zb = matmul_nt(xn, wb, BF16, 1024, 512, "proj_bf16")
        ys = mixers(zf, zb, l, b, s, cos, sin, bias_tbl, mla_q_norm[l], mla_kv_norm[l], mla_w_uq[l], mla_w_ukv[l],
                    mlstm_conv_w[l], mlstm_conv_b[l], mlstm_gate_b[l], diff_lambda[l], diff_subln[l],
                    swa_sink[l])
        yn = group_rmsnorm(*ys, group_norm[l])
        x, xn = resid_update(yn, w_o, l, x, g[3], g[4], 1.0)
        h = swiglu_up(xn, ffn2_w_gu, l)
        g_next = norm_gains[l + 1, 0] if l + 1 < depth else g[5]
        x, xn = resid_update(h, w_down2, l, x, g[5], g_next, 0.5)
    return x.reshape(b, s, d)
```

```python
import functools
import math

import jax
import jax.numpy as jnp
from jax import lax
from jax.experimental import pallas as pl
from jax.experimental.pallas import tpu as pltpu

F32 = jnp.float32
BF16 = jnp.bfloat16
NORM_EPS = 1e-6
ROPE_THETA = 10000.0
LOG2E = math.log2(math.e)
LANES = 128
ATTN_ROWS = 128
VMEM_LIMIT_BYTES = 60 * 1024 * 1024

D_MODEL = 4096
GROUP_WIDTH = D_MODEL // 4
D_FF = (3 * D_MODEL) // 2
MLA_HEADS = GROUP_WIDTH // 128
MLA_Q_LORA = D_MODEL // 4
MLA_KV_LORA = D_MODEL // 8
MLA_NOPE = 128
MLA_ROPE = 64
MLA_V = GROUP_WIDTH // MLA_HEADS
MLSTM_HEADS = 4
MLSTM_V = GROUP_WIDTH // MLSTM_HEADS
MLSTM_QK = MLSTM_V // 2
MLSTM_CONV = 5
DIFF_HEADS = 8
DIFF_HEAD_DIM = GROUP_WIDTH // (2 * DIFF_HEADS)
SWA_HEADS = 16
SWA_KV_HEADS = 2
SWA_HEAD_DIM = GROUP_WIDTH // SWA_HEADS
SWA_WINDOW = 128
SWA_REP = SWA_HEADS // SWA_KV_HEADS

MLA_Q_SCALE = (MLA_NOPE + MLA_ROPE) ** -0.5 * LOG2E
DIFF_Q_SCALE = DIFF_HEAD_DIM ** -0.5 * LOG2E

MLA_COLS = MLA_Q_LORA + MLA_KV_LORA + MLA_ROPE
MLSTM_COLS = 2 * MLSTM_HEADS * MLSTM_QK + 2 * MLSTM_HEADS * MLSTM_V + 4 * MLSTM_HEADS
DIFF_COLS = 3 * DIFF_HEADS * 2 * DIFF_HEAD_DIM
SWA_COLS = (SWA_HEADS + 2 * SWA_KV_HEADS) * SWA_HEAD_DIM

ZF_CQ = 0
ZF_MQK = 1024
ZF_MO = 2048
ZF_CKV = 3072
ZF_KPE = 3584
ZF_GATES = 3712
ZF_COLS = 3840
ZB_MV = 0
ZB_DQ = 1024
ZB_DK = 2048
ZB_DV = 3072
ZB_SQ = 4096
ZB_SK = 5120
ZB_SV = 5376
ZB_COLS = 5632


def _params(*sem):
    return pltpu.CompilerParams(dimension_semantics=sem, vmem_limit_bytes=VMEM_LIMIT_BYTES)


def _rms(x, g):
    return x * lax.rsqrt(jnp.mean(x * x, axis=-1, keepdims=True) + NORM_EPS) * g


def _sigmoid(x):
    return 1.0 / (1.0 + jnp.exp(-x))


_NT = (((1,), (1,)), ((), ()))


def _prenorm_kernel(x_ref, g_ref, o_ref):
    o_ref[...] = _rms(x_ref[...], g_ref[...]).astype(o_ref.dtype)


def prenorm(x, g, tm=512):
    t, d = x.shape
    return pl.pallas_call(
        _prenorm_kernel,
        grid=(t // tm,),
        in_specs=[pl.BlockSpec((tm, d), lambda i: (i, 0)), pl.BlockSpec((1, d), lambda i: (0, 0))],
        out_specs=pl.BlockSpec((tm, d), lambda i: (i, 0)),
        out_shape=jax.ShapeDtypeStruct((t, d), BF16),
        compiler_params=_params("parallel"),
        name="prenorm",
    )(x, g.reshape(1, d))


def _matmul_kernel(a_ref, w_ref, o_ref):
    o_ref[...] = lax.dot_general(a_ref[...], w_ref[...], _NT, preferred_element_type=F32).astype(o_ref.dtype)


def matmul_nt(a, w_t, out_dtype, tm, tn, name):
    m, k = a.shape
    n = w_t.shape[0]
    return pl.pallas_call(
        _matmul_kernel,
        grid=(m // tm, n // tn),
        in_specs=[pl.BlockSpec((tm, k), lambda i, j: (i, 0)), pl.BlockSpec((tn, k), lambda i, j: (j, 0))],
        out_specs=pl.BlockSpec((tm, tn), lambda i, j: (i, j)),
        out_shape=jax.ShapeDtypeStruct((m, n), out_dtype),
        compiler_params=_params("parallel", "arbitrary"),
        name=name,
    )(a, w_t)


def _swiglu_kernel(a_ref, wg_ref, wu_ref, o_ref, wgb_ref, wub_ref):
    @pl.when(pl.program_id(1) == 0)
    def _():
        wgb_ref[...] = wg_ref[...].astype(BF16)
        wub_ref[...] = wu_ref[...].astype(BF16)

    half = a_ref.shape[0] // 2
    for r in range(2):
        rows = slice(r * half, (r + 1) * half)
        a = a_ref[rows, :]
        g = jnp.dot(a, wgb_ref[...], preferred_element_type=F32)
        u = jnp.dot(a, wub_ref[...], preferred_element_type=F32)
        o_ref[rows, :] = (g * _sigmoid(g) * u).astype(o_ref.dtype)


def swiglu_up(a, w_gu, l, tm=2048, tn=256):
    m, k = a.shape
    f = w_gu.shape[2] // 2
    nj = f // tn
    return pl.pallas_call(
        _swiglu_kernel,
        grid=(nj, m // tm),
        in_specs=[pl.BlockSpec((tm, k), lambda j, i: (i, 0)),
                  pl.BlockSpec((None, k, tn), lambda j, i: (l, 0, j)),
                  pl.BlockSpec((None, k, tn), lambda j, i: (l, 0, j + nj))],
        out_specs=pl.BlockSpec((tm, tn), lambda j, i: (i, j)),
        out_shape=jax.ShapeDtypeStruct((m, f), BF16),
        scratch_shapes=[pltpu.VMEM((k, tn), BF16), pltpu.VMEM((k, tn), BF16)],
        compiler_params=_params("parallel", "arbitrary"),
        name="swiglu_up",
    )(a, w_gu, w_gu)


def _cast_kernel(x_ref, o_ref):
    o_ref[...] = x_ref[...].astype(o_ref.dtype)


def cast_bf16(w, tr=512):
    nl, r, c = w.shape
    spec = pl.BlockSpec((None, tr, c), lambda l, i: (l, i, 0))
    return pl.pallas_call(
        _cast_kernel,
        grid=(nl, r // tr),
        in_specs=[spec],
        out_specs=spec,
        out_shape=jax.ShapeDtypeStruct(w.shape, BF16),
        compiler_params=_params("parallel", "parallel"),
        name="cast_bf16",
    )(w)


def _w_in_pieces():
    a0, b0 = 0, MLA_COLS
    c0 = b0 + MLSTM_COLS
    d0 = c0 + DIFF_COLS
    qk_w = 2 * MLSTM_HEADS * MLSTM_QK
    v_w = MLSTM_HEADS * MLSTM_V
    dq_w = DIFF_HEADS * 2 * DIFF_HEAD_DIM
    sq_w = SWA_HEADS * SWA_HEAD_DIM
    hd = SWA_HEAD_DIM
    kpe = a0 + MLA_Q_LORA + MLA_KV_LORA
    half = MLA_ROPE // 2
    f = [(ZF_CQ, a0, MLA_Q_LORA, 1.0), (ZF_MQK, b0, qk_w, 1.0), (ZF_MO, b0 + qk_w + v_w, v_w, 1.0),
         (ZF_CKV, a0 + MLA_Q_LORA, MLA_KV_LORA, 1.0), (ZF_KPE, kpe, MLA_ROPE, 1.0),
         (ZF_KPE + MLA_ROPE, kpe + half, half, -1.0), (ZF_KPE + MLA_ROPE + half, kpe, half, 1.0),
         (ZF_GATES, b0 + qk_w + 2 * v_w, 4 * MLSTM_HEADS, 1.0)]
    b = [(ZB_MV, b0 + qk_w, v_w, 1.0), (ZB_DQ, c0, dq_w, DIFF_Q_SCALE), (ZB_DK, c0 + dq_w, 2 * dq_w, 1.0),
         (ZB_SQ, d0, sq_w, SWA_HEAD_DIM ** -0.5)]
    for g in range(SWA_KV_HEADS):
        for rep in range(2):
            b.append((ZB_SK + (2 * g + rep) * hd, d0 + sq_w + g * hd, hd, 1.0))
            b.append((ZB_SV + (2 * g + rep) * hd, d0 + sq_w + SWA_KV_HEADS * hd + g * hd, hd, 1.0))
    return [(0,) + p for p in f] + [(1,) + p for p in b]


def _w_in_kernel(w_ref, wf_ref, wb_ref):
    pad0 = ZF_GATES + 4 * MLSTM_HEADS
    wf_ref[pad0:, :] = jnp.zeros((ZF_COLS - pad0, wf_ref.shape[1]), wf_ref.dtype)
    for grp, dst, src, width, scale in _w_in_pieces():
        piece = w_ref[src:src + width, :]
        if scale != 1.0:
            piece = piece * scale
        (wb_ref if grp else wf_ref)[dst:dst + width, :] = piece.astype(BF16)


def w_in_layout(w_in_t, l, tk=256):
    c, k = w_in_t.shape[1:]
    return pl.pallas_call(
        _w_in_kernel,
        grid=(k // tk,),
        in_specs=[pl.BlockSpec((None, c, tk), lambda i: (l, 0, i))],
        out_specs=[pl.BlockSpec((ZF_COLS, tk), lambda i: (0, i)), pl.BlockSpec((ZB_COLS, tk), lambda i: (0, i))],
        out_shape=[jax.ShapeDtypeStruct((ZF_COLS, k), BF16), jax.ShapeDtypeStruct((ZB_COLS, k), BF16)],
        compiler_params=_params("parallel"),
        name="w_in_layout",
    )(w_in_t)


RESID_STEPS = 8


def _resid_kernel(a_ref, w_ref, x_ref, gp_ref, gn_ref, xo_ref, xn_ref, acc0_ref, acc1_ref, *, coef, n_tiles):
    i = pl.program_id(0)
    k = pl.program_id(1)
    rs = x_ref.shape[0]

    def epilogue(prev_ref):
        rows = pl.ds(pl.multiple_of(k * rs, rs), rs)
        xnew = x_ref[...] + coef * _rms(prev_ref[rows, :], gp_ref[...])
        xo_ref[...] = xnew
        xn_ref[...] = _rms(xnew, gn_ref[...]).astype(xn_ref.dtype)
        prev_ref[rows, :] = jnp.zeros((rs, prev_ref.shape[1]), F32)

    @pl.when((i == 0) & (k == 0))
    def _():
        acc0_ref[...] = jnp.zeros_like(acc0_ref)
        acc1_ref[...] = jnp.zeros_like(acc1_ref)

    for parity, (cur_ref, prev_ref) in enumerate(((acc0_ref, acc1_ref), (acc1_ref, acc0_ref))):
        @pl.when((i % 2 == parity) & (i < n_tiles))
        def _(cur_ref=cur_ref, prev_ref=prev_ref):
            epilogue(prev_ref)
            cur_ref[...] += jnp.dot(a_ref[...], w_ref[...], preferred_element_type=F32)

        @pl.when((i % 2 == parity) & (i == n_tiles))
        def _(prev_ref=prev_ref):
            epilogue(prev_ref)


def resid_update(a, w, l, x, g_post, g_next, coef, tm=1024):
    m, kk = a.shape
    d = w.shape[2]
    n_tiles = m // tm
    tk = kk // RESID_STEPS
    rs = tm // RESID_STEPS
    row_map = lambda i, k: (jnp.where(i == 0, 0, (i - 1) * RESID_STEPS + k), 0)
    return pl.pallas_call(
        functools.partial(_resid_kernel, coef=coef, n_tiles=n_tiles),
        grid=(n_tiles + 1, RESID_STEPS),
        in_specs=[pl.BlockSpec((tm, tk), lambda i, k: (jnp.minimum(i, n_tiles - 1), k)),
                  pl.BlockSpec((None, tk, d), lambda i, k: (l, k, 0)),
                  pl.BlockSpec((rs, d), row_map),
                  pl.BlockSpec((1, d), lambda i, k: (0, 0)),
                  pl.BlockSpec((1, d), lambda i, k: (0, 0))],
        out_specs=[pl.BlockSpec((rs, d), row_map), pl.BlockSpec((rs, d), row_map)],
        out_shape=[jax.ShapeDtypeStruct((m, d), F32), jax.ShapeDtypeStruct((m, d), BF16)],
        scratch_shapes=[pltpu.VMEM((tm, d), F32), pltpu.VMEM((tm, d), F32)],
        compiler_params=_params("arbitrary", "arbitrary"),
        name="resid_update",
    )(a, w, x, g_post.reshape(1, d), g_next.reshape(1, d))


def _group_norm_kernel(a_ref, b_ref, c_ref, d_ref, g_ref, o_ref):
    w = a_ref.shape[1]
    for n, r in enumerate((a_ref, b_ref, c_ref, d_ref)):
        o_ref[:, n * w:(n + 1) * w] = _rms(r[...], g_ref[:, n * w:(n + 1) * w]).astype(o_ref.dtype)


def group_rmsnorm(ya, yb, yc, yd, g, tm=512):
    t, w = ya.shape
    spec = pl.BlockSpec((tm, w), lambda i: (i, 0))
    return pl.pallas_call(
        _group_norm_kernel,
        grid=(t // tm,),
        in_specs=[spec, spec, spec, spec, pl.BlockSpec((1, 4 * w), lambda i: (0, 0))],
        out_specs=pl.BlockSpec((tm, 4 * w), lambda i: (i, 0)),
        out_shape=jax.ShapeDtypeStruct((t, 4 * w), BF16),
        compiler_params=_params("parallel"),
        name="group_norm",
    )(ya, yb, yc, yd, g.reshape(1, 4 * w))


def _mla_proj_kernel(cq_ref, ckv_ref, kpe_ref, qg_ref, kvg_ref, wq_ref, wkv_ref, cos_ref, sin_ref,
                     q_ref, k_ref, v_ref):
    cos = cos_ref[...]
    sin = sin_ref[...]
    kp = kpe_ref[...]
    k_rope = (kp[:, :MLA_ROPE] * cos + kp[:, MLA_ROPE:] * sin).astype(k_ref.dtype)
    qq = jnp.dot(_rms(cq_ref[...], qg_ref[...]).astype(BF16), wq_ref[...],
                 preferred_element_type=F32) * MLA_Q_SCALE
    kv = jnp.dot(_rms(ckv_ref[...], kvg_ref[...]).astype(BF16), wkv_ref[...], preferred_element_type=F32)
    hw = 2 * LANES
    for h in range(MLA_HEADS):
        qh = qq[:, h * hw:(h + 1) * hw]
        q_ref[h, :, :MLA_NOPE] = qh[:, :MLA_NOPE].astype(q_ref.dtype)
        q_ref[h, :, MLA_NOPE:] = (qh[:, MLA_NOPE:MLA_NOPE + MLA_ROPE] * cos
                                  + qh[:, MLA_NOPE + MLA_ROPE:] * sin).astype(q_ref.dtype)
        k_ref[h, :, :MLA_NOPE] = kv[:, h * hw:h * hw + MLA_NOPE].astype(k_ref.dtype)
        k_ref[h, :, MLA_NOPE:] = k_rope
        v_ref[h] = kv[:, h * hw + MLA_NOPE:(h + 1) * hw].astype(v_ref.dtype)


def mla_proj(zf, q_gain, kv_gain, wq, wkv, cos, sin, b, s, tm=512):
    ns = s // tm
    nh = MLA_HEADS
    dqk = MLA_NOPE + MLA_ROPE
    out_map = lambda i: (i // ns, 0, i % ns, 0)
    return pl.pallas_call(
        _mla_proj_kernel,
        grid=(b * ns,),
        in_specs=[pl.BlockSpec((tm, MLA_Q_LORA), lambda i: (i, ZF_CQ // MLA_Q_LORA)),
                  pl.BlockSpec((tm, MLA_KV_LORA), lambda i: (i, ZF_CKV // MLA_KV_LORA)),
                  pl.BlockSpec((tm, LANES), lambda i: (i, ZF_KPE // LANES)),
                  pl.BlockSpec((1, MLA_Q_LORA), lambda i: (0, 0)),
                  pl.BlockSpec((1, MLA_KV_LORA), lambda i: (0, 0)),
                  pl.BlockSpec(wq.shape, lambda i: (0, 0)),
                  pl.BlockSpec(wkv.shape, lambda i: (0, 0)),
                  pl.BlockSpec((tm, MLA_ROPE), lambda i: (i % ns, 0)),
                  pl.BlockSpec((tm, MLA_ROPE), lambda i: (i % ns, 0))],
        out_specs=[pl.BlockSpec((None, nh, tm, dqk), out_map),
                   pl.BlockSpec((None, nh, tm, dqk), out_map),
                   pl.BlockSpec((None, nh, tm, MLA_V), out_map)],
        out_shape=[jax.ShapeDtypeStruct((b, nh, s, dqk), BF16),
                   jax.ShapeDtypeStruct((b, nh, s, dqk), BF16),
                   jax.ShapeDtypeStruct((b, nh, s, MLA_V), BF16)],
        compiler_params=_params("parallel"),
        name="mla_proj",
    )(zf, zf, zf, q_gain.reshape(1, -1), kv_gain.reshape(1, -1), wq, wkv, cos, sin)


def _key_chunk(s):
    return min(256, s)


def _fill_v_ones(vaug_ref, v_ref):
    dv = v_ref.shape[1]
    vaug_ref[:, :dv] = v_ref[...]
    vaug_ref[:, dv:] = jnp.ones((v_ref.shape[0], vaug_ref.shape[1] - dv), vaug_ref.dtype)


def _softmax_pv(lhs, k_ref, vaug_ref, dv, bias_fn=None):
    chunk = _key_chunk(k_ref.shape[0])
    m = acc = None
    for c in range(k_ref.shape[0] // chunk):
        ks = slice(c * chunk, (c + 1) * chunk)
        sc = lax.dot_general(lhs, k_ref[ks, :], _NT, preferred_element_type=F32)
        if bias_fn is not None:
            sc = sc + bias_fn(c)
        top = jnp.max(sc, axis=-1, keepdims=True)
        m_new = top if m is None else jnp.maximum(m, top)
        pv = jnp.dot(jnp.exp2(sc - m_new).astype(BF16), vaug_ref[ks, :], preferred_element_type=F32)
        acc = pv if acc is None else acc * jnp.exp2(m - m_new) + pv
        m = m_new
    return acc[:, :dv] / acc[:, dv:]


def _mla_attn_kernel(q_ref, k_ref, v_ref, o_ref, vaug_ref):
    @pl.when(pl.program_id(2) == 0)
    def _():
        _fill_v_ones(vaug_ref, v_ref)

    o_ref[...] = _softmax_pv(q_ref[...], k_ref, vaug_ref, MLA_V).astype(o_ref.dtype)


def mla_attn(q, k, v, tq=4096):
    b, h, s, dqk = q.shape
    tq = min(tq, s)
    nq = s // tq
    return pl.pallas_call(
        _mla_attn_kernel,
        grid=(b, h, nq),
        in_specs=[pl.BlockSpec((None, None, tq, dqk), lambda b_, h_, i: (b_, h_, i, 0)),
                  pl.BlockSpec((None, None, s, dqk), lambda b_, h_, i: (b_, h_, 0, 0)),
                  pl.BlockSpec((None, None, s, MLA_V), lambda b_, h_, i: (b_, h_, 0, 0))],
        out_specs=pl.BlockSpec((tq, MLA_V), lambda b_, h_, i: (b_ * nq + i, h_)),
        out_shape=jax.ShapeDtypeStruct((b * s, h * MLA_V), F32),
        scratch_shapes=[pltpu.VMEM((s, 2 * MLA_V), BF16)],
        compiler_params=_params("parallel", "parallel", "arbitrary"),
        name="mla_attn",
    )(q, k, v)


def _diff_attn_kernel(q_ref, k_ref, v_ref, bias_ref, lv_ref, sub_ref, o_ref, vaug_ref, *, lam_init, rows):
    i = pl.program_id(2)
    tq = q_ref.shape[0]
    s = k_ref.shape[0]
    d = DIFF_HEAD_DIM

    @pl.when(i == 0)
    def _():
        _fill_v_ones(vaug_ref, v_ref)

    lv = lv_ref[...]
    lam = (jnp.exp(jnp.sum(lv[0:1] * lv[1:2], axis=-1, keepdims=True))
           - jnp.exp(jnp.sum(lv[2:3] * lv[3:4], axis=-1, keepdims=True)) + lam_init)
    q = q_ref[...]
    lane = lax.broadcasted_iota(jnp.int32, q.shape, 1)
    zero = jnp.zeros_like(q)
    lhs = jnp.concatenate([jnp.where(lane < d, q, zero), jnp.where(lane >= d, q, zero)], axis=0)

    def bias(c):
        chunk = _key_chunk(s)
        groups = [bias_ref[:, pl.ds(pl.multiple_of(s - rows - (i * tq + g * rows), rows) + c * chunk, chunk)]
                  for g in range(tq // rows)]
        return jnp.concatenate(groups + groups, axis=0)

    o = _softmax_pv(lhs, k_ref, vaug_ref, 2 * d, bias)
    o = o[:tq] - lam * o[tq:]
    o_ref[...] = (_rms(o, sub_ref[...]) * (1.0 - lam_init)).astype(o_ref.dtype)


def _alibi_table(n_heads, s, rows):
    slopes = 2.0 ** (-8.0 * jnp.arange(1, n_heads + 1, dtype=F32) / n_heads)
    r = jnp.arange(rows, dtype=jnp.int32)[:, None]
    c = jnp.arange(2 * s - rows, dtype=jnp.int32)[None, :]
    dist = jnp.abs(r - c + (s - rows)).astype(F32)
    return (-LOG2E * slopes)[:, None, None] * dist[None]


def diff_attn(zb, lam_vecs, subln, lam_init, bias_tbl, b, s, tq=2048):
    tq = min(tq, s)
    nq = s // tq
    hd = 2 * DIFF_HEAD_DIM
    rows, width = bias_tbl.shape[1:]
    return pl.pallas_call(
        functools.partial(_diff_attn_kernel, lam_init=lam_init, rows=rows),
        grid=(b, DIFF_HEADS, nq),
        in_specs=[pl.BlockSpec((tq, hd), lambda b_, h, i: (b_ * nq + i, ZB_DQ // hd + h)),
                  pl.BlockSpec((s, hd), lambda b_, h, i: (b_, ZB_DK // hd + h)),
                  pl.BlockSpec((s, hd), lambda b_, h, i: (b_, ZB_DV // hd + h)),
                  pl.BlockSpec((None, rows, width), lambda b_, h, i: (h, 0, 0)),
                  pl.BlockSpec((4, DIFF_HEAD_DIM), lambda b_, h, i: (0, 0)),
                  pl.BlockSpec((1, hd), lambda b_, h, i: (0, 0))],
        out_specs=pl.BlockSpec((tq, hd), lambda b_, h, i: (b_ * nq + i, h)),
        out_shape=jax.ShapeDtypeStruct((b * s, DIFF_HEADS * hd), F32),
        scratch_shapes=[pltpu.VMEM((s, 2 * hd), BF16)],
        compiler_params=_params("parallel", "parallel", "arbitrary"),
        name="diff_attn",
    )(zb, zb, zb, bias_tbl, lam_vecs, subln.reshape(1, hd))


def _swa_kernel(sink_ref, q_ref, k_ref, v_ref, o_ref, *, seq):
    w = SWA_WINDOW
    d = SWA_HEAD_DIM
    qb = q_ref.shape[0]
    band = qb + 2 * w
    n = pl.program_id(1)
    start = pl.multiple_of(jnp.clip(n * qb - w, 0, seq - band), w)
    kband = k_ref[pl.ds(start, band), :]
    vband = v_ref[pl.ds(start, band), :]
    qpos = n * qb + lax.broadcasted_iota(jnp.int32, (2 * qb, band), 0) % qb
    kpos = start + lax.broadcasted_iota(jnp.int32, (2 * qb, band), 1)
    rel = jnp.abs(qpos - kpos)
    neg_rel = jnp.where(rel <= w, -rel.astype(F32), -jnp.inf)
    lane = lax.broadcasted_iota(jnp.int32, (qb, 2 * d), 1)
    row = lax.broadcasted_iota(jnp.int32, (2 * qb, 1), 0)
    for pair in range(SWA_HEADS // 2):
        g = (2 * pair) // SWA_REP
        qp = q_ref[:, pair * 2 * d:(pair + 1) * 2 * d]
        zero = jnp.zeros_like(qp)
        qs = jnp.concatenate([jnp.where(lane < d, qp, zero), jnp.where(lane >= d, qp, zero)], axis=0)
        sc = lax.dot_general(qs, kband[:, g * 2 * d:(g + 1) * 2 * d], _NT, preferred_element_type=F32)
        h0 = 2 * pair
        slope = jnp.where(row < qb, 2.0 ** (-8.0 * (h0 + 1) / SWA_HEADS), 2.0 ** (-8.0 * (h0 + 2) / SWA_HEADS))
        sink = jnp.where(row < qb, sink_ref[h0], sink_ref[h0 + 1])
        sc = sc + slope * neg_rel
        m = jnp.maximum(jnp.max(sc, axis=-1, keepdims=True), sink)
        e = jnp.exp(sc - m)
        l = jnp.sum(e, axis=-1, keepdims=True) + jnp.exp(sink - m)
        p = (e * (1.0 / l)).astype(BF16)
        o = jnp.dot(p, vband[:, g * 2 * d:(g + 1) * 2 * d], preferred_element_type=F32)
        o_ref[:, pair * 2 * d:(pair + 1) * 2 * d] = jnp.where(lane < d, o[:qb], o[qb:]).astype(o_ref.dtype)


def swa_attn(zb, sink, b, s, qb=256):
    qb = min(qb, s // 2)
    nb = s // qb
    kvw = SWA_KV_HEADS * 2 * SWA_HEAD_DIM
    qw = SWA_HEADS * SWA_HEAD_DIM
    return pl.pallas_call(
        functools.partial(_swa_kernel, seq=s),
        grid=(b, nb),
        in_specs=[pl.BlockSpec(memory_space=pltpu.SMEM),
                  pl.BlockSpec((qb, qw), lambda b_, n: (b_ * nb + n, ZB_SQ // qw)),
                  pl.BlockSpec((s, kvw), lambda b_, n: (b_, ZB_SK // kvw)),
                  pl.BlockSpec((s, kvw), lambda b_, n: (b_, ZB_SV // kvw))],
        out_specs=pl.BlockSpec((qb, qw), lambda b_, n: (b_ * nb + n, 0)),
        out_shape=jax.ShapeDtypeStruct((b * s, qw), F32),
        compiler_params=_params("parallel", "arbitrary"),
        name="swa_attn",
    )(sink, zb, zb, zb)


def _conv_kernel(x_ref, w_ref, b_ref, o_ref, *, kscale):
    x = x_ref[...]
    s = x.shape[0]
    row = lax.broadcasted_iota(jnp.int32, x.shape, 0)
    pad = MLSTM_CONV // 2
    y = x * w_ref[pad:pad + 1, :] + b_ref[...]
    for j in range(MLSTM_CONV):
        off = j - pad
        if off == 0:
            continue
        shifted = pltpu.roll(x, (-off) % s, 0)
        ok = (row >= -off) if off < 0 else (row < s - off)
        y = y + jnp.where(ok, shifted, 0.0) * w_ref[j:j + 1, :]
    y = y * _sigmoid(y)
    y = y * jnp.where(pl.program_id(1) >= pl.num_programs(1) // 2, kscale, 1.0)
    o_ref[...] = y.astype(o_ref.dtype)


def mlstm_conv(zf, conv_w, conv_b, b, s, tc=256):
    c = conv_w.shape[1]
    nc = c // tc
    return pl.pallas_call(
        functools.partial(_conv_kernel, kscale=MLSTM_QK ** -0.5),
        grid=(b, nc),
        in_specs=[pl.BlockSpec((s, tc), lambda b_, j: (b_, ZF_MQK // tc + j)),
                  pl.BlockSpec((MLSTM_CONV, tc), lambda b_, j: (0, j)),
                  pl.BlockSpec((1, tc), lambda b_, j: (0, j))],
        out_specs=pl.BlockSpec((s, tc), lambda b_, j: (b_, j)),
        out_shape=jax.ShapeDtypeStruct((b * s, c), BF16),
        compiler_params=_params("parallel", "arbitrary"),
        name="mlstm_conv",
    )(zf, conv_w, conv_b.reshape(1, c))


def _scan(x, row, op, fill, reverse):
    s = x.shape[0]
    d = 1
    while d < s:
        if reverse:
            shifted = jnp.where(row < s - d, pltpu.roll(x, s - d, 0), fill)
        else:
            shifted = jnp.where(row >= d, pltpu.roll(x, d, 0), fill)
        x = op(x, shifted)
        d *= 2
    return x


def _gate_kernel(g_ref, gb_ref, col_ref, row_ref):
    nh = MLSTM_HEADS
    g = g_ref[...] + gb_ref[...]
    ls = jnp.minimum(g, 0.0) - jnp.log1p(jnp.exp(-jnp.abs(g)))
    row = lax.broadcasted_iota(jnp.int32, g.shape, 0)
    lane = lax.broadcasted_iota(jnp.int32, g.shape, 1)
    cum_f = pltpu.roll(_scan(ls, row, jnp.add, 0.0, False), LANES - nh, 1)
    cum_b = pltpu.roll(_scan(ls, row, jnp.add, 0.0, True), LANES - nh, 1)
    a_f = g - cum_f
    a_b = g - cum_b
    top_f = _scan(a_f, row, jnp.maximum, -jnp.inf, False)
    top_b = _scan(a_b, row, jnp.maximum, -jnp.inf, True)
    m_f = cum_f + top_f
    m_b = cum_b + top_b
    src_t = (jnp.where(lane < 2 * nh, a_f, a_b) * LOG2E).T
    sub = lax.broadcasted_iota(jnp.int32, row_ref.shape[1:], 0)
    for h in range(nh):
        hb = 2 * nh + h
        col_ref[h] = jnp.where(lane == 0, top_f[:, h:h + 1] * LOG2E,
                     jnp.where(lane == 1, m_f[:, h:h + 1],
                     jnp.where(lane == 2, top_b[:, hb:hb + 1] * LOG2E,
                     jnp.where(lane == 3, m_b[:, hb:hb + 1],
                     jnp.where(lane == 4, a_f[:, h:h + 1] * LOG2E, a_b[:, hb:hb + 1] * LOG2E)))))
        row_ref[h] = jnp.where(sub == 0, src_t[h:h + 1, :],
                     jnp.where(sub == 1, src_t[2 * nh + h:2 * nh + h + 1, :], 0.0))


def mlstm_gates(zf, gate_b, b, s):
    gb = jnp.zeros((1, LANES), F32).at[0, :4 * MLSTM_HEADS].set(gate_b.reshape(-1))
    nh = MLSTM_HEADS
    return pl.pallas_call(
        _gate_kernel,
        grid=(b,),
        in_specs=[pl.BlockSpec((s, LANES), lambda b_: (b_, ZF_GATES // LANES)),
                  pl.BlockSpec((1, LANES), lambda b_: (0, 0))],
        out_specs=[pl.BlockSpec((None, nh, s, LANES), lambda b_: (b_, 0, 0, 0)),
                   pl.BlockSpec((None, nh, 8, s), lambda b_: (b_, 0, 0, 0))],
        out_shape=[jax.ShapeDtypeStruct((b, nh, s, LANES), F32),
                   jax.ShapeDtypeStruct((b, nh, 8, s), F32)],
        compiler_params=_params("parallel"),
        name="mlstm_gates",
    )(zf, gb)


MLSTM_SCAN_CHUNK = 512


def _mlstm_scan_kernel(q_ref, k_ref, v_ref, col_ref, row_ref, og_ref, o_ref, hf_ref):
    s = q_ref.shape[0]
    ch = min(MLSTM_SCAN_CHUNK, s)
    nc = s // ch
    tpos = lax.broadcasted_iota(jnp.int32, (ch, ch), 0)
    spos = lax.broadcasted_iota(jnp.int32, (ch, ch), 1)

    def sweep(order, lane0, src_row, mask, edge_row, emit):
        state = zsum = edge = None
        for n, j in enumerate(order):
            rows = slice(j * ch, (j + 1) * ch)
            q, k, v = q_ref[rows, :], k_ref[rows, :], v_ref[rows, :]
            col = col_ref[rows, :]
            top = col[:, lane0:lane0 + 1]
            sc = lax.dot_general(q, k, _NT, preferred_element_type=F32)
            wgt = jnp.where(mask, jnp.exp2(row_ref[src_row:src_row + 1, rows] - top), 0.0) * sc
            num = jnp.dot(wgt.astype(BF16), v, preferred_element_type=F32)
            den = jnp.sum(wgt, axis=-1, keepdims=True)
            if state is not None:
                carry = jnp.exp2(edge - top)
                num = num + carry * jnp.dot(q, state.astype(BF16), preferred_element_type=F32)
                den = den + carry * jnp.sum(q.astype(F32) * zsum, axis=-1, keepdims=True)
            emit(rows, num / jnp.maximum(jnp.abs(den), jnp.exp(-col[:, lane0 + 1:lane0 + 2])))
            if n + 1 < nc:
                new_edge = col_ref[edge_row(j):edge_row(j) + 1, lane0:lane0 + 1]
                a_col = col[:, lane0 // 2 + 4:lane0 // 2 + 5]
                kw = k.astype(F32) * jnp.exp2(a_col - new_edge)
                upd = lax.dot_general(kw.astype(BF16), v, (((0,), (0,)), ((), ())),
                                      preferred_element_type=F32)
                zupd = jnp.sum(kw.astype(BF16).astype(F32), axis=0, keepdims=True)
                if state is None:
                    state, zsum = upd, zupd
                else:
                    decay = jnp.exp2(edge - new_edge)
                    state, zsum = decay * state + upd, decay * zsum + zupd
                edge = new_edge

    def emit_forward(rows, h):
        hf_ref[rows, :] = h

    def emit_backward(rows, h):
        o_ref[rows, :] = (_sigmoid(og_ref[rows, :]) * (hf_ref[rows, :] + h)).astype(o_ref.dtype)

    sweep(range(nc), 0, 0, tpos >= spos, lambda j: (j + 1) * ch - 1, emit_forward)
    sweep(range(nc - 1, -1, -1), 2, 1, tpos <= spos, lambda j: j * ch, emit_backward)


def mlstm_attn(qk, zb, zf, col, row, b, s):
    nh, dk, dv = MLSTM_HEADS, MLSTM_QK, MLSTM_V
    return pl.pallas_call(
        _mlstm_scan_kernel,
        grid=(b, nh),
        in_specs=[pl.BlockSpec((s, dk), lambda b_, h: (b_, h)),
                  pl.BlockSpec((s, dk), lambda b_, h: (b_, nh + h)),
                  pl.BlockSpec((s, dv), lambda b_, h: (b_, ZB_MV // dv + h)),
                  pl.BlockSpec((None, None, s, LANES), lambda b_, h: (b_, h, 0, 0)),
                  pl.BlockSpec((None, None, 8, s), lambda b_, h: (b_, h, 0, 0)),
                  pl.BlockSpec((s, dv), lambda b_, h: (b_, ZF_MO // dv + h))],
        out_specs=pl.BlockSpec((s, dv), lambda b_, h: (b_, h)),
        out_shape=jax.ShapeDtypeStruct((b * s, nh * dv), F32),
        scratch_shapes=[pltpu.VMEM((s, dv), F32)],
        compiler_params=_params("parallel", "arbitrary"),
        name="mlstm_attn",
    )(qk, qk, zb, col, row, zf)


def _rot_half_cols(w):
    half = w.shape[-1] // 2
    return jnp.concatenate([-w[..., half:], w[..., :half]], axis=-1)


def _mla_weights(w_uq, w_ukv):
    wq = w_uq.reshape(MLA_Q_LORA, MLA_HEADS, MLA_NOPE + MLA_ROPE)
    pe = wq[..., MLA_NOPE:]
    wq = jnp.concatenate([wq[..., :MLA_NOPE], pe, _rot_half_cols(pe)], axis=-1)
    return wq.reshape(MLA_Q_LORA, -1).astype(BF16), w_ukv.astype(BF16)


def _rope_tables(s):
    half = MLA_ROPE // 2
    inv = ROPE_THETA ** (-jnp.arange(half, dtype=F32) / half)
    ang = jnp.arange(s).astype(F32)[:, None] * inv[None, :]
    cos, sin = jnp.cos(ang), jnp.sin(ang)
    return jnp.concatenate([cos, cos], axis=-1), jnp.concatenate([sin, sin], axis=-1)


def mixers(zf, zb, l, b, s, cos, sin, bias_tbl, mla_q_norm, mla_kv_norm, mla_w_uq, mla_w_ukv, mlstm_conv_w,
           mlstm_conv_b, mlstm_gate_b, diff_lambda, diff_subln, swa_sink):
    wq, wkv = _mla_weights(mla_w_uq, mla_w_ukv)
    q, k, v = mla_proj(zf, mla_q_norm, mla_kv_norm, wq, wkv, cos, sin, b, s)
    y_a = mla_attn(q, k, v)
    qk = mlstm_conv(zf, mlstm_conv_w, mlstm_conv_b, b, s)
    col, row = mlstm_gates(zf, mlstm_gate_b, b, s)
    y_b = mlstm_attn(qk, zb, zf, col, row, b, s)
    lam_init = 0.8 - 0.6 * math.exp(-0.3 * l)
    y_c = diff_attn(zb, diff_lambda, diff_subln, lam_init, bias_tbl, b, s)
    y_d = swa_attn(zb, swa_sink, b, s)
    return y_a, y_b, y_c, y_d


def kernel(x, norm_gains, w_in, mla_q_norm, mla_kv_norm, mla_w_uq, mla_w_ukv, mlstm_conv_w,
           mlstm_conv_b, mlstm_gate_b, diff_lambda, diff_subln, swa_sink, group_norm, w_out,
           ffn1_w_gu, ffn1_w_down, ffn2_w_gu, ffn2_w_down):
    b, s, d = x.shape
    depth = w_in.shape[0]
    cos, sin = _rope_tables(s)
    bias_tbl = _alibi_table(DIFF_HEADS, s, ATTN_ROWS)
    w_down1, w_down2, w_o = cast_bf16(ffn1_w_down), cast_bf16(ffn2_w_down), cast_bf16(w_out)
    w_in_t = jnp.swapaxes(w_in, 1, 2)
    x = x.reshape(b * s, d)
    xn = prenorm(x, norm_gains[0, 0])
    for l in range(depth):
        g = norm_gains[l]
        h = swiglu_up(xn, ffn1_w_gu, l)
        x, xn = resid_update(h, w_down1, l, x, g[1], g[2], 0.5)
        wf, wb = w_in_layout(w_in_t, l)
        zf = matmul_nt(xn, wf, F32, 1024, 768, "proj_f32")
        zb = matmul_nt(xn, wb, BF16, 1024, 512, "proj_bf16")
        ys = mixers(zf, zb, l, b, s, cos, sin, bias_tbl, mla_q_norm[l], mla_kv_norm[l], mla_w_uq[l], mla_w_ukv[l],
                    mlstm_conv_w[l], mlstm_conv_b[l], mlstm_gate_b[l], diff_lambda[l], diff_subln[l],
                    swa_sink[l])
        yn = group_rmsnorm(*ys, group_norm[l])
        x, xn = resid_update(yn, w_o, l, x, g[3], g[4], 1.0)
        h = swiglu_up(xn, ffn2_w_gu, l)
        g_next = norm_gains[l + 1, 0] if l + 1 < depth else g[5]
        x, xn = resid_update(h, w_down2, l, x, g[5], g_next, 0.5)
    return x.reshape(b, s, d)
```
